```python
import math
import jax, jax.numpy as jnp
from jax import lax
import numpy as np

D_MODEL = 1024
BATCH = 8
SEQ = 4096
DEPTH = 2

GDN_HEADS = 4
GDN_DK = 128
GDN_DV = 128
GDN_CONV = 4
GDN_CHUNK = 64
DSWA_GROUPS = ((128, 1), (512, 4), (2048, 16))
DSWA_HEADS_PER_GROUP = 3
DSWA_HEADS = 9
DSWA_DH = 64
DSWA_BLOCK = 128
ROPE_THETA = 500000.0
ROPE_DIM = DSWA_DH // 4
SB_HEADS = 8
SB_DH = 64
SB_BLOCK = 128
RET_HEADS = 4
RET_DK = 64
RET_DV = 128
RET_CHUNK = 64
RET_THETA = 10000.0
N_BRANCH = 4
D_FF = 4 * D_MODEL
NORM_EPS = 1e-6

COL_SIZES = (
    2 * GDN_HEADS * GDN_DK + GDN_HEADS * GDN_DV,
    GDN_HEADS * GDN_DV,
    GDN_HEADS,
    GDN_HEADS,
    3 * DSWA_HEADS * DSWA_DH,
    3 * SB_HEADS * SB_DH,
    2 * RET_HEADS * RET_DK,
    RET_HEADS * RET_DV,
    RET_HEADS * RET_DV,
    N_BRANCH * D_MODEL,
)
N_IN_COLS = sum(COL_SIZES)

kernel_name = 'hybrid_gated_four_mixer_block'


def rmsnorm(x, w, eps=NORM_EPS):
    xf = x.astype(jnp.float32)
    y = xf * lax.rsqrt(jnp.mean(xf * xf, axis=-1, keepdims=True) + eps)
    return (y * w.astype(jnp.float32)).astype(x.dtype)


def l2norm(x, eps=1e-6):
    xf = x.astype(jnp.float32)
    return xf * lax.rsqrt(jnp.sum(xf * xf, axis=-1, keepdims=True) + eps)


def split_cols(t, sizes):
    return jnp.split(t, list(np.cumsum(sizes)[:-1]), axis=-1)


def rope_tables(positions, inv_freq):
    ang = positions.astype(jnp.float32)[:, None] * inv_freq[None, :]
    return jnp.cos(ang), jnp.sin(ang)


def apply_rope(x, cos, sin):
    half = cos.shape[-1]
    r = 2 * half
    c = cos[:, None, :].astype(x.dtype)
    s = sin[:, None, :].astype(x.dtype)
    x1, x2, rest = x[..., :half], x[..., half:r], x[..., r:]
    return jnp.concatenate([x1 * c - x2 * s, x2 * c + x1 * s, rest], axis=-1)


def causal_dwconv(x, w):
    K, C = w.shape
    return lax.conv_general_dilated(
        x, w[:, None, :].astype(x.dtype), window_strides=(1,), padding=((K - 1, 0),),
        dimension_numbers=('NWC', 'WIO', 'NWC'), feature_group_count=C)


def gated_delta_chunked(q, k, v, beta, g):
    f32 = jnp.float32
    B, H, S, DK = q.shape
    DV = v.shape[-1]
    C = GDN_CHUNK
    N = S // C
    q = q.reshape(B, H, N, C, DK).astype(f32)
    k = k.reshape(B, H, N, C, DK).astype(f32)
    v = v.reshape(B, H, N, C, DV).astype(f32)
    beta = beta.reshape(B, H, N, C).astype(f32)
    gc = jnp.cumsum(g.reshape(B, H, N, C).astype(f32), axis=-1)
    idx = jnp.arange(C)
    incl = idx[:, None] >= idx[None, :]
    strict = idx[:, None] > idx[None, :]
    decay = jnp.exp(jnp.where(incl, gc[..., :, None] - gc[..., None, :], -jnp.inf))
    k_beta = k * beta[..., None]
    v_beta = v * beta[..., None]
    lower = jnp.where(strict, jnp.einsum('bhncd,bhnsd->bhncs', k_beta, k) * decay, 0.0)
    eye = jnp.eye(C, dtype=f32)
    t_inv = lax.linalg.triangular_solve(eye + lower, jnp.broadcast_to(eye, lower.shape),
                                        left_side=True, lower=True, unit_diagonal=True)
    u = jnp.einsum('bhncs,bhnsd->bhncd', t_inv, v_beta)
    w = jnp.einsum('bhncs,bhnsd->bhncd', t_inv, k_beta * jnp.exp(gc)[..., None])
    attn = jnp.einsum('bhncd,bhnsd->bhncs', q, k) * decay
    q_dec = q * jnp.exp(gc)[..., None]
    g_last = gc[..., -1]
    k_dec = k * jnp.exp(g_last[..., None] - gc)[..., None]

    def step(state, inp):
        q_c, k_c, u_c, w_c, a_c, gl = inp
        v_new = u_c - jnp.einsum('bhcd,bhde->bhce', w_c, state)
        o = jnp.einsum('bhcd,bhde->bhce', q_c, state) + jnp.einsum('bhcs,bhse->bhce', a_c, v_new)
        state = state * jnp.exp(gl)[..., None, None] + jnp.einsum('bhcd,bhce->bhde', k_c, v_new)
        return state, o

    xs = tuple(jnp.moveaxis(t, 2, 0) for t in (q_dec, k_dec, u, w, attn, g_last))
    _, o = lax.scan(step, jnp.zeros((B, H, DK, DV), f32), xs)
    return jnp.moveaxis(o, 0, 2).reshape(B, H, S, DV)


def dilated_window_attn(q, k, v, dilation, lookback):
    B, S, H, D = q.shape
    L = S // dilation
    BL = DSWA_BLOCK
    nb = -(-L // BL)
    Lp = nb * BL

    def strided(t):
        t = t.reshape(B, L, dilation, H, D).transpose(0, 2, 3, 1, 4)
        t = jnp.pad(t, ((0, 0), (0, 0), (0, 0), (0, Lp - L), (0, 0)))
        return t.reshape(B, dilation, H, nb, BL, D)

    def with_prev(t):
        prev = jnp.pad(t, ((0, 0), (0, 0), (0, 0), (1, 0), (0, 0), (0, 0)))[:, :, :, :-1]
        return jnp.concatenate([prev, t], axis=4)

    qb = strided(q)
    kk = with_prev(strided(k))
    vv = with_prev(strided(v))
    s = jnp.einsum('bghnqd,bghnkd->bghnqk', qb, kk).astype(jnp.float32)
    blk = jnp.arange(nb)[:, None]
    qi = blk * BL + jnp.arange(BL)[None, :]
    kj = (blk - 1) * BL + jnp.arange(2 * BL)[None, :]
    rel = qi[:, :, None] - kj[:, None, :]
    mask = (rel >= 0) & (rel <= lookback) & (kj[:, None, :] >= 0)
    s = jnp.where(mask, s, -jnp.inf)
    lse = jax.nn.logsumexp(s, axis=-1)
    p = jnp.exp(s - lse[..., None])
    o = jnp.einsum('bghnqk,bghnkd->bghnqd', p.astype(v.dtype), vv)
    o = o.reshape(B, dilation, H, Lp, D)[:, :, :, :L].transpose(0, 3, 1, 2, 4).reshape(B, S, H, D)
    lse = lse.reshape(B, dilation, H, Lp)[..., :L].transpose(0, 3, 1, 2).reshape(B, S, H)
    return o, lse


def stick_breaking_attn(q, k, v):
    B, H, S, D = q.shape
    nb = S // SB_BLOCK
    qb = q.reshape(B, H, nb, SB_BLOCK, D).transpose(2, 0, 1, 3, 4)
    kpos = jnp.arange(S)

    def block(args):
        q_blk, n = args
        z = jnp.einsum('bhqd,bhkd->bhqk', q_blk, k).astype(jnp.float32)
        qpos = n * SB_BLOCK + jnp.arange(SB_BLOCK)
        mask = kpos[None, :] < qpos[:, None]
        log_one_minus = jnp.where(mask, jax.nn.log_sigmoid(-z), 0.0)
        suffix = lax.cumsum(log_one_minus, axis=3, reverse=True) - log_one_minus
        a = jnp.where(mask, jnp.exp(jax.nn.log_sigmoid(z) + suffix), 0.0)
        return jnp.einsum('bhqk,bhkd->bhqd', a.astype(v.dtype), v)

    o = lax.map(block, (qb, jnp.arange(nb)))
    return o.transpose(1, 2, 0, 3, 4).reshape(B, H, S, D)


def retention_chunked(q, k, v, log_gamma):
    f32 = jnp.float32
    B, H, S, DK = q.shape
    DV = v.shape[-1]
    C = RET_CHUNK
    N = S // C
    q = q.reshape(B, H, N, C, DK).astype(f32)
    k = k.reshape(B, H, N, C, DK).astype(f32)
    v = v.reshape(B, H, N, C, DV).astype(f32)
    idx = jnp.arange(C, dtype=f32)
    diff = idx[:, None] - idx[None, :]
    lg = log_gamma[:, None, None]
    dmat = jnp.exp(jnp.where(diff[None] >= 0, diff[None] * lg, -jnp.inf))
    intra = jnp.einsum('bhncd,bhnsd->bhncs', q, k) * dmat[:, None]
    intra_o = jnp.einsum('bhncs,bhnse->bhnce', intra, v)
    xi = jnp.exp((idx + 1.0)[None, :] * log_gamma[:, None])
    zeta = jnp.exp((C - 1.0 - idx)[None, :] * log_gamma[:, None])
    chunk_decay = jnp.exp(C * log_gamma)
    kz = k * zeta[None, :, None, :, None]

    def step(r, inp):
        q_c, kz_c, v_c = inp
        cross = jnp.einsum('bhcd,bhde->bhce', q_c, r)
        r = r * chunk_decay[None, :, None, None] + jnp.einsum('bhcd,bhce->bhde', kz_c, v_c)
        return r, cross

    xs = tuple(jnp.moveaxis(t, 2, 0) for t in (q, kz, v))
    _, cross = lax.scan(step, jnp.zeros((B, H, DK, DV), f32), xs)
    o = intra_o + jnp.moveaxis(cross, 0, 2) * xi[None, :, None, :, None]
    return o.reshape(B, H, S, DV)


def hybrid_layer(x, rope_cs, ret_cs, n_pre_mix, n_post_mix, n_pre_mlp, n_post_mlp, w_in, conv_w,
                 a_log, dt_bias, gdn_norm, ret_norm, w_br_a, w_br_b, w_br_c, w_br_d, w_o,
                 w_mlp_in, w_mlp_out):
    f32 = jnp.float32
    B, S, _ = x.shape

    def to_heads(t, n_heads):
        return t.reshape(B, S, n_heads, -1).transpose(0, 2, 1, 3)

    h = rmsnorm(x, n_pre_mix)
    proj = h @ w_in
    (gdn_qkv, gdn_z, gdn_a, gdn_b, dswa_qkv, sb_qkv, ret_qk, ret_v, ret_g, gates) = split_cols(proj, COL_SIZES)

    qkv = jax.nn.silu(causal_dwconv(gdn_qkv, conv_w))
    qa, ka, va = jnp.split(qkv, [GDN_HEADS * GDN_DK, 2 * GDN_HEADS * GDN_DK], axis=-1)
    qa = l2norm(to_heads(qa, GDN_HEADS)) * GDN_DK ** -0.5
    ka = l2norm(to_heads(ka, GDN_HEADS))
    va = to_heads(va, GDN_HEADS)
    beta = jax.nn.sigmoid(gdn_b.astype(f32)).transpose(0, 2, 1)
    g = (-jnp.exp(a_log.astype(f32)) * jax.nn.softplus(gdn_a.astype(f32) + dt_bias.astype(f32))).transpose(0, 2, 1)
    oa = gated_delta_chunked(qa, ka, va, beta, g).transpose(0, 2, 1, 3)
    oa = rmsnorm(oa, gdn_norm) * jax.nn.silu(gdn_z.reshape(B, S, GDN_HEADS, GDN_DV).astype(f32))
    y_a = oa.reshape(B, S, -1).astype(x.dtype) @ w_br_a

    qb, kb, vb = [t.reshape(B, S, DSWA_HEADS, DSWA_DH) for t in jnp.split(dswa_qkv, 3, axis=-1)]
    cos_p, sin_p = rope_cs
    qb = apply_rope(qb, cos_p, sin_p) * DSWA_DH ** -0.5
    kb = apply_rope(kb, cos_p, sin_p)
    outs, lses = [], []
    for gi, (window, dil) in enumerate(DSWA_GROUPS):
        hs = slice(gi * DSWA_HEADS_PER_GROUP, (gi + 1) * DSWA_HEADS_PER_GROUP)
        o_g, lse_g = dilated_window_attn(qb[:, :, hs], kb[:, :, hs], vb[:, :, hs], dil, window // dil)
        outs.append(o_g)
        lses.append(lse_g)
    alpha = jax.nn.softmax(jnp.stack(lses, axis=2), axis=2)
    ob = (jnp.stack(outs, axis=2) * alpha[..., None]).reshape(B, S, -1).astype(x.dtype)
    y_b = ob @ w_br_b

    qc, kc, vc = [to_heads(t, SB_HEADS) for t in jnp.split(sb_qkv, 3, axis=-1)]
    oc = stick_breaking_attn(qc * SB_DH ** -0.5, kc, vc).transpose(0, 2, 1, 3).reshape(B, S, -1)
    y_c = oc @ w_br_c

    cos_r, sin_r = ret_cs
    qd, kd = jnp.split(ret_qk, 2, axis=-1)
    qd = apply_rope(qd.reshape(B, S, RET_HEADS, RET_DK), cos_r, sin_r)
    kd = apply_rope(kd.reshape(B, S, RET_HEADS, RET_DK), cos_r, sin_r) * RET_DK ** -0.5
    vd = ret_v.reshape(B, S, RET_HEADS, RET_DV)
    log_gamma = jnp.log1p(-jnp.exp2(-5.0 - jnp.arange(RET_HEADS, dtype=f32)))
    od = retention_chunked(qd.transpose(0, 2, 1, 3), kd.transpose(0, 2, 1, 3),
                           vd.transpose(0, 2, 1, 3), log_gamma).transpose(0, 2, 1, 3)
    od = rmsnorm(od, ret_norm) * jax.nn.silu(ret_g.reshape(B, S, RET_HEADS, RET_DV).astype(f32))
    y_d = od.reshape(B, S, -1).astype(x.dtype) @ w_br_d

    g_a, g_b, g_c, g_d = jnp.split(jax.nn.sigmoid(gates), N_BRANCH, axis=-1)
    mixed = (g_a * y_a + g_b * y_b + g_c * y_c + g_d * y_d) @ w_o
    x = x + rmsnorm(mixed, n_post_mix)

    h2 = rmsnorm(x, n_pre_mlp)
    f = jnp.square(jax.nn.relu(h2 @ w_mlp_in)) @ w_mlp_out
    return x + rmsnorm(f, n_post_mlp)


def setup_inputs(seed: int = 0) -> dict:
    key = jax.random.key(seed)
    ks = jax.random.split(key, 20)
    f32 = jnp.float32

    def dense(k, shape, fan_in):
        return jax.random.normal(k, shape, f32) * fan_in ** -0.5

    def gain(k, shape):
        return 1.0 + 0.05 * jax.random.normal(k, shape, f32)

    dt = jnp.exp(jax.random.uniform(ks[8], (DEPTH, GDN_HEADS), f32, math.log(1e-3), math.log(1e-1)))
    return {
        'x': jax.random.normal(ks[0], (BATCH, SEQ, D_MODEL), f32),
        'norm_pre_mix': gain(ks[1], (DEPTH, D_MODEL)),
        'norm_post_mix': gain(ks[2], (DEPTH, D_MODEL)),
        'norm_pre_mlp': gain(ks[3], (DEPTH, D_MODEL)),
        'norm_post_mlp': gain(ks[4], (DEPTH, D_MODEL)),
        'w_in': dense(ks[5], (DEPTH, D_MODEL, N_IN_COLS), D_MODEL),
        'conv_w': dense(ks[6], (DEPTH, GDN_CONV, COL_SIZES[0]), GDN_CONV),
        'a_log': jnp.log(jax.random.uniform(ks[7], (DEPTH, GDN_HEADS), f32, 1.0, 16.0)),
        'dt_bias': dt + jnp.log(-jnp.expm1(-dt)),
        'gdn_norm': gain(ks[9], (DEPTH, GDN_DV)),
        'ret_norm': gain(ks[10], (DEPTH, RET_DV)),
        'w_br_a': dense(ks[11], (DEPTH, GDN_HEADS * GDN_DV, D_MODEL), GDN_HEADS * GDN_DV),
        'w_br_b': dense(ks[12], (DEPTH, DSWA_HEADS * DSWA_DH, D_MODEL), DSWA_HEADS * DSWA_DH),
        'w_br_c': dense(ks[13], (DEPTH, SB_HEADS * SB_DH, D_MODEL), SB_HEADS * SB_DH),
        'w_br_d': dense(ks[14], (DEPTH, RET_HEADS * RET_DV, D_MODEL), RET_HEADS * RET_DV),
        'w_o': dense(ks[15], (DEPTH, D_MODEL, D_MODEL), D_MODEL),
        'w_mlp_in': dense(ks[16], (DEPTH, D_MODEL, D_FF), D_MODEL),
        'w_mlp_out': dense(ks[17], (DEPTH, D_FF, D_MODEL), D_FF),
    }


def reference(x, norm_pre_mix, norm_post_mix, norm_pre_mlp, norm_post_mlp, w_in, conv_w, a_log,
              dt_bias, gdn_norm, ret_norm, w_br_a, w_br_b, w_br_c, w_br_d, w_o, w_mlp_in, w_mlp_out):
    f32 = jnp.float32
    pos = jnp.arange(x.shape[1], dtype=jnp.int32)
    rope_cs = rope_tables(pos, ROPE_THETA ** (-jnp.arange(0, ROPE_DIM, 2, dtype=f32) / ROPE_DIM))
    ret_cs = rope_tables(pos, RET_THETA ** (-jnp.linspace(0.0, 1.0, RET_DK // 2, dtype=f32)))
    for l in range(DEPTH):
        x = hybrid_layer(x, rope_cs, ret_cs, norm_pre_mix[l], norm_post_mix[l], norm_pre_mlp[l],
                         norm_post_mlp[l], w_in[l], conv_w[l], a_log[l], dt_bias[l], gdn_norm[l],
                         ret_norm[l], w_br_a[l], w_br_b[l], w_br_c[l], w_br_d[l], w_o[l],
                         w_mlp_in[l], w_mlp_out[l])
    return x
```

```python
import functools
import math

import numpy as np
import jax
import jax.numpy as jnp
from jax import lax
from jax.experimental import pallas as pl
from jax.experimental.pallas import tpu as pltpu

F32 = jnp.float32
BF16 = jnp.bfloat16

D_MODEL = 1024
N_LAYERS = 2
GDN_HEADS, GDN_DK, GDN_DV, GDN_CONV, GDN_CHUNK = 4, 128, 128, 4, 64
DSWA_GROUPS = ((128, 1), (512, 4), (2048, 16))
DSWA_HPG, DSWA_DH, DSWA_BLOCK = 3, 64, 128
DSWA_HEADS = DSWA_HPG * len(DSWA_GROUPS)
DSWA_ROPE_THETA, DSWA_ROPE_DIM = 500000.0, DSWA_DH // 4
SB_HEADS, SB_DH, SB_BLOCK = 8, 64, 128
RET_HEADS, RET_DK, RET_DV, RET_THETA = 4, 64, 128, 10000.0
RET_CHUNK = 128
D_FF = 4 * D_MODEL
NORM_EPS = 1e-6
L2_EPS = 1e-6

_COL_SIZES = (1536, 512, 4, 4, 1728, 1536, 512, 512, 512, 4096)
_COL_OFF = np.concatenate([[0], np.cumsum(_COL_SIZES)])

PM_A_QKV, PM_A_Z, PM_C, PM_D_QK, PM_D_V, PM_D_G, PM_GATES, PM_WIDTH = 0, 1536, 2048, 3584, 4096, 4608, 5120, 9216
DSWA_GW = 640
DSWA_OW = 256

VMEM_LIMIT = 48 * 1024 * 1024
LANES = 128

_RET_LOG_GAMMA = [float(np.log1p(-np.exp2(np.float32(-5.0 - h))).astype(np.float32)) for h in range(RET_HEADS)]


def _cparams(*sem):
    return pltpu.CompilerParams(dimension_semantics=sem, vmem_limit_bytes=VMEM_LIMIT)


def _sigmoid(x):
    return 1.0 / (1.0 + jnp.exp(-x))


def _softplus(x):
    return jnp.maximum(x, 0.0) + jnp.log(1.0 + jnp.exp(-jnp.abs(x)))


def _dot(a, b):
    return jnp.dot(a, b, preferred_element_type=F32)


def _dot_nt(a, b):
    return lax.dot_general(a, b, (((1,), (1,)), ((), ())), preferred_element_type=F32)


def _split3(x):
    hi = x.astype(BF16)
    r = x - hi.astype(F32)
    mid = r.astype(BF16)
    lo = (r - mid.astype(F32)).astype(BF16)
    return hi, mid, lo


def _dot_exact_lhs(a_bf16, x):
    hi, mid, lo = _split3(x)
    return _dot(a_bf16, hi) + (_dot(a_bf16, mid) + _dot(a_bf16, lo))


def _dot_f32(a, b):
    ah, am, al = _split3(a)
    bh, bm, bl = _split3(b)
    return (_dot(ah, bh) + (_dot(ah, bm) + _dot(am, bh))
            + (_dot(am, bm) + _dot(ah, bl) + _dot(al, bh)))


def _rms(x, w):
    ms = jnp.mean(x * x, axis=-1, keepdims=True)
    return x * lax.rsqrt(ms + NORM_EPS) * w


def _norm_matmul_kernel(x_ref, g_ref, w_ref, o_ref, h_ref):
    @pl.when(pl.program_id(1) == 0)
    def _():
        h_ref[...] = _rms(x_ref[...], g_ref[...]).astype(BF16)

    o_ref[...] = _dot(h_ref[...], w_ref[...]).astype(o_ref.dtype)


def _norm_matmul(x, gain, w, tm, tn):
    T, K = x.shape
    N = w.shape[1]
    return pl.pallas_call(
        _norm_matmul_kernel,
        grid=(T // tm, N // tn),
        in_specs=[pl.BlockSpec((tm, K), lambda i, j: (i, 0)),
                  pl.BlockSpec((1, K), lambda i, j: (0, 0)),
                  pl.BlockSpec((K, tn), lambda i, j: (0, j))],
        out_specs=pl.BlockSpec((tm, tn), lambda i, j: (i, j)),
        out_shape=jax.ShapeDtypeStruct((T, N), F32),
        scratch_shapes=[pltpu.VMEM((tm, K), BF16)],
        compiler_params=_cparams("parallel", "arbitrary"),
    )(x, gain, w)


def _dswa_proj_kernel(x_ref, g_ref, w_ref, cos_ref, sin_ref, o0_ref, o1_ref, o2_ref, oab_ref):
    h = _rms(x_ref[...], g_ref[...]).astype(BF16)
    p = _dot(h, w_ref[...])
    cos = cos_ref[...]
    sin = sin_ref[...]
    lane = lax.broadcasted_iota(jnp.int32, cos.shape, 1)
    first = (lane & (DSWA_DH - 1)) < (DSWA_ROPE_DIM // 2)
    for g, o_ref in enumerate((o0_ref, o1_ref, o2_ref)):
        base = g * DSWA_GW
        for t in range(3):
            xt = p[:, base + t * LANES: base + (t + 1) * LANES]
            sw = jnp.where(first, pltpu.roll(xt, LANES - DSWA_ROPE_DIM // 2, 1),
                           pltpu.roll(xt, DSWA_ROPE_DIM // 2, 1))
            o_ref[:, t * LANES:(t + 1) * LANES] = xt * cos + sw * sin
        o_ref[:, 3 * LANES:DSWA_GW] = p[:, base + 3 * LANES: base + DSWA_GW]
    oab_ref[...] = p[:, 3 * DSWA_GW:]


def _dswa_proj(x, gain, w, cos_t, sin_t, seq, tm):
    T, K = x.shape
    N = w.shape[1]
    n_s = seq // tm
    grp = jax.ShapeDtypeStruct((T, DSWA_GW), F32)
    return pl.pallas_call(
        _dswa_proj_kernel,
        grid=(T // tm,),
        in_specs=[pl.BlockSpec((tm, K), lambda i: (i, 0)),
                  pl.BlockSpec((1, K), lambda i: (0, 0)),
                  pl.BlockSpec((K, N), lambda i: (0, 0)),
                  pl.BlockSpec((tm, LANES), lambda i: (i % n_s, 0)),
                  pl.BlockSpec((tm, LANES), lambda i: (i % n_s, 0))],
        out_specs=[pl.BlockSpec((tm, DSWA_GW), lambda i: (i, 0))] * 3
                  + [pl.BlockSpec((tm, LANES), lambda i: (i, 0))],
        out_shape=[grp, grp, grp, jax.ShapeDtypeStruct((T, LANES), F32)],
        compiler_params=_cparams("parallel"),
    )(x, gain, w, cos_t, sin_t)


def _unit_lower_inverse(low, eye):
    x = eye - low
    p = _dot_f32(low, low)
    n = 2
    while True:
        x = x + _dot_f32(x, p)
        n *= 2
        if n >= GDN_CHUNK:
            break
        p = _dot_f32(p, p)
    return x


def _gdn_kernel(qkv_ref, z_ref, ab_ref, cw_ref, alog_ref, dtb_ref, nw_ref, o_ref,
                carry_ref, q_s, k_s, v_s, state_ref):
    tc = qkv_ref.shape[0]
    C = GDN_CHUNK
    HD = GDN_HEADS * GDN_DK

    @pl.when(pl.program_id(1) == 0)
    def _():
        carry_ref[...] = jnp.zeros_like(carry_ref)
        state_ref[...] = jnp.zeros_like(state_ref)

    for grp, dst in enumerate((q_s, k_s, v_s)):
        cols = slice(grp * HD, (grp + 1) * HD)
        x = qkv_ref[:, cols]
        xf = jnp.concatenate([carry_ref[:, cols], x], axis=0)
        y = x * cw_ref[GDN_CONV - 1:GDN_CONV, cols]
        for j in range(1, GDN_CONV):
            y = y + pltpu.roll(xf, j, 0)[8:] * cw_ref[GDN_CONV - 1 - j:GDN_CONV - j, cols]
        y = y * _sigmoid(y)
        if grp == 2:
            dst[...] = y
        else:
            for h in range(GDN_HEADS):
                yh = y[:, h * GDN_DK:(h + 1) * GDN_DK]
                yh = yh * lax.rsqrt(jnp.sum(yh * yh, axis=-1, keepdims=True) + L2_EPS)
                if grp == 0:
                    yh = yh * (GDN_DK ** -0.5)
                dst[:, h * GDN_DK:(h + 1) * GDN_DK] = yh
    carry_ref[...] = qkv_ref[tc - 8:tc, :]

    ri = lax.broadcasted_iota(jnp.int32, (C, C), 0)
    ci = lax.broadcasted_iota(jnp.int32, (C, C), 1)
    incl = ri >= ci
    strict = ri > ci
    a_low = jnp.where(incl, 1.0, 0.0).astype(BF16)
    eye = jnp.where(ri == ci, 1.0, 0.0).astype(F32)
    neg_exp_alog = -jnp.exp(alog_ref[...])
    dtb = dtb_ref[...]
    nw = nw_ref[...]

    def chunk(c, _):
        rows = pl.ds(pl.multiple_of(c * C, C), C)
        ab = ab_ref[rows, :]
        gv = neg_exp_alog * _softplus(ab + dtb)
        bv = _sigmoid(ab)
        for h in range(GDN_HEADS):
            hs = slice(h * GDN_DK, (h + 1) * GDN_DK)
            g_b = jnp.broadcast_to(gv[:, h:h + 1], (C, GDN_DK))
            beta_b = jnp.broadcast_to(bv[:, GDN_HEADS + h:GDN_HEADS + h + 1], (C, GDN_DK))
            q = q_s[rows, hs]
            k = k_s[rows, hs]
            v = v_s[rows, hs]
            gc = _dot_exact_lhs(a_low, g_b)
            egc = jnp.exp(gc)
            gl = gc[C - 1:C, :]
            dlog = _dot_exact_lhs(a_low, jnp.where(strict, g_b[:, :C], 0.0))
            decay = jnp.where(incl, jnp.exp(dlog), 0.0)
            kb = k * beta_b
            vb = v * beta_b
            kbf = k.astype(BF16)
            low = jnp.where(strict, _dot_nt(kb.astype(BF16), kbf) * decay, 0.0)
            t_inv = _unit_lower_inverse(low, eye).astype(BF16)
            u = _dot(t_inv, vb.astype(BF16))
            w = _dot(t_inv, (kb * egc).astype(BF16))
            attn = _dot_nt(q.astype(BF16), kbf) * decay
            q_dec = q * egc
            k_dec = k * jnp.exp(gl - gc)
            st = state_ref[h]
            stb = st.astype(BF16)
            v_new = u - _dot(w.astype(BF16), stb)
            vnb = v_new.astype(BF16)
            o = _dot(q_dec.astype(BF16), stb) + _dot(attn.astype(BF16), vnb)
            state_ref[h] = st * jnp.exp(gl) + _dot(k_dec.T.astype(BF16), vnb)
            zz = z_ref[rows, hs]
            o_ref[rows, hs] = _rms(o, nw) * (zz * _sigmoid(zz))
        return 0

    lax.fori_loop(0, tc // C, chunk, 0)


def _gdn(proj, ab, conv_w, alog_v, dtb_v, norm_w, batch, seq, tc):
    T = proj.shape[0]
    n_s = seq // tc
    HD = GDN_HEADS * GDN_DK
    return pl.pallas_call(
        _gdn_kernel,
        grid=(batch, n_s),
        in_specs=[pl.BlockSpec((tc, 3 * HD), lambda b, s: (b * n_s + s, PM_A_QKV // (3 * HD))),
                  pl.BlockSpec((tc, HD), lambda b, s: (b * n_s + s, PM_A_Z // HD)),
                  pl.BlockSpec((tc, LANES), lambda b, s: (b * n_s + s, 0)),
                  pl.BlockSpec((8, 3 * HD), lambda b, s: (0, 0)),
                  pl.BlockSpec((1, LANES), lambda b, s: (0, 0)),
                  pl.BlockSpec((1, LANES), lambda b, s: (0, 0)),
                  pl.BlockSpec((1, GDN_DV), lambda b, s: (0, 0))],
        out_specs=pl.BlockSpec((tc, HD), lambda b, s: (b * n_s + s, 0)),
        out_shape=jax.ShapeDtypeStruct((T, HD), F32),
        scratch_shapes=[pltpu.VMEM((8, 3 * HD), F32),
                        pltpu.VMEM((tc, HD), F32), pltpu.VMEM((tc, HD), F32), pltpu.VMEM((tc, HD), F32),
                        pltpu.VMEM((GDN_HEADS, GDN_DK, GDN_DV), F32)],
        compiler_params=_cparams("parallel", "arbitrary"),
    )(proj, proj, ab, conv_w, alog_v, dtb_v, norm_w)


def _dswa_kernel(cur_ref, prev_ref, o_ref, lse_ref):
    BL = DSWA_BLOCK
    n = pl.program_id(2)
    cur = cur_ref[0]
    prev = prev_ref[0]
    ii = lax.broadcasted_iota(jnp.int32, (BL, 2 * BL), 0)
    jj = lax.broadcasted_iota(jnp.int32, (BL, 2 * BL), 1)
    first_valid = jnp.where(n > 0, 0, BL)
    mask = (jj >= ii) & (jj <= ii + BL) & (jj >= first_valid)
    kw = DSWA_HPG * DSWA_DH
    for h in range(DSWA_HPG):
        hs = slice(h * DSWA_DH, (h + 1) * DSWA_DH)
        ks = slice(kw + h * DSWA_DH, kw + (h + 1) * DSWA_DH)
        vs = slice(2 * kw + h * DSWA_DH, 2 * kw + (h + 1) * DSWA_DH)
        q = cur[:, hs].astype(BF16)
        k = jnp.concatenate([prev[:, ks], cur[:, ks]], axis=0).astype(BF16)
        v = jnp.concatenate([prev[:, vs], cur[:, vs]], axis=0).astype(BF16)
        s = jnp.where(mask, _dot_nt(q, k), -jnp.inf)
        m = jnp.max(s, axis=-1, keepdims=True)
        p = jnp.exp(s - m)
        l = jnp.sum(p, axis=-1, keepdims=True)
        o_ref[0, :, hs] = _dot(p.astype(BF16), v) * (1.0 / l)
        lse_ref[0, :, hs] = jnp.broadcast_to(m + jnp.log(l), (BL, DSWA_DH))
    o_ref[0, :, kw:] = jnp.zeros((BL, DSWA_OW - kw), F32)
    lse_ref[0, :, kw:] = jnp.zeros((BL, DSWA_OW - kw), F32)


def _dswa(grp, batch, seq, dil):
    T = grp.shape[0]
    L = seq // dil
    nb = L // DSWA_BLOCK
    g3 = grp.reshape(batch, L, dil * DSWA_GW)
    out = jax.ShapeDtypeStruct((batch, L, dil * DSWA_OW), F32)
    o, lse = pl.pallas_call(
        _dswa_kernel,
        grid=(batch, dil, nb),
        in_specs=[pl.BlockSpec((1, DSWA_BLOCK, DSWA_GW), lambda b, r, n: (b, n, r)),
                  pl.BlockSpec((1, DSWA_BLOCK, DSWA_GW), lambda b, r, n: (b, jnp.maximum(n - 1, 0), r))],
        out_specs=[pl.BlockSpec((1, DSWA_BLOCK, DSWA_OW), lambda b, r, n: (b, n, r))] * 2,
        out_shape=[out, out],
        compiler_params=_cparams("parallel", "parallel", "arbitrary"),
    )(g3, g3)
    return o.reshape(T, DSWA_OW), lse.reshape(T, DSWA_OW)


def _split2(x):
    hi = x.astype(BF16)
    lo = (x - hi.astype(F32)).astype(BF16)
    return hi, lo


def _sb_kernel(q_ref, k_ref, v_ref, o_ref):
    BLK = SB_BLOCK
    DH = SB_DH
    i = pl.program_id(2)
    tt = lax.broadcasted_iota(jnp.int32, (BLK, BLK), 0)
    ss = lax.broadcasted_iota(jnp.int32, (BLK, BLK), 1)
    later = jnp.where(tt > ss, 1.0, 0.0).astype(BF16)
    causal = ss < tt
    q2 = q_ref[...]
    qs = [q2[:, hh * DH:(hh + 1) * DH].astype(BF16) for hh in range(2)]

    def block(j, carry, diag):
        rows = pl.ds(pl.multiple_of(j * BLK, BLK), BLK)
        kj = k_ref[rows, :]
        vj = v_ref[rows, :]
        new = []
        for hh in range(2):
            c, acc = carry[hh]
            k = kj[:, hh * DH:(hh + 1) * DH].astype(BF16)
            v = vj[:, hh * DH:(hh + 1) * DH].astype(BF16)
            z = _dot_nt(qs[hh], k)
            lom = -(jnp.maximum(z, 0.0) + jnp.log(1.0 + jnp.exp(-jnp.abs(z))))
            lom_m = jnp.where(causal, lom, 0.0) if diag else lom
            hi, lo = _split2(lom_m)
            suffix = _dot(hi, later) + _dot(lo, later)
            a = jnp.exp(z + lom + suffix + c)
            if diag:
                a = jnp.where(causal, a, 0.0)
            acc = acc + _dot(a.astype(BF16), v)
            c = c + jnp.sum(lom_m, axis=-1, keepdims=True)
            new.append((c, acc))
        return tuple(new)

    zero = (jnp.zeros((BLK, 1), F32), jnp.zeros((BLK, DH), F32))
    carry = block(i, (zero, zero), True)
    carry = lax.fori_loop(0, i, lambda it, cr: block(i - 1 - it, cr, False), carry)
    o_ref[...] = jnp.concatenate([carry[0][1], carry[1][1]], axis=1)


def _sb(proj, batch, seq):
    T = proj.shape[0]
    nq = seq // SB_BLOCK
    pairs = SB_HEADS // 2
    qo, ko, vo = PM_C // LANES, (PM_C + SB_HEADS * SB_DH) // LANES, (PM_C + 2 * SB_HEADS * SB_DH) // LANES
    return pl.pallas_call(
        _sb_kernel,
        grid=(batch, pairs, nq),
        in_specs=[pl.BlockSpec((SB_BLOCK, LANES), lambda b, p, i: (b * nq + i, qo + p)),
                  pl.BlockSpec((seq, LANES), lambda b, p, i: (b, ko + p)),
                  pl.BlockSpec((seq, LANES), lambda b, p, i: (b, vo + p))],
        out_specs=pl.BlockSpec((SB_BLOCK, LANES), lambda b, p, i: (b * nq + i, p)),
        out_shape=jax.ShapeDtypeStruct((T, SB_HEADS * SB_DH), F32),
        compiler_params=_cparams("parallel", "parallel", "arbitrary"),
    )(proj, proj, proj)


def _ret_kernel(qk_ref, v_ref, g_ref, cos_ref, sin_ref, nw_ref, o_ref, r_ref):
    tc = qk_ref.shape[0]
    C = RET_CHUNK
    half = RET_DK // 2

    @pl.when(pl.program_id(1) == 0)
    def _():
        r_ref[...] = jnp.zeros_like(r_ref)

    lane = lax.broadcasted_iota(jnp.int32, (C, LANES), 1)
    first = (lane & (RET_DK - 1)) < half
    ri = lax.broadcasted_iota(jnp.int32, (C, C), 0)
    ci = lax.broadcasted_iota(jnp.int32, (C, C), 1)
    diff = (ri - ci).astype(F32)
    causal = ri >= ci
    rowf = lax.broadcasted_iota(jnp.int32, (C, LANES), 0).astype(F32)
    nw = nw_ref[...]

    def chunk(c, _):
        rows = pl.ds(pl.multiple_of(c * C, C), C)
        cos = cos_ref[rows, :]
        sin = sin_ref[rows, :]

        def rope(x):
            sw = jnp.where(first, pltpu.roll(x, LANES - half, 1), pltpu.roll(x, half, 1))
            return x * cos + sw * sin

        for pair in range(RET_HEADS // 2):
            qp = rope(qk_ref[rows, pair * LANES:(pair + 1) * LANES])
            kp = rope(qk_ref[rows, RET_HEADS * RET_DK + pair * LANES:RET_HEADS * RET_DK + (pair + 1) * LANES])
            for hh in range(2):
                h = 2 * pair + hh
                lg = _RET_LOG_GAMMA[h]
                hs = slice(h * RET_DV, (h + 1) * RET_DV)
                q = qp[:, hh * RET_DK:(hh + 1) * RET_DK]
                k = kp[:, hh * RET_DK:(hh + 1) * RET_DK]
                vb = v_ref[rows, hs].astype(BF16)
                qb = q.astype(BF16)
                dmat = jnp.where(causal, jnp.exp(diff * lg), 0.0)
                intra = _dot_nt(qb, k.astype(BF16)) * dmat
                xi = jnp.exp((rowf + 1.0) * lg)
                zeta = jnp.exp((C - 1.0 - rowf[:, :RET_DK]) * lg)
                r = r_ref[h]
                o = _dot(intra.astype(BF16), vb) + _dot(qb, r.astype(BF16)) * xi
                r_ref[h] = r * math.exp(C * lg) + _dot((k * zeta).T.astype(BF16), vb)
                gg = g_ref[rows, hs]
                o_ref[rows, hs] = _rms(o, nw) * (gg * _sigmoid(gg))
        return 0

    lax.fori_loop(0, tc // C, chunk, 0)


def _ret(proj, cos_t, sin_t, norm_w, batch, seq, tc):
    T = proj.shape[0]
    n_s = seq // tc
    W = RET_HEADS * RET_DV
    return pl.pallas_call(
        _ret_kernel,
        grid=(batch, n_s),
        in_specs=[pl.BlockSpec((tc, W), lambda b, s: (b * n_s + s, PM_D_QK // W)),
                  pl.BlockSpec((tc, W), lambda b, s: (b * n_s + s, PM_D_V // W)),
                  pl.BlockSpec((tc, W), lambda b, s: (b * n_s + s, PM_D_G // W)),
                  pl.BlockSpec((tc, LANES), lambda b, s: (s, 0)),
                  pl.BlockSpec((tc, LANES), lambda b, s: (s, 0)),
                  pl.BlockSpec((1, RET_DV), lambda b, s: (0, 0))],
        out_specs=pl.BlockSpec((tc, W), lambda b, s: (b * n_s + s, 0)),
        out_shape=jax.ShapeDtypeStruct((T, W), F32),
        scratch_shapes=[pltpu.VMEM((RET_HEADS, RET_DK, RET_DV), F32)],
        compiler_params=_cparams("parallel", "arbitrary"),
    )(proj, proj, proj, cos_t, sin_t, norm_w)


def _merge_kernel(x_ref, oa_ref, ob0_ref, ob1_ref, ob2_ref, l0_ref, l1_ref, l2_ref, oc_ref, od_ref,
                  ga_ref, gb_ref, gc_ref, gd_ref, wa_ref, wb_ref, wc_ref, wd_ref, wo_ref, nw_ref, out_ref):
    l0, l1, l2 = l0_ref[...], l1_ref[...], l2_ref[...]
    m = jnp.maximum(jnp.maximum(l0, l1), l2)
    e0, e1, e2 = jnp.exp(l0 - m), jnp.exp(l1 - m), jnp.exp(l2 - m)
    inv = 1.0 / (e0 + e1 + e2)
    ob = jnp.concatenate([ob0_ref[...] * (e0 * inv), ob1_ref[...] * (e1 * inv), ob2_ref[...] * (e2 * inv)],
                         axis=1).astype(BF16)
    y = _sigmoid(ga_ref[...]) * _dot(oa_ref[...].astype(BF16), wa_ref[...])
    y = y + _sigmoid(gb_ref[...]) * _dot(ob, wb_ref[...])
    y = y + _sigmoid(gc_ref[...]) * _dot(oc_ref[...].astype(BF16), wc_ref[...])
    y = y + _sigmoid(gd_ref[...]) * _dot(od_ref[...].astype(BF16), wd_ref[...])
    mixed = _dot(y.astype(BF16), wo_ref[...])
    out_ref[...] = x_ref[...] + _rms(mixed, nw_ref[...])


def _merge(x, oa, obs, lses, oc, od, proj, wa, wb, wc, wd, wo, nw, tm):
    T, D = x.shape
    row = lambda w: pl.BlockSpec((tm, w), lambda i: (i, 0))
    gate = lambda br: pl.BlockSpec((tm, D), lambda i: (i, PM_GATES // D + br))
    full = lambda a: pl.BlockSpec(a.shape, lambda i: (0, 0))
    return pl.pallas_call(
        _merge_kernel,
        grid=(T // tm,),
        in_specs=[row(D), row(oa.shape[1])] + [row(DSWA_OW)] * 6 + [row(oc.shape[1]), row(od.shape[1])]
                 + [gate(0), gate(1), gate(2), gate(3)] + [full(wa), full(wb), full(wc), full(wd), full(wo), full(nw)],
        out_specs=row(D),
        out_shape=jax.ShapeDtypeStruct((T, D), F32),
        compiler_params=_cparams("parallel"),
    )(x, oa, *obs, *lses, oc, od, proj, proj, proj, proj, wa, wb, wc, wd, wo, nw)


def _mlp_kernel(x_ref, n1_ref, w1_ref, w2_ref, n2_ref, out_ref, h_ref, acc_ref):
    f = pl.program_id(1)

    @pl.when(f == 0)
    def _():
        h_ref[...] = _rms(x_ref[...], n1_ref[...]).astype(BF16)
        acc_ref[...] = jnp.zeros_like(acc_ref)

    hid = jnp.maximum(_dot(h_ref[...], w1_ref[...]), 0.0)
    acc_ref[...] += _dot((hid * hid).astype(BF16), w2_ref[...])

    @pl.when(f == pl.num_programs(1) - 1)
    def _():
        out_ref[...] = x_ref[...] + _rms(acc_ref[...], n2_ref[...])


def _mlp(x, n1, w1, w2, n2, tm, tf):
    T, D = x.shape
    F = w1.shape[1]
    return pl.pallas_call(
        _mlp_kernel,
        grid=(T // tm, F // tf),
        in_specs=[pl.BlockSpec((tm, D), lambda i, f: (i, 0)),
                  pl.BlockSpec((1, D), lambda i, f: (0, 0)),
                  pl.BlockSpec((D, tf), lambda i, f: (0, f)),
                  pl.BlockSpec((tf, D), lambda i, f: (f, 0)),
                  pl.BlockSpec((1, D), lambda i, f: (0, 0))],
        out_specs=pl.BlockSpec((tm, D), lambda i, f: (i, 0)),
        out_shape=jax.ShapeDtypeStruct((T, D), F32),
        scratch_shapes=[pltpu.VMEM((tm, D), BF16), pltpu.VMEM((tm, D), F32)],
        compiler_params=_cparams("parallel", "arbitrary"),
    )(x, n1, w1, w2, n2)


def _prep_w_in(w_in):
    sec = [w_in[:, _COL_OFF[i]:_COL_OFF[i + 1]] for i in range(len(_COL_SIZES))]
    a_qkv, a_z, a_a, a_b, b_qkv, c_qkv, d_qk, d_v, d_g, gates = sec
    sbw = SB_HEADS * SB_DH
    c_qkv = jnp.concatenate([c_qkv[:, :sbw] * SB_DH ** -0.5, c_qkv[:, sbw:]], axis=1)
    rw = RET_HEADS * RET_DK
    d_qk = jnp.concatenate([d_qk[:, :rw], d_qk[:, rw:] * RET_DK ** -0.5], axis=1)
    w_main = jnp.concatenate([a_qkv, a_z, c_qkv, d_qk, d_v, d_g, gates], axis=1).astype(BF16)
    bw = DSWA_HEADS * DSWA_DH
    gw = DSWA_HPG * DSWA_DH
    K = w_in.shape[0]
    groups = []
    for g in range(len(DSWA_GROUPS)):
        q = b_qkv[:, g * gw:(g + 1) * gw] * DSWA_DH ** -0.5
        k = b_qkv[:, bw + g * gw: bw + (g + 1) * gw]
        v = b_qkv[:, 2 * bw + g * gw: 2 * bw + (g + 1) * gw]
        groups += [q, k, v, jnp.zeros((K, DSWA_GW - 3 * gw), F32)]
    w_b = jnp.concatenate(groups + [a_a, a_b, jnp.zeros((K, LANES - 2 * GDN_HEADS), F32)], axis=1).astype(BF16)
    return w_main, w_b


def _rope_tables(seq):
    pos = jnp.arange(seq, dtype=jnp.int32).astype(F32)[:, None]
    inv_b = DSWA_ROPE_THETA ** (-jnp.arange(0, DSWA_ROPE_DIM, 2, dtype=F32) / DSWA_ROPE_DIM)
    ang = pos * inv_b[None, :]
    cb, sb = jnp.cos(ang), jnp.sin(ang)
    rest = DSWA_DH - DSWA_ROPE_DIM
    cos_b = jnp.tile(jnp.concatenate([cb, cb, jnp.ones((seq, rest), F32)], axis=1), (1, LANES // DSWA_DH))
    sin_b = jnp.tile(jnp.concatenate([-sb, sb, jnp.zeros((seq, rest), F32)], axis=1), (1, LANES // DSWA_DH))
    inv_r = RET_THETA ** (-jnp.linspace(0.0, 1.0, RET_DK // 2, dtype=F32))
    ang = pos * inv_r[None, :]
    cr, sr = jnp.cos(ang), jnp.sin(ang)
    cos_r = jnp.tile(jnp.concatenate([cr, cr], axis=1), (1, LANES // RET_DK))
    sin_r = jnp.tile(jnp.concatenate([-sr, sr], axis=1), (1, LANES // RET_DK))
    return cos_b, sin_b, cos_r, sin_r


def _lane_vec(v):
    return jnp.zeros((1, LANES), F32).at[0, :v.shape[0]].set(v.astype(F32))


def _layer(x, batch, seq, tabs, n_pre_mix, n_post_mix, n_pre_mlp, n_post_mlp, w_in, conv_w, a_log, dt_bias,
           gdn_norm, ret_norm, w_br_a, w_br_b, w_br_c, w_br_d, w_o, w_mlp_in, w_mlp_out):
    T = x.shape[0]
    cos_b, sin_b, cos_r, sin_r = tabs
    w_main, w_b = _prep_w_in(w_in)
    gain = n_pre_mix.reshape(1, -1)
    proj = _norm_matmul(x, gain, w_main, tm=min(1024, T), tn=512)
    g0, g1, g2, ab = _dswa_proj(x, gain, w_b, cos_b, sin_b, seq, tm=min(512, seq))

    cw = jnp.zeros((8, conv_w.shape[1]), F32).at[:GDN_CONV].set(conv_w)
    oa = _gdn(proj, ab, cw, _lane_vec(a_log), _lane_vec(dt_bias), gdn_norm.reshape(1, -1), batch, seq,
              tc=min(256, seq))
    obs, lses = [], []
    for grp, (_, dil) in zip((g0, g1, g2), DSWA_GROUPS):
        o, lse = _dswa(grp, batch, seq, dil)
        obs.append(o)
        lses.append(lse)
    oc = _sb(proj, batch, seq)
    od = _ret(proj, cos_r, sin_r, ret_norm.reshape(1, -1), batch, seq, tc=min(512, seq))

    gw = DSWA_HPG * DSWA_DH
    wb = jnp.concatenate(
        [jnp.concatenate([w_br_b[g * gw:(g + 1) * gw], jnp.zeros((DSWA_OW - gw, D_MODEL), F32)], axis=0)
         for g in range(len(DSWA_GROUPS))], axis=0).astype(BF16)
    x1 = _merge(x, oa, obs, lses, oc, od, proj, w_br_a.astype(BF16), wb, w_br_c.astype(BF16),
                w_br_d.astype(BF16), w_o.astype(BF16), n_post_mix.reshape(1, -1), tm=min(256, T))
    return _mlp(x1, n_pre_mlp.reshape(1, -1), w_mlp_in.astype(BF16), w_mlp_out.astype(BF16),
                n_post_mlp.reshape(1, -1), tm=min(1024, T), tf=512)


def kernel(x, norm_pre_mix, norm_post_mix, norm_pre_mlp, norm_post_mlp, w_in, conv_w, a_log, dt_bias, gdn_norm,
           ret_norm, w_br_a, w_br_b, w_br_c, w_br_d, w_o, w_mlp_in, w_mlp_out):
    batch, seq, d = x.shape
    tabs = _rope_tables(seq)
    h = x.reshape(batch * seq, d)
    for l in range(norm_pre_mix.shape[0]):
        h = _layer(h, batch, seq, tabs, norm_pre_mix[l], norm_post_mix[l], norm_pre_mlp[l], norm_post_mlp[l],
                   w_in[l], conv_w[l], a_log[l], dt_bias[l], gdn_norm[l], ret_norm[l], w_br_a[l], w_br_b[l],
                   w_br_c[l], w_br_d[l], w_o[l], w_mlp_in[l], w_mlp_out[l])
    return h.reshape(batch, seq, d)
```

```python
import functools
import math

import numpy as np
import jax
import jax.numpy as jnp
from jax import lax
from jax.experimental import pallas as pl
from jax.experimental.pallas import tpu as pltpu

F32 = jnp.float32
BF16 = jnp.bfloat16

D_MODEL = 1024
N_LAYERS = 2
GDN_HEADS, GDN_DK, GDN_DV, GDN_CONV, GDN_CHUNK = 4, 128, 128, 4, 64
DSWA_GROUPS = ((128, 1), (512, 4), (2048, 16))
DSWA_HPG, DSWA_DH, DSWA_BLOCK = 3, 64, 128
DSWA_HEADS = DSWA_HPG * len(DSWA_GROUPS)
DSWA_ROPE_THETA, DSWA_ROPE_DIM = 500000.0, DSWA_DH // 4
SB_HEADS, SB_DH, SB_BLOCK = 8, 64, 128
SB_LOG_ZERO = -110.0
RET_HEADS, RET_DK, RET_DV, RET_THETA = 4, 64, 128, 10000.0
RET_CHUNK = 128
D_FF = 4 * D_MODEL
NORM_EPS = 1e-6
L2_EPS = 1e-6

_COL_SIZES = (1536, 512, 4, 4, 1728, 1536, 512, 512, 512, 4096)
_COL_OFF = np.concatenate([[0], np.cumsum(_COL_SIZES)])

PM_A_QKV, PM_A_Z, PM_C, PM_D_QK, PM_D_V, PM_D_G, PM_GATES, PM_WIDTH = 0, 1536, 2048, 3584, 4096, 4608, 5120, 9216
DSWA_GW = 640
DSWA_OW = 256

VMEM_LIMIT = 48 * 1024 * 1024
LANES = 128

_RET_LOG_GAMMA = [float(np.log1p(-np.exp2(np.float32(-5.0 - h))).astype(np.float32)) for h in range(RET_HEADS)]


def _cparams(*sem):
    return pltpu.CompilerParams(dimension_semantics=sem, vmem_limit_bytes=VMEM_LIMIT)


def _sigmoid(x):
    return 1.0 / (1.0 + jnp.exp(-x))


def _softplus(x):
    return jnp.maximum(x, 0.0) + jnp.log(1.0 + jnp.exp(-jnp.abs(x)))


def _dot(a, b):
    return jnp.dot(a, b, preferred_element_type=F32)


def _dot_nt(a, b):
    return lax.dot_general(a, b, (((1,), (1,)), ((), ())), preferred_element_type=F32)


def _split3(x):
    hi = x.astype(BF16)
    r = x - hi.astype(F32)
    mid = r.astype(BF16)
    lo = (r - mid.astype(F32)).astype(BF16)
    return hi, mid, lo


def _dot_exact_lhs(a_bf16, x):
    hi, mid, lo = _split3(x)
    return _dot(a_bf16, hi) + (_dot(a_bf16, mid) + _dot(a_bf16, lo))


def _dot_f32(a, b):
    ah, am, al = _split3(a)
    bh, bm, bl = _split3(b)
    return (_dot(ah, bh) + (_dot(ah, bm) + _dot(am, bh))
            + (_dot(am, bm) + _dot(ah, bl) + _dot(al, bh)))


def _rms(x, w):
    ms = jnp.mean(x * x, axis=-1, keepdims=True)
    return x * lax.rsqrt(ms + NORM_EPS) * w


def _norm_matmul_kernel(x_ref, g_ref, w_ref, o_ref, h_ref):
    @pl.when(pl.program_id(1) == 0)
    def _():
        h_ref[...] = _rms(x_ref[...], g_ref[...]).astype(BF16)

    o_ref[...] = _dot(h_ref[...], w_ref[...]).astype(o_ref.dtype)


def _norm_matmul(x, gain, w, tm, tn):
    T, K = x.shape
    N = w.shape[1]
    return pl.pallas_call(
        _norm_matmul_kernel,
        grid=(T // tm, N // tn),
        in_specs=[pl.BlockSpec((tm, K), lambda i, j: (i, 0)),
                  pl.BlockSpec((1, K), lambda i, j: (0, 0)),
                  pl.BlockSpec((K, tn), lambda i, j: (0, j))],
        out_specs=pl.BlockSpec((tm, tn), lambda i, j: (i, j)),
        out_shape=jax.ShapeDtypeStruct((T, N), F32),
        scratch_shapes=[pltpu.VMEM((tm, K), BF16)],
        compiler_params=_cparams("parallel", "arbitrary"),
    )(x, gain, w)


def _dswa_proj_kernel(x_ref, g_ref, w_ref, cos_ref, sin_ref, o0_ref, o1_ref, o2_ref, oab_ref):
    h = _rms(x_ref[...], g_ref[...]).astype(BF16)
    p = _dot(h, w_ref[...])
    cos = cos_ref[...]
    sin = sin_ref[...]
    lane = lax.broadcasted_iota(jnp.int32, cos.shape, 1)
    first = (lane & (DSWA_DH - 1)) < (DSWA_ROPE_DIM // 2)
    for g, o_ref in enumerate((o0_ref, o1_ref, o2_ref)):
        base = g * DSWA_GW
        for t in range(3):
            xt = p[:, base + t * LANES: base + (t + 1) * LANES]
            sw = jnp.where(first, pltpu.roll(xt, LANES - DSWA_ROPE_DIM // 2, 1),
                           pltpu.roll(xt, DSWA_ROPE_DIM // 2, 1))
            o_ref[:, t * LANES:(t + 1) * LANES] = xt * cos + sw * sin
        o_ref[:, 3 * LANES:DSWA_GW] = p[:, base + 3 * LANES: base + DSWA_GW]
    oab_ref[...] = p[:, 3 * DSWA_GW:]


def _dswa_proj(x, gain, w, cos_t, sin_t, seq, tm):
    T, K = x.shape
    N = w.shape[1]
    n_s = seq // tm
    grp = jax.ShapeDtypeStruct((T, DSWA_GW), F32)
    return pl.pallas_call(
        _dswa_proj_kernel,
        grid=(T // tm,),
        in_specs=[pl.BlockSpec((tm, K), lambda i: (i, 0)),
                  pl.BlockSpec((1, K), lambda i: (0, 0)),
                  pl.BlockSpec((K, N), lambda i: (0, 0)),
                  pl.BlockSpec((tm, LANES), lambda i: (i % n_s, 0)),
                  pl.BlockSpec((tm, LANES), lambda i: (i % n_s, 0))],
        out_specs=[pl.BlockSpec((tm, DSWA_GW), lambda i: (i, 0))] * 3
                  + [pl.BlockSpec((tm, LANES), lambda i: (i, 0))],
        out_shape=[grp, grp, grp, jax.ShapeDtypeStruct((T, LANES), F32)],
        compiler_params=_cparams("parallel"),
    )(x, gain, w, cos_t, sin_t)


def _unit_lower_inverse(low, eye):
    x = eye - low
    p = _dot_f32(low, low)
    n = 2
    while True:
        x = x + _dot_f32(x, p)
        n *= 2
        if n >= GDN_CHUNK:
            break
        p = _dot_f32(p, p)
    return x


def _gdn_kernel(qkv_ref, z_ref, ab_ref, cw_ref, alog_ref, dtb_ref, nw_ref, o_ref,
                carry_ref, q_s, k_s, v_s, state_ref):
    tc = qkv_ref.shape[0]
    C = GDN_CHUNK
    HD = GDN_HEADS * GDN_DK

    @pl.when(pl.program_id(1) == 0)
    def _():
        carry_ref[...] = jnp.zeros_like(carry_ref)
        state_ref[...] = jnp.zeros_like(state_ref)

    for grp, dst in enumerate((q_s, k_s, v_s)):
        cols = slice(grp * HD, (grp + 1) * HD)
        x = qkv_ref[:, cols]
        xf = jnp.concatenate([carry_ref[:, cols], x], axis=0)
        y = x * cw_ref[GDN_CONV - 1:GDN_CONV, cols]
        for j in range(1, GDN_CONV):
            y = y + pltpu.roll(xf, j, 0)[8:] * cw_ref[GDN_CONV - 1 - j:GDN_CONV - j, cols]
        y = y * _sigmoid(y)
        if grp == 2:
            dst[...] = y
        else:
            for h in range(GDN_HEADS):
                yh = y[:, h * GDN_DK:(h + 1) * GDN_DK]
                yh = yh * lax.rsqrt(jnp.sum(yh * yh, axis=-1, keepdims=True) + L2_EPS)
                if grp == 0:
                    yh = yh * (GDN_DK ** -0.5)
                dst[:, h * GDN_DK:(h + 1) * GDN_DK] = yh
    carry_ref[...] = qkv_ref[tc - 8:tc, :]

    ri = lax.broadcasted_iota(jnp.int32, (C, C), 0)
    ci = lax.broadcasted_iota(jnp.int32, (C, C), 1)
    incl = ri >= ci
    strict = ri > ci
    a_low = jnp.where(incl, 1.0, 0.0).astype(BF16)
    eye = jnp.where(ri == ci, 1.0, 0.0).astype(F32)
    neg_exp_alog = -jnp.exp(alog_ref[...])
    dtb = dtb_ref[...]
    nw = nw_ref[...]

    def chunk(c, _):
        rows = pl.ds(pl.multiple_of(c * C, C), C)
        ab = ab_ref[rows, :]
        gv = neg_exp_alog * _softplus(ab + dtb)
        bv = _sigmoid(ab)
        for h in range(GDN_HEADS):
            hs = slice(h * GDN_DK, (h + 1) * GDN_DK)
            g_b = jnp.broadcast_to(gv[:, h:h + 1], (C, GDN_DK))
            beta_b = jnp.broadcast_to(bv[:, GDN_HEADS + h:GDN_HEADS + h + 1], (C, GDN_DK))
            q = q_s[rows, hs]
            k = k_s[rows, hs]
            v = v_s[rows, hs]
            gc = _dot_exact_lhs(a_low, g_b)
            egc = jnp.exp(gc)
            gl = gc[C - 1:C, :]
            dlog = _dot_exact_lhs(a_low, jnp.where(strict, g_b[:, :C], 0.0))
            decay = jnp.where(incl, jnp.exp(dlog), 0.0)
            kb = k * beta_b
            vb = v * beta_b
            kbf = k.astype(BF16)
            low = jnp.where(strict, _dot_nt(kb.astype(BF16), kbf) * decay, 0.0)
            t_inv = _unit_lower_inverse(low, eye).astype(BF16)
            u = _dot(t_inv, vb.astype(BF16))
            w = _dot(t_inv, (kb * egc).astype(BF16))
            attn = _dot_nt(q.astype(BF16), kbf) * decay
            q_dec = q * egc
            k_dec = k * jnp.exp(gl - gc)
            st = state_ref[h]
            stb = st.astype(BF16)
            v_new = u - _dot(w.astype(BF16), stb)
            vnb = v_new.astype(BF16)
            o = _dot(q_dec.astype(BF16), stb) + _dot(attn.astype(BF16), vnb)
            state_ref[h] = st * jnp.exp(gl) + _dot(k_dec.T.astype(BF16), vnb)
            zz = z_ref[rows, hs]
            o_ref[rows, hs] = _rms(o, nw) * (zz * _sigmoid(zz))
        return 0

    lax.fori_loop(0, tc // C, chunk, 0)


def _gdn(proj, ab, conv_w, alog_v, dtb_v, norm_w, batch, seq, tc):
    T = proj.shape[0]
    n_s = seq // tc
    HD = GDN_HEADS * GDN_DK
    return pl.pallas_call(
        _gdn_kernel,
        grid=(batch, n_s),
        in_specs=[pl.BlockSpec((tc, 3 * HD), lambda b, s: (b * n_s + s, PM_A_QKV // (3 * HD))),
                  pl.BlockSpec((tc, HD), lambda b, s: (b * n_s + s, PM_A_Z // HD)),
                  pl.BlockSpec((tc, LANES), lambda b, s: (b * n_s + s, 0)),
                  pl.BlockSpec((8, 3 * HD), lambda b, s: (0, 0)),
                  pl.BlockSpec((1, LANES), lambda b, s: (0, 0)),
                  pl.BlockSpec((1, LANES), lambda b, s: (0, 0)),
                  pl.BlockSpec((1, GDN_DV), lambda b, s: (0, 0))],
        out_specs=pl.BlockSpec((tc, HD), lambda b, s: (b * n_s + s, 0)),
        out_shape=jax.ShapeDtypeStruct((T, HD), F32),
        scratch_shapes=[pltpu.VMEM((8, 3 * HD), F32),
                        pltpu.VMEM((tc, HD), F32), pltpu.VMEM((tc, HD), F32), pltpu.VMEM((tc, HD), F32),
                        pltpu.VMEM((GDN_HEADS, GDN_DK, GDN_DV), F32)],
        compiler_params=_cparams("parallel", "arbitrary"),
    )(proj, proj, ab, conv_w, alog_v, dtb_v, norm_w)


def _dswa_kernel(cur_ref, prev_ref, o_ref, lse_ref):
    BL = DSWA_BLOCK
    n = pl.program_id(2)
    cur = cur_ref[0]
    prev = prev_ref[0]
    ii = lax.broadcasted_iota(jnp.int32, (BL, 2 * BL), 0)
    jj = lax.broadcasted_iota(jnp.int32, (BL, 2 * BL), 1)
    first_valid = jnp.where(n > 0, 0, BL)
    mask = (jj >= ii) & (jj <= ii + BL) & (jj >= first_valid)
    kw = DSWA_HPG * DSWA_DH
    for h in range(DSWA_HPG):
        hs = slice(h * DSWA_DH, (h + 1) * DSWA_DH)
        ks = slice(kw + h * DSWA_DH, kw + (h + 1) * DSWA_DH)
        vs = slice(2 * kw + h * DSWA_DH, 2 * kw + (h + 1) * DSWA_DH)
        q = cur[:, hs].astype(BF16)
        k = jnp.concatenate([prev[:, ks], cur[:, ks]], axis=0).astype(BF16)
        v = jnp.concatenate([prev[:, vs], cur[:, vs]], axis=0).astype(BF16)
        s = jnp.where(mask, _dot_nt(q, k), -jnp.inf)
        m = jnp.max(s, axis=-1, keepdims=True)
        p = jnp.exp(s - m)
        l = jnp.sum(p, axis=-1, keepdims=True)
        o_ref[0, :, hs] = _dot(p.astype(BF16), v) * (1.0 / l)
        lse_ref[0, :, hs] = jnp.broadcast_to(m + jnp.log(l), (BL, DSWA_DH))
    o_ref[0, :, kw:] = jnp.zeros((BL, DSWA_OW - kw), F32)
    lse_ref[0, :, kw:] = jnp.zeros((BL, DSWA_OW - kw), F32)


def _dswa(grp, batch, seq, dil):
    T = grp.shape[0]
    L = seq // dil
    nb = L // DSWA_BLOCK
    g3 = grp.reshape(batch, L, dil * DSWA_GW)
    out = jax.ShapeDtypeStruct((batch, L, dil * DSWA_OW), F32)
    o, lse = pl.pallas_call(
        _dswa_kernel,
        grid=(batch, dil, nb),
        in_specs=[pl.BlockSpec((1, DSWA_BLOCK, DSWA_GW), lambda b, r, n: (b, n, r)),
                  pl.BlockSpec((1, DSWA_BLOCK, DSWA_GW), lambda b, r, n: (b, jnp.maximum(n - 1, 0), r))],
        out_specs=[pl.BlockSpec((1, DSWA_BLOCK, DSWA_OW), lambda b, r, n: (b, n, r))] * 2,
        out_shape=[out, out],
        compiler_params=_cparams("parallel", "parallel", "arbitrary"),
    )(g3, g3)
    return o.reshape(T, DSWA_OW), lse.reshape(T, DSWA_OW)


def _split2(x):
    hi = x.astype(BF16)
    lo = (x - hi.astype(F32)).astype(BF16)
    return hi, lo


def _sb_kernel(q_ref, k_ref, v_ref, o_ref):
    BLK = SB_BLOCK
    DH = SB_DH
    i = pl.program_id(2)
    tt = lax.broadcasted_iota(jnp.int32, (BLK, BLK), 0)
    ss = lax.broadcasted_iota(jnp.int32, (BLK, BLK), 1)
    later = jnp.where(tt > ss, 1.0, 0.0).astype(BF16)
    causal = ss < tt
    q2 = q_ref[...]
    qs = [q2[:, hh * DH:(hh + 1) * DH].astype(BF16) for hh in range(2)]

    def block(j, carry, diag):
        rows = pl.ds(pl.multiple_of(j * BLK, BLK), BLK)
        kj = k_ref[rows, :]
        vj = v_ref[rows, :]
        new = []
        for hh in range(2):
            c, acc = carry[hh]
            k = kj[:, hh * DH:(hh + 1) * DH].astype(BF16)
            v = vj[:, hh * DH:(hh + 1) * DH].astype(BF16)
            z = _dot_nt(qs[hh], k)
            lom = -(jnp.maximum(z, 0.0) + jnp.log(1.0 + jnp.exp(-jnp.abs(z))))
            lom_m = jnp.where(causal, lom, 0.0) if diag else lom
            hi, lo = _split2(lom_m)
            suffix = _dot(hi, later) + _dot(lo, later)
            a = jnp.exp(z + lom + suffix + c)
            if diag:
                a = jnp.where(causal, a, 0.0)
            acc = acc + _dot(a.astype(BF16), v)
            c = c + jnp.sum(lom_m, axis=-1, keepdims=True)
            new.append((c, acc))
        return tuple(new)

    zero = (jnp.zeros((BLK, 1), F32), jnp.zeros((BLK, DH), F32))
    carry = block(i, (zero, zero), True)

    def cond(state):
        it, live, _ = state
        return (it < i) & live

    def body(state):
        it, _, cr = state
        cr = block(i - 1 - it, cr, False)
        c_max = jnp.maximum(jnp.max(cr[0][0]), jnp.max(cr[1][0]))
        return it + 1, c_max > SB_LOG_ZERO, cr

    _, _, carry = lax.while_loop(cond, body, (jnp.int32(0), jnp.bool_(True), carry))
    o_ref[...] = jnp.concatenate([carry[0][1], carry[1][1]], axis=1)


def _sb(proj, batch, seq):
    T = proj.shape[0]
    nq = seq // SB_BLOCK
    pairs = SB_HEADS // 2
    qo, ko, vo = PM_C // LANES, (PM_C + SB_HEADS * SB_DH) // LANES, (PM_C + 2 * SB_HEADS * SB_DH) // LANES
    return pl.pallas_call(
        _sb_kernel,
        grid=(batch, pairs, nq),
        in_specs=[pl.BlockSpec((SB_BLOCK, LANES), lambda b, p, i: (b * nq + i, qo + p)),
                  pl.BlockSpec((seq, LANES), lambda b, p, i: (b, ko + p)),
                  pl.BlockSpec((seq, LANES), lambda b, p, i: (b, vo + p))],
        out_specs=pl.BlockSpec((SB_BLOCK, LANES), lambda b, p, i: (b * nq + i, p)),
        out_shape=jax.ShapeDtypeStruct((T, SB_HEADS * SB_DH), F32),
        compiler_params=_cparams("parallel", "parallel", "arbitrary"),
    )(proj, proj, proj)


def _ret_kernel(qk_ref, v_ref, g_ref, cos_ref, sin_ref, nw_ref, o_ref, r_ref):
    tc = qk_ref.shape[0]
    C = RET_CHUNK
    half = RET_DK // 2

    @pl.when(pl.program_id(1) == 0)
    def _():
        r_ref[...] = jnp.zeros_like(r_ref)

    lane = lax.broadcasted_iota(jnp.int32, (C, LANES), 1)
    first = (lane & (RET_DK - 1)) < half
    ri = lax.broadcasted_iota(jnp.int32, (C, C), 0)
    ci = lax.broadcasted_iota(jnp.int32, (C, C), 1)
    diff = (ri - ci).astype(F32)
    causal = ri >= ci
    rowf = lax.broadcasted_iota(jnp.int32, (C, LANES), 0).astype(F32)
    nw = nw_ref[...]

    def chunk(c, _):
        rows = pl.ds(pl.multiple_of(c * C, C), C)
        cos = cos_ref[rows, :]
        sin = sin_ref[rows, :]

        def rope(x):
            sw = jnp.where(first, pltpu.roll(x, LANES - half, 1), pltpu.roll(x, half, 1))
            return x * cos + sw * sin

        for pair in range(RET_HEADS // 2):
            qp = rope(qk_ref[rows, pair * LANES:(pair + 1) * LANES])
            kp = rope(qk_ref[rows, RET_HEADS * RET_DK + pair * LANES:RET_HEADS * RET_DK + (pair + 1) * LANES])
            for hh in range(2):
                h = 2 * pair + hh
                lg = _RET_LOG_GAMMA[h]
                hs = slice(h * RET_DV, (h + 1) * RET_DV)
                q = qp[:, hh * RET_DK:(hh + 1) * RET_DK]
                k = kp[:, hh * RET_DK:(hh + 1) * RET_DK]
                vb = v_ref[rows, hs].astype(BF16)
                qb = q.astype(BF16)
                dmat = jnp.where(causal, jnp.exp(diff * lg), 0.0)
                intra = _dot_nt(qb, k.astype(BF16)) * dmat
                xi = jnp.exp((rowf + 1.0) * lg)
                zeta = jnp.exp((C - 1.0 - rowf[:, :RET_DK]) * lg)
                r = r_ref[h]
                o = _dot(intra.astype(BF16), vb) + _dot(qb, r.astype(BF16)) * xi
                r_ref[h] = r * math.exp(C * lg) + _dot((k * zeta).T.astype(BF16), vb)
                gg = g_ref[rows, hs]
                o_ref[rows, hs] = _rms(o, nw) * (gg * _sigmoid(gg))
        return 0

    lax.fori_loop(0, tc // C, chunk, 0)


def _ret(proj, cos_t, sin_t, norm_w, batch, seq, tc):
    T = proj.shape[0]
    n_s = seq // tc
    W = RET_HEADS * RET_DV
    return pl.pallas_call(
        _ret_kernel,
        grid=(batch, n_s),
        in_specs=[pl.BlockSpec((tc, W), lambda b, s: (b * n_s + s, PM_D_QK // W)),
                  pl.BlockSpec((tc, W), lambda b, s: (b * n_s + s, PM_D_V // W)),
                  pl.BlockSpec((tc, W), lambda b, s: (b * n_s + s, PM_D_G // W)),
                  pl.BlockSpec((tc, LANES), lambda b, s: (s, 0)),
                  pl.BlockSpec((tc, LANES), lambda b, s: (s, 0)),
                  pl.BlockSpec((1, RET_DV), lambda b, s: (0, 0))],
        out_specs=pl.BlockSpec((tc, W), lambda b, s: (b * n_s + s, 0)),
        out_shape=jax.ShapeDtypeStruct((T, W), F32),
        scratch_shapes=[pltpu.VMEM((RET_HEADS, RET_DK, RET_DV), F32)],
        compiler_params=_cparams("parallel", "arbitrary"),
    )(proj, proj, proj, cos_t, sin_t, norm_w)


def _merge_kernel(x_ref, oa_ref, ob0_ref, ob1_ref, ob2_ref, l0_ref, l1_ref, l2_ref, oc_ref, od_ref,
                  ga_ref, gb_ref, gc_ref, gd_ref, wa_ref, wb_ref, wc_ref, wd_ref, wo_ref, nw_ref, out_ref):
    l0, l1, l2 = l0_ref[...], l1_ref[...], l2_ref[...]
    m = jnp.maximum(jnp.maximum(l0, l1), l2)
    e0, e1, e2 = jnp.exp(l0 - m), jnp.exp(l1 - m), jnp.exp(l2 - m)
    inv = 1.0 / (e0 + e1 + e2)
    ob = jnp.concatenate([ob0_ref[...] * (e0 * inv), ob1_ref[...] * (e1 * inv), ob2_ref[...] * (e2 * inv)],
                         axis=1).astype(BF16)
    y = _sigmoid(ga_ref[...]) * _dot(oa_ref[...].astype(BF16), wa_ref[...])
    y = y + _sigmoid(gb_ref[...]) * _dot(ob, wb_ref[...])
    y = y + _sigmoid(gc_ref[...]) * _dot(oc_ref[...].astype(BF16), wc_ref[...])
    y = y + _sigmoid(gd_ref[...]) * _dot(od_ref[...].astype(BF16), wd_ref[...])
    mixed = _dot(y.astype(BF16), wo_ref[...])
    out_ref[...] = x_ref[...] + _rms(mixed, nw_ref[...])


def _merge(x, oa, obs, lses, oc, od, proj, wa, wb, wc, wd, wo, nw, tm):
    T, D = x.shape
    row = lambda w: pl.BlockSpec((tm, w), lambda i: (i, 0))
    gate = lambda br: pl.BlockSpec((tm, D), lambda i: (i, PM_GATES // D + br))
    full = lambda a: pl.BlockSpec(a.shape, lambda i: (0, 0))
    return pl.pallas_call(
        _merge_kernel,
        grid=(T // tm,),
        in_specs=[row(D), row(oa.shape[1])] + [row(DSWA_OW)] * 6 + [row(oc.shape[1]), row(od.shape[1])]
                 + [gate(0), gate(1), gate(2), gate(3)] + [full(wa), full(wb), full(wc), full(wd), full(wo), full(nw)],
        out_specs=row(D),
        out_shape=jax.ShapeDtypeStruct((T, D), F32),
        compiler_params=_cparams("parallel"),
    )(x, oa, *obs, *lses, oc, od, proj, proj, proj, proj, wa, wb, wc, wd, wo, nw)


def _mlp_kernel(x_ref, n1_ref, w1_ref, w2_ref, n2_ref, out_ref, h_ref, acc_ref):
    f = pl.program_id(1)

    @pl.when(f == 0)
    def _():
        h_ref[...] = _rms(x_ref[...], n1_ref[...]).astype(BF16)
        acc_ref[...] = jnp.zeros_like(acc_ref)

    hid = jnp.maximum(_dot(h_ref[...], w1_ref[...]), 0.0)
    acc_ref[...] += _dot((hid * hid).astype(BF16), w2_ref[...])

    @pl.when(f == pl.num_programs(1) - 1)
    def _():
        out_ref[...] = x_ref[...] + _rms(acc_ref[...], n2_ref[...])


def _mlp(x, n1, w1, w2, n2, tm, tf):
    T, D = x.shape
    F = w1.shape[1]
    return pl.pallas_call(
        _mlp_kernel,
        grid=(T // tm, F // tf),
        in_specs=[pl.BlockSpec((tm, D), lambda i, f: (i, 0)),
                  pl.BlockSpec((1, D), lambda i, f: (0, 0)),
                  pl.BlockSpec((D, tf), lambda i, f: (0, f)),
                  pl.BlockSpec((tf, D), lambda i, f: (f, 0)),
                  pl.BlockSpec((1, D), lambda i, f: (0, 0))],
        out_specs=pl.BlockSpec((tm, D), lambda i, f: (i, 0)),
        out_shape=jax.ShapeDtypeStruct((T, D), F32),
        scratch_shapes=[pltpu.VMEM((tm, D), BF16), pltpu.VMEM((tm, D), F32)],
        compiler_params=_cparams("parallel", "arbitrary"),
    )(x, n1, w1, w2, n2)


def _prep_w_in(w_in):
    sec = [w_in[:, _COL_OFF[i]:_COL_OFF[i + 1]] for i in range(len(_COL_SIZES))]
    a_qkv, a_z, a_a, a_b, b_qkv, c_qkv, d_qk, d_v, d_g, gates = sec
    sbw = SB_HEADS * SB_DH
    c_qkv = jnp.concatenate([c_qkv[:, :sbw] * SB_DH ** -0.5, c_qkv[:, sbw:]], axis=1)
    rw = RET_HEADS * RET_DK
    d_qk = jnp.concatenate([d_qk[:, :rw], d_qk[:, rw:] * RET_DK ** -0.5], axis=1)
    w_main = jnp.concatenate([a_qkv, a_z, c_qkv, d_qk, d_v, d_g, gates], axis=1).astype(BF16)
    bw = DSWA_HEADS * DSWA_DH
    gw = DSWA_HPG * DSWA_DH
    K = w_in.shape[0]
    groups = []
    for g in range(len(DSWA_GROUPS)):
        q = b_qkv[:, g * gw:(g + 1) * gw] * DSWA_DH ** -0.5
        k = b_qkv[:, bw + g * gw: bw + (g + 1) * gw]
        v = b_qkv[:, 2 * bw + g * gw: 2 * bw + (g + 1) * gw]
        groups += [q, k, v, jnp.zeros((K, DSWA_GW - 3 * gw), F32)]
    w_b = jnp.concatenate(groups + [a_a, a_b, jnp.zeros((K, LANES - 2 * GDN_HEADS), F32)], axis=1).astype(BF16)
    return w_main, w_b


def _rope_tables(seq):
    pos = jnp.arange(seq, dtype=jnp.int32).astype(F32)[:, None]
    inv_b = DSWA_ROPE_THETA ** (-jnp.arange(0, DSWA_ROPE_DIM, 2, dtype=F32) / DSWA_ROPE_DIM)
    ang = pos * inv_b[None, :]
    cb, sb = jnp.cos(ang), jnp.sin(ang)
    rest = DSWA_DH - DSWA_ROPE_DIM
    cos_b = jnp.tile(jnp.concatenate([cb, cb, jnp.ones((seq, rest), F32)], axis=1), (1, LANES // DSWA_DH))
    sin_b = jnp.tile(jnp.concatenate([-sb, sb, jnp.zeros((seq, rest), F32)], axis=1), (1, LANES // DSWA_DH))
    inv_r = RET_THETA ** (-jnp.linspace(0.0, 1.0, RET_DK // 2, dtype=F32))
    ang = pos * inv_r[None, :]
    cr, sr = jnp.cos(ang), jnp.sin(ang)
    cos_r = jnp.tile(jnp.concatenate([cr, cr], axis=1), (1, LANES // RET_DK))
    sin_r = jnp.tile(jnp.concatenate([-sr, sr], axis=1), (1, LANES // RET_DK))
    return cos_b, sin_b, cos_r, sin_r


def _lane_vec(v):
    return jnp.zeros((1, LANES), F32).at[0, :v.shape[0]].set(v.astype(F32))


def _layer(x, batch, seq, tabs, n_pre_mix, n_post_mix, n_pre_mlp, n_post_mlp, w_in, conv_w, a_log, dt_bias,
           gdn_norm, ret_norm, w_br_a, w_br_b, w_br_c, w_br_d, w_o, w_mlp_in, w_mlp_out):
    T = x.shape[0]
    cos_b, sin_b, cos_r, sin_r = tabs
    w_main, w_b = _prep_w_in(w_in)
    gain = n_pre_mix.reshape(1, -1)
    proj = _norm_matmul(x, gain, w_main, tm=min(1024, T), tn=512)
    g0, g1, g2, ab = _dswa_proj(x, gain, w_b, cos_b, sin_b, seq, tm=min(512, seq))

    cw = jnp.zeros((8, conv_w.shape[1]), F32).at[:GDN_CONV].set(conv_w)
    oa = _gdn(proj, ab, cw, _lane_vec(a_log), _lane_vec(dt_bias), gdn_norm.reshape(1, -1), batch, seq,
              tc=min(256, seq))
    obs, lses = [], []
    for grp, (_, dil) in zip((g0, g1, g2), DSWA_GROUPS):
        o, lse = _dswa(grp, batch, seq, dil)
        obs.append(o)
        lses.append(lse)
    oc = _sb(proj, batch, seq)
    od = _ret(proj, cos_r, sin_r, ret_norm.reshape(1, -1), batch, seq, tc=min(512, seq))

    gw = DSWA_HPG * DSWA_DH
    wb = jnp.concatenate(
        [jnp.concatenate([w_br_b[g * gw:(g + 1) * gw], jnp.zeros((DSWA_OW - gw, D_MODEL), F32)], axis=0)
         for g in range(len(DSWA_GROUPS))], axis=0).astype(BF16)
    x1 = _merge(x, oa, obs, lses, oc, od, proj, w_br_a.astype(BF16), wb, w_br_c.astype(BF16),
                w_br_d.astype(BF16), w_o.astype(BF16), n_post_mix.reshape(1, -1), tm=min(256, T))
    return _mlp(x1, n_pre_mlp.reshape(1, -1), w_mlp_in.astype(BF16), w_mlp_out.astype(BF16),
                n_post_mlp.reshape(1, -1), tm=min(1024, T), tf=512)


def kernel(x, norm_pre_mix, norm_post_mix, norm_pre_mlp, norm_post_mlp, w_in, conv_w, a_log, dt_bias, gdn_norm,
           ret_norm, w_br_a, w_br_b, w_br_c, w_br_d, w_o, w_mlp_in, w_mlp_out):
    batch, seq, d = x.shape
    tabs = _rope_tables(seq)
    h = x.reshape(batch * seq, d)
    for l in range(norm_pre_mix.shape[0]):
        h = _layer(h, batch, seq, tabs, norm_pre_mix[l], norm_post_mix[l], norm_pre_mlp[l], norm_post_mlp[l],
                   w_in[l], conv_w[l], a_log[l], dt_bias[l], gdn_norm[l], ret_norm[l], w_br_a[l], w_br_b[l],
                   w_br_c[l], w_br_d[l], w_o[l], w_mlp_in[l], w_mlp_out[l])
    return h.reshape(batch, seq, d)
```

```python
import functools
import math

import numpy as np
import jax
import jax.numpy as jnp
from jax import lax
from jax.experimental import pallas as pl
from jax.experimental.pallas import tpu as pltpu

F32 = jnp.float32
BF16 = jnp.bfloat16

D_MODEL = 1024
N_LAYERS = 2
GDN_HEADS, GDN_DK, GDN_DV, GDN_CONV, GDN_CHUNK = 4, 128, 128, 4, 64
DSWA_GROUPS = ((128, 1), (512, 4), (2048, 16))
DSWA_HPG, DSWA_DH, DSWA_BLOCK = 3, 64, 128
DSWA_HEADS = DSWA_HPG * len(DSWA_GROUPS)
DSWA_ROPE_THETA, DSWA_ROPE_DIM = 500000.0, DSWA_DH // 4
SB_HEADS, SB_DH, SB_BLOCK = 8, 64, 128
SB_LOG_ZERO = -110.0
RET_HEADS, RET_DK, RET_DV, RET_THETA = 4, 64, 128, 10000.0
RET_CHUNK = 128
D_FF = 4 * D_MODEL
NORM_EPS = 1e-6
L2_EPS = 1e-6

_COL_SIZES = (1536, 512, 4, 4, 1728, 1536, 512, 512, 512, 4096)
_COL_OFF = np.concatenate([[0], np.cumsum(_COL_SIZES)])

PM_A_QKV, PM_A_Z, PM_C, PM_D_QK, PM_D_V, PM_D_G, PM_GATES, PM_WIDTH = 0, 1536, 2048, 3584, 4096, 4608, 5120, 9216
DSWA_GW = 640
DSWA_OW = 256

VMEM_LIMIT = 48 * 1024 * 1024
LANES = 128

_RET_LOG_GAMMA = [float(np.log1p(-np.exp2(np.float32(-5.0 - h))).astype(np.float32)) for h in range(RET_HEADS)]


def _cparams(*sem):
    return pltpu.CompilerParams(dimension_semantics=sem, vmem_limit_bytes=VMEM_LIMIT)


def _sigmoid(x):
    return 1.0 / (1.0 + jnp.exp(-x))


def _softplus(x):
    return jnp.maximum(x, 0.0) + jnp.log(1.0 + jnp.exp(-jnp.abs(x)))


def _dot(a, b):
    return jnp.dot(a, b, preferred_element_type=F32)


def _dot_nt(a, b):
    return lax.dot_general(a, b, (((1,), (1,)), ((), ())), preferred_element_type=F32)


def _split3(x):
    hi = x.astype(BF16)
    r = x - hi.astype(F32)
    mid = r.astype(BF16)
    lo = (r - mid.astype(F32)).astype(BF16)
    return hi, mid, lo


def _dot_exact_lhs(a_bf16, x):
    hi, mid, lo = _split3(x)
    return _dot(a_bf16, hi) + (_dot(a_bf16, mid) + _dot(a_bf16, lo))


def _dot_f32(a, b):
    ah, am, al = _split3(a)
    bh, bm, bl = _split3(b)
    return (_dot(ah, bh) + (_dot(ah, bm) + _dot(am, bh))
            + (_dot(am, bm) + _dot(ah, bl) + _dot(al, bh)))


def _rms(x, w):
    ms = jnp.mean(x * x, axis=-1, keepdims=True)
    return x * lax.rsqrt(ms + NORM_EPS) * w


def _norm_matmul_kernel(x_ref, g_ref, w_ref, o_ref, h_ref):
    @pl.when(pl.program_id(1) == 0)
    def _():
        h_ref[...] = _rms(x_ref[...], g_ref[...]).astype(BF16)

    o_ref[...] = _dot(h_ref[...], w_ref[...]).astype(o_ref.dtype)


def _norm_matmul(x, gain, w, tm, tn):
    T, K = x.shape
    N = w.shape[1]
    return pl.pallas_call(
        _norm_matmul_kernel,
        grid=(T // tm, N // tn),
        in_specs=[pl.BlockSpec((tm, K), lambda i, j: (i, 0)),
                  pl.BlockSpec((1, K), lambda i, j: (0, 0)),
                  pl.BlockSpec((K, tn), lambda i, j: (0, j))],
        out_specs=pl.BlockSpec((tm, tn), lambda i, j: (i, j)),
        out_shape=jax.ShapeDtypeStruct((T, N), F32),
        scratch_shapes=[pltpu.VMEM((tm, K), BF16)],
        compiler_params=_cparams("parallel", "arbitrary"),
    )(x, gain, w)


def _dswa_proj_kernel(x_ref, g_ref, w_ref, cos_ref, sin_ref, o0_ref, o1_ref, o2_ref, oab_ref):
    h = _rms(x_ref[...], g_ref[...]).astype(BF16)
    p = _dot(h, w_ref[...])
    cos = cos_ref[...]
    sin = sin_ref[...]
    lane = lax.broadcasted_iota(jnp.int32, cos.shape, 1)
    first = (lane & (DSWA_DH - 1)) < (DSWA_ROPE_DIM // 2)
    for g, o_ref in enumerate((o0_ref, o1_ref, o2_ref)):
        base = g * DSWA_GW
        for t in range(3):
            xt = p[:, base + t * LANES: base + (t + 1) * LANES]
            sw = jnp.where(first, pltpu.roll(xt, LANES - DSWA_ROPE_DIM // 2, 1),
                           pltpu.roll(xt, DSWA_ROPE_DIM // 2, 1))
            o_ref[:, t * LANES:(t + 1) * LANES] = xt * cos + sw * sin
        o_ref[:, 3 * LANES:DSWA_GW] = p[:, base + 3 * LANES: base + DSWA_GW]
    oab_ref[...] = p[:, 3 * DSWA_GW:]


def _dswa_proj(x, gain, w, cos_t, sin_t, seq, tm):
    T, K = x.shape
    N = w.shape[1]
    n_s = seq // tm
    grp = jax.ShapeDtypeStruct((T, DSWA_GW), F32)
    return pl.pallas_call(
        _dswa_proj_kernel,
        grid=(T // tm,),
        in_specs=[pl.BlockSpec((tm, K), lambda i: (i, 0)),
                  pl.BlockSpec((1, K), lambda i: (0, 0)),
                  pl.BlockSpec((K, N), lambda i: (0, 0)),
                  pl.BlockSpec((tm, LANES), lambda i: (i % n_s, 0)),
                  pl.BlockSpec((tm, LANES), lambda i: (i % n_s, 0))],
        out_specs=[pl.BlockSpec((tm, DSWA_GW), lambda i: (i, 0))] * 3
                  + [pl.BlockSpec((tm, LANES), lambda i: (i, 0))],
        out_shape=[grp, grp, grp, jax.ShapeDtypeStruct((T, LANES), F32)],
        compiler_params=_cparams("parallel"),
    )(x, gain, w, cos_t, sin_t)


def _per_head_matmul(x_cat, p_cat, diag_mask):
    p_bd = jnp.where(diag_mask, jnp.concatenate([p_cat] * GDN_HEADS, axis=0), 0.0)
    xh, xl = _split2(x_cat)
    ph, pl_ = _split2(p_bd)
    return _dot(xh, ph) + (_dot(xh, pl_) + _dot(xl, ph))


def _unit_lower_inverse(low_cat, eye_cat, diag_mask):
    x = eye_cat - low_cat
    p = _per_head_matmul(low_cat, low_cat, diag_mask)
    n = 2
    while True:
        x = x + _per_head_matmul(x, p, diag_mask)
        n *= 2
        if n >= GDN_CHUNK:
            break
        p = _per_head_matmul(p, p, diag_mask)
    return x


def _gdn_kernel(qkv_ref, z_ref, ab_ref, cw_ref, alog_ref, dtb_ref, nw_ref, o_ref,
                carry_ref, q_s, k_s, v_s, state_ref):
    tc = qkv_ref.shape[0]
    C = GDN_CHUNK
    H = GDN_HEADS
    HD = GDN_HEADS * GDN_DK

    @pl.when(pl.program_id(1) == 0)
    def _():
        carry_ref[...] = jnp.zeros_like(carry_ref)
        state_ref[...] = jnp.zeros_like(state_ref)

    for grp, dst in enumerate((q_s, k_s, v_s)):
        cols = slice(grp * HD, (grp + 1) * HD)
        x = qkv_ref[:, cols]
        xf = jnp.concatenate([carry_ref[:, cols], x], axis=0)
        y = x * cw_ref[GDN_CONV - 1:GDN_CONV, cols]
        for j in range(1, GDN_CONV):
            y = y + pltpu.roll(xf, j, 0)[8:] * cw_ref[GDN_CONV - 1 - j:GDN_CONV - j, cols]
        y = y * _sigmoid(y)
        if grp == 2:
            dst[...] = y
        else:
            for h in range(GDN_HEADS):
                yh = y[:, h * GDN_DK:(h + 1) * GDN_DK]
                yh = yh * lax.rsqrt(jnp.sum(yh * yh, axis=-1, keepdims=True) + L2_EPS)
                if grp == 0:
                    yh = yh * (GDN_DK ** -0.5)
                dst[:, h * GDN_DK:(h + 1) * GDN_DK] = yh
    carry_ref[...] = qkv_ref[tc - 8:tc, :]

    ri = lax.broadcasted_iota(jnp.int32, (C, C), 0)
    ci = lax.broadcasted_iota(jnp.int32, (C, C), 1)
    incl = ri >= ci
    strict = ri > ci
    a_low = jnp.where(incl, 1.0, 0.0).astype(BF16)
    eye_cat = jnp.concatenate([jnp.where(ri == ci, 1.0, 0.0).astype(F32)] * H, axis=1)
    rb = lax.broadcasted_iota(jnp.int32, (H * C, H * C), 0) // C
    cb = lax.broadcasted_iota(jnp.int32, (H * C, H * C), 1) // C
    diag_mask = rb == cb
    neg_exp_alog = -jnp.exp(alog_ref[...])
    dtb = dtb_ref[...]
    nw = nw_ref[...]
    n_chunks = tc // C
    heads = range(H)
    hsl = [slice(h * GDN_DK, (h + 1) * GDN_DK) for h in heads]

    pre = []
    for c in range(n_chunks):
        rows = slice(c * C, (c + 1) * C)
        ab = ab_ref[rows, :]
        gv = neg_exp_alog * _softplus(ab + dtb)
        bv = _sigmoid(ab)
        gc_all = _dot_exact_lhs(a_low, gv)
        gc_t = gc_all.T
        q = [q_s[rows, hsl[h]] for h in heads]
        k = [k_s[rows, hsl[h]] for h in heads]
        v = [v_s[rows, hsl[h]] for h in heads]
        gc = [jnp.broadcast_to(gc_all[:, h:h + 1], (C, GDN_DK)) for h in heads]
        beta = [jnp.broadcast_to(bv[:, H + h:H + h + 1], (C, GDN_DK)) for h in heads]
        decay = [jnp.exp(jnp.where(incl, gc[h][:, :C] - gc_t[h:h + 1, :], -jnp.inf)) for h in heads]
        egc = [jnp.exp(gc[h]) for h in heads]
        gl = [gc[h][C - 1:C, :] for h in heads]
        kb = [k[h] * beta[h] for h in heads]
        kbf = [k[h].astype(BF16) for h in heads]
        low_cat = jnp.concatenate(
            [jnp.where(strict, _dot_nt(kb[h].astype(BF16), kbf[h]) * decay[h], 0.0) for h in heads], axis=1)
        t_cat = _unit_lower_inverse(low_cat, eye_cat, diag_mask).astype(BF16)
        uw = [_dot(t_cat[:, h * C:(h + 1) * C],
                   jnp.concatenate([v[h] * beta[h], kb[h] * egc[h]], axis=1).astype(BF16)) for h in heads]
        attn = [(_dot_nt(q[h].astype(BF16), kbf[h]) * decay[h]).astype(BF16) for h in heads]
        q_dec = [(q[h] * egc[h]).astype(BF16) for h in heads]
        k_dec_t = [(k[h] * jnp.exp(gl[h] - gc[h])).T.astype(BF16) for h in heads]
        pre.append((uw, attn, q_dec, k_dec_t, [jnp.exp(gl[h]) for h in heads]))

    st = [state_ref[h] for h in heads]
    for c in range(n_chunks):
        rows = slice(c * C, (c + 1) * C)
        uw, attn, q_dec, k_dec_t, egl = pre[c]
        for h in heads:
            stb = st[h].astype(BF16)
            v_new = uw[h][:, :GDN_DV] - _dot(uw[h][:, GDN_DV:].astype(BF16), stb)
            vnb = v_new.astype(BF16)
            o = _dot(q_dec[h], stb) + _dot(attn[h], vnb)
            st[h] = st[h] * egl[h] + _dot(k_dec_t[h], vnb)
            zz = z_ref[rows, hsl[h]]
            o_ref[rows, hsl[h]] = _rms(o, nw) * (zz * _sigmoid(zz))
    for h in heads:
        state_ref[h] = st[h]


def _gdn(proj, ab, conv_w, alog_v, dtb_v, norm_w, batch, seq, tc):
    T = proj.shape[0]
    n_s = seq // tc
    HD = GDN_HEADS * GDN_DK
    return pl.pallas_call(
        _gdn_kernel,
        grid=(batch, n_s),
        in_specs=[pl.BlockSpec((tc, 3 * HD), lambda b, s: (b * n_s + s, PM_A_QKV // (3 * HD))),
                  pl.BlockSpec((tc, HD), lambda b, s: (b * n_s + s, PM_A_Z // HD)),
                  pl.BlockSpec((tc, LANES), lambda b, s: (b * n_s + s, 0)),
                  pl.BlockSpec((8, 3 * HD), lambda b, s: (0, 0)),
                  pl.BlockSpec((1, LANES), lambda b, s: (0, 0)),
                  pl.BlockSpec((1, LANES), lambda b, s: (0, 0)),
                  pl.BlockSpec((1, GDN_DV), lambda b, s: (0, 0))],
        out_specs=pl.BlockSpec((tc, HD), lambda b, s: (b * n_s + s, 0)),
        out_shape=jax.ShapeDtypeStruct((T, HD), F32),
        scratch_shapes=[pltpu.VMEM((8, 3 * HD), F32),
                        pltpu.VMEM((tc, HD), F32), pltpu.VMEM((tc, HD), F32), pltpu.VMEM((tc, HD), F32),
                        pltpu.VMEM((GDN_HEADS, GDN_DK, GDN_DV), F32)],
        compiler_params=_cparams("parallel", "arbitrary"),
    )(proj, proj, ab, conv_w, alog_v, dtb_v, norm_w)


def _dswa_kernel(cur_ref, prev_ref, o_ref, lse_ref):
    BL = DSWA_BLOCK
    n = pl.program_id(2)
    cur = cur_ref[0]
    prev = prev_ref[0]
    ii = lax.broadcasted_iota(jnp.int32, (BL, 2 * BL), 0)
    jj = lax.broadcasted_iota(jnp.int32, (BL, 2 * BL), 1)
    first_valid = jnp.where(n > 0, 0, BL)
    mask = (jj >= ii) & (jj <= ii + BL) & (jj >= first_valid)
    kw = DSWA_HPG * DSWA_DH
    for h in range(DSWA_HPG):
        hs = slice(h * DSWA_DH, (h + 1) * DSWA_DH)
        ks = slice(kw + h * DSWA_DH, kw + (h + 1) * DSWA_DH)
        vs = slice(2 * kw + h * DSWA_DH, 2 * kw + (h + 1) * DSWA_DH)
        q = cur[:, hs].astype(BF16)
        k = jnp.concatenate([prev[:, ks], cur[:, ks]], axis=0).astype(BF16)
        v = jnp.concatenate([prev[:, vs], cur[:, vs]], axis=0).astype(BF16)
        s = jnp.where(mask, _dot_nt(q, k), -jnp.inf)
        m = jnp.max(s, axis=-1, keepdims=True)
        p = jnp.exp(s - m)
        l = jnp.sum(p, axis=-1, keepdims=True)
        o_ref[0, :, hs] = _dot(p.astype(BF16), v) * (1.0 / l)
        lse_ref[0, :, hs] = jnp.broadcast_to(m + jnp.log(l), (BL, DSWA_DH))
    o_ref[0, :, kw:] = jnp.zeros((BL, DSWA_OW - kw), F32)
    lse_ref[0, :, kw:] = jnp.zeros((BL, DSWA_OW - kw), F32)


def _dswa(grp, batch, seq, dil):
    T = grp.shape[0]
    L = seq // dil
    nb = L // DSWA_BLOCK
    g3 = grp.reshape(batch, L, dil * DSWA_GW)
    out = jax.ShapeDtypeStruct((batch, L, dil * DSWA_OW), F32)
    o, lse = pl.pallas_call(
        _dswa_kernel,
        grid=(batch, dil, nb),
        in_specs=[pl.BlockSpec((1, DSWA_BLOCK, DSWA_GW), lambda b, r, n: (b, n, r)),
                  pl.BlockSpec((1, DSWA_BLOCK, DSWA_GW), lambda b, r, n: (b, jnp.maximum(n - 1, 0), r))],
        out_specs=[pl.BlockSpec((1, DSWA_BLOCK, DSWA_OW), lambda b, r, n: (b, n, r))] * 2,
        out_shape=[out, out],
        compiler_params=_cparams("parallel", "parallel", "arbitrary"),
    )(g3, g3)
    return o.reshape(T, DSWA_OW), lse.reshape(T, DSWA_OW)


def _split2(x):
    hi = x.astype(BF16)
    lo = (x - hi.astype(F32)).astype(BF16)
    return hi, lo


def _sb_kernel(q_ref, k_ref, v_ref, o_ref):
    BLK = SB_BLOCK
    DH = SB_DH
    i = pl.program_id(2)
    tt = lax.broadcasted_iota(jnp.int32, (BLK, BLK), 0)
    ss = lax.broadcasted_iota(jnp.int32, (BLK, BLK), 1)
    later = jnp.where(tt > ss, 1.0, 0.0).astype(BF16)
    causal = ss < tt
    q2 = q_ref[...]
    qs = [q2[:, hh * DH:(hh + 1) * DH].astype(BF16) for hh in range(2)]

    def block(j, carry, diag):
        rows = pl.ds(pl.multiple_of(j * BLK, BLK), BLK)
        kj = k_ref[rows, :]
        vj = v_ref[rows, :]
        new = []
        for hh in range(2):
            c, acc = carry[hh]
            k = kj[:, hh * DH:(hh + 1) * DH].astype(BF16)
            v = vj[:, hh * DH:(hh + 1) * DH].astype(BF16)
            z = _dot_nt(qs[hh], k)
            lom = -(jnp.maximum(z, 0.0) + jnp.log(1.0 + jnp.exp(-jnp.abs(z))))
            lom_m = jnp.where(causal, lom, 0.0) if diag else lom
            hi, lo = _split2(lom_m)
            suffix = _dot(hi, later) + _dot(lo, later)
            a = jnp.exp(z + lom + suffix + c)
            if diag:
                a = jnp.where(causal, a, 0.0)
            acc = acc + _dot(a.astype(BF16), v)
            c = c + jnp.sum(lom_m, axis=-1, keepdims=True)
            new.append((c, acc))
        return tuple(new)

    zero = (jnp.zeros((BLK, 1), F32), jnp.zeros((BLK, DH), F32))
    carry = block(i, (zero, zero), True)

    def cond(state):
        it, live, _ = state
        return (it < i) & live

    def body(state):
        it, _, cr = state
        cr = block(i - 1 - it, cr, False)
        c_max = jnp.maximum(jnp.max(cr[0][0]), jnp.max(cr[1][0]))
        return it + 1, c_max > SB_LOG_ZERO, cr

    _, _, carry = lax.while_loop(cond, body, (jnp.int32(0), jnp.bool_(True), carry))
    o_ref[...] = jnp.concatenate([carry[0][1], carry[1][1]], axis=1)


def _sb(proj, batch, seq):
    T = proj.shape[0]
    nq = seq // SB_BLOCK
    pairs = SB_HEADS // 2
    qo, ko, vo = PM_C // LANES, (PM_C + SB_HEADS * SB_DH) // LANES, (PM_C + 2 * SB_HEADS * SB_DH) // LANES
    return pl.pallas_call(
        _sb_kernel,
        grid=(batch, pairs, nq),
        in_specs=[pl.BlockSpec((SB_BLOCK, LANES), lambda b, p, i: (b * nq + i, qo + p)),
                  pl.BlockSpec((seq, LANES), lambda b, p, i: (b, ko + p)),
                  pl.BlockSpec((seq, LANES), lambda b, p, i: (b, vo + p))],
        out_specs=pl.BlockSpec((SB_BLOCK, LANES), lambda b, p, i: (b * nq + i, p)),
        out_shape=jax.ShapeDtypeStruct((T, SB_HEADS * SB_DH), F32),
        compiler_params=_cparams("parallel", "parallel", "arbitrary"),
    )(proj, proj, proj)


def _ret_kernel(qk_ref, v_ref, g_ref, cos_ref, sin_ref, nw_ref, o_ref, r_ref):
    tc = qk_ref.shape[0]
    C = RET_CHUNK
    half = RET_DK // 2

    @pl.when(pl.program_id(1) == 0)
    def _():
        r_ref[...] = jnp.zeros_like(r_ref)

    lane = lax.broadcasted_iota(jnp.int32, (C, LANES), 1)
    first = (lane & (RET_DK - 1)) < half
    ri = lax.broadcasted_iota(jnp.int32, (C, C), 0)
    ci = lax.broadcasted_iota(jnp.int32, (C, C), 1)
    diff = (ri - ci).astype(F32)
    causal = ri >= ci
    rowf = lax.broadcasted_iota(jnp.int32, (C, LANES), 0).astype(F32)
    nw = nw_ref[...]

    def chunk(c, _):
        rows = pl.ds(pl.multiple_of(c * C, C), C)
        cos = cos_ref[rows, :]
        sin = sin_ref[rows, :]

        def rope(x):
            sw = jnp.where(first, pltpu.roll(x, LANES - half, 1), pltpu.roll(x, half, 1))
            return x * cos + sw * sin

        for pair in range(RET_HEADS // 2):
            qp = rope(qk_ref[rows, pair * LANES:(pair + 1) * LANES])
            kp = rope(qk_ref[rows, RET_HEADS * RET_DK + pair * LANES:RET_HEADS * RET_DK + (pair + 1) * LANES])
            for hh in range(2):
                h = 2 * pair + hh
                lg = _RET_LOG_GAMMA[h]
                hs = slice(h * RET_DV, (h + 1) * RET_DV)
                q = qp[:, hh * RET_DK:(hh + 1) * RET_DK]
                k = kp[:, hh * RET_DK:(hh + 1) * RET_DK]
                vb = v_ref[rows, hs].astype(BF16)
                qb = q.astype(BF16)
                dmat = jnp.where(causal, jnp.exp(diff * lg), 0.0)
                intra = _dot_nt(qb, k.astype(BF16)) * dmat
                xi = jnp.exp((rowf + 1.0) * lg)
                zeta = jnp.exp((C - 1.0 - rowf[:, :RET_DK]) * lg)
                r = r_ref[h]
                o = _dot(intra.astype(BF16), vb) + _dot(qb, r.astype(BF16)) * xi
                r_ref[h] = r * math.exp(C * lg) + _dot((k * zeta).T.astype(BF16), vb)
                gg = g_ref[rows, hs]
                o_ref[rows, hs] = _rms(o, nw) * (gg * _sigmoid(gg))
        return 0

    lax.fori_loop(0, tc // C, chunk, 0)


def _ret(proj, cos_t, sin_t, norm_w, batch, seq, tc):
    T = proj.shape[0]
    n_s = seq // tc
    W = RET_HEADS * RET_DV
    return pl.pallas_call(
        _ret_kernel,
        grid=(batch, n_s),
        in_specs=[pl.BlockSpec((tc, W), lambda b, s: (b * n_s + s, PM_D_QK // W)),
                  pl.BlockSpec((tc, W), lambda b, s: (b * n_s + s, PM_D_V // W)),
                  pl.BlockSpec((tc, W), lambda b, s: (b * n_s + s, PM_D_G // W)),
                  pl.BlockSpec((tc, LANES), lambda b, s: (s, 0)),
                  pl.BlockSpec((tc, LANES), lambda b, s: (s, 0)),
                  pl.BlockSpec((1, RET_DV), lambda b, s: (0, 0))],
        out_specs=pl.BlockSpec((tc, W), lambda b, s: (b * n_s + s, 0)),
        out_shape=jax.ShapeDtypeStruct((T, W), F32),
        scratch_shapes=[pltpu.VMEM((RET_HEADS, RET_DK, RET_DV), F32)],
        compiler_params=_cparams("parallel", "arbitrary"),
    )(proj, proj, proj, cos_t, sin_t, norm_w)


def _merge_kernel(x_ref, oa_ref, ob0_ref, ob1_ref, ob2_ref, l0_ref, l1_ref, l2_ref, oc_ref, od_ref,
                  ga_ref, gb_ref, gc_ref, gd_ref, wa_ref, wb_ref, wc_ref, wd_ref, wo_ref, nw_ref, out_ref):
    l0, l1, l2 = l0_ref[...], l1_ref[...], l2_ref[...]
    m = jnp.maximum(jnp.maximum(l0, l1), l2)
    e0, e1, e2 = jnp.exp(l0 - m), jnp.exp(l1 - m), jnp.exp(l2 - m)
    inv = 1.0 / (e0 + e1 + e2)
    ob = jnp.concatenate([ob0_ref[...] * (e0 * inv), ob1_ref[...] * (e1 * inv), ob2_ref[...] * (e2 * inv)],
                         axis=1).astype(BF16)
    y = _sigmoid(ga_ref[...]) * _dot(oa_ref[...].astype(BF16), wa_ref[...])
    y = y + _sigmoid(gb_ref[...]) * _dot(ob, wb_ref[...])
    y = y + _sigmoid(gc_ref[...]) * _dot(oc_ref[...].astype(BF16), wc_ref[...])
    y = y + _sigmoid(gd_ref[...]) * _dot(od_ref[...].astype(BF16), wd_ref[...])
    mixed = _dot(y.astype(BF16), wo_ref[...])
    out_ref[...] = x_ref[...] + _rms(mixed, nw_ref[...])


def _merge(x, oa, obs, lses, oc, od, proj, wa, wb, wc, wd, wo, nw, tm):
    T, D = x.shape
    row = lambda w: pl.BlockSpec((tm, w), lambda i: (i, 0))
    gate = lambda br: pl.BlockSpec((tm, D), lambda i: (i, PM_GATES // D + br))
    full = lambda a: pl.BlockSpec(a.shape, lambda i: (0, 0))
    return pl.pallas_call(
        _merge_kernel,
        grid=(T // tm,),
        in_specs=[row(D), row(oa.shape[1])] + [row(DSWA_OW)] * 6 + [row(oc.shape[1]), row(od.shape[1])]
                 + [gate(0), gate(1), gate(2), gate(3)] + [full(wa), full(wb), full(wc), full(wd), full(wo), full(nw)],
        out_specs=row(D),
        out_shape=jax.ShapeDtypeStruct((T, D), F32),
        compiler_params=_cparams("parallel"),
    )(x, oa, *obs, *lses, oc, od, proj, proj, proj, proj, wa, wb, wc, wd, wo, nw)


def _mlp_kernel(x_ref, n1_ref, w1_ref, w2_ref, n2_ref, out_ref, h_ref, acc_ref):
    f = pl.program_id(1)

    @pl.when(f == 0)
    def _():
        h_ref[...] = _rms(x_ref[...], n1_ref[...]).astype(BF16)
        acc_ref[...] = jnp.zeros_like(acc_ref)

    hid = jnp.maximum(_dot(h_ref[...], w1_ref[...]), 0.0)
    acc_ref[...] += _dot((hid * hid).astype(BF16), w2_ref[...])

    @pl.when(f == pl.num_programs(1) - 1)
    def _():
        out_ref[...] = x_ref[...] + _rms(acc_ref[...], n2_ref[...])


def _mlp(x, n1, w1, w2, n2, tm, tf):
    T, D = x.shape
    F = w1.shape[1]
    return pl.pallas_call(
        _mlp_kernel,
        grid=(T // tm, F // tf),
        in_specs=[pl.BlockSpec((tm, D), lambda i, f: (i, 0)),
                  pl.BlockSpec((1, D), lambda i, f: (0, 0)),
                  pl.BlockSpec((D, tf), lambda i, f: (0, f)),
                  pl.BlockSpec((tf, D), lambda i, f: (f, 0)),
                  pl.BlockSpec((1, D), lambda i, f: (0, 0))],
        out_specs=pl.BlockSpec((tm, D), lambda i, f: (i, 0)),
        out_shape=jax.ShapeDtypeStruct((T, D), F32),
        scratch_shapes=[pltpu.VMEM((tm, D), BF16), pltpu.VMEM((tm, D), F32)],
        compiler_params=_cparams("parallel", "arbitrary"),
    )(x, n1, w1, w2, n2)


def _prep_w_in(w_in):
    sec = [w_in[:, _COL_OFF[i]:_COL_OFF[i + 1]] for i in range(len(_COL_SIZES))]
    a_qkv, a_z, a_a, a_b, b_qkv, c_qkv, d_qk, d_v, d_g, gates = sec
    sbw = SB_HEADS * SB_DH
    c_qkv = jnp.concatenate([c_qkv[:, :sbw] * SB_DH ** -0.5, c_qkv[:, sbw:]], axis=1)
    rw = RET_HEADS * RET_DK
    d_qk = jnp.concatenate([d_qk[:, :rw], d_qk[:, rw:] * RET_DK ** -0.5], axis=1)
    w_main = jnp.concatenate([a_qkv, a_z, c_qkv, d_qk, d_v, d_g, gates], axis=1).astype(BF16)
    bw = DSWA_HEADS * DSWA_DH
    gw = DSWA_HPG * DSWA_DH
    K = w_in.shape[0]
    groups = []
    for g in range(len(DSWA_GROUPS)):
        q = b_qkv[:, g * gw:(g + 1) * gw] * DSWA_DH ** -0.5
        k = b_qkv[:, bw + g * gw: bw + (g + 1) * gw]
        v = b_qkv[:, 2 * bw + g * gw: 2 * bw + (g + 1) * gw]
        groups += [q, k, v, jnp.zeros((K, DSWA_GW - 3 * gw), F32)]
    w_b = jnp.concatenate(groups + [a_a, a_b, jnp.zeros((K, LANES - 2 * GDN_HEADS), F32)], axis=1).astype(BF16)
    return w_main, w_b


def _rope_tables(seq):
    pos = jnp.arange(seq, dtype=jnp.int32).astype(F32)[:, None]
    inv_b = DSWA_ROPE_THETA ** (-jnp.arange(0, DSWA_ROPE_DIM, 2, dtype=F32) / DSWA_ROPE_DIM)
    ang = pos * inv_b[None, :]
    cb, sb = jnp.cos(ang), jnp.sin(ang)
    rest = DSWA_DH - DSWA_ROPE_DIM
    cos_b = jnp.tile(jnp.concatenate([cb, cb, jnp.ones((seq, rest), F32)], axis=1), (1, LANES // DSWA_DH))
    sin_b = jnp.tile(jnp.concatenate([-sb, sb, jnp.zeros((seq, rest), F32)], axis=1), (1, LANES // DSWA_DH))
    inv_r = RET_THETA ** (-jnp.linspace(0.0, 1.0, RET_DK // 2, dtype=F32))
    ang = pos * inv_r[None, :]
    cr, sr = jnp.cos(ang), jnp.sin(ang)
    cos_r = jnp.tile(jnp.concatenate([cr, cr], axis=1), (1, LANES // RET_DK))
    sin_r = jnp.tile(jnp.concatenate([-sr, sr], axis=1), (1, LANES // RET_DK))
    return cos_b, sin_b, cos_r, sin_r


def _lane_vec(v):
    return jnp.zeros((1, LANES), F32).at[0, :v.shape[0]].set(v.astype(F32))


def _layer(x, batch, seq, tabs, n_pre_mix, n_post_mix, n_pre_mlp, n_post_mlp, w_in, conv_w, a_log, dt_bias,
           gdn_norm, ret_norm, w_br_a, w_br_b, w_br_c, w_br_d, w_o, w_mlp_in, w_mlp_out):
    T = x.shape[0]
    cos_b, sin_b, cos_r, sin_r = tabs
    w_main, w_b = _prep_w_in(w_in)
    gain = n_pre_mix.reshape(1, -1)
    proj = _norm_matmul(x, gain, w_main, tm=min(1024, T), tn=512)
    g0, g1, g2, ab = _dswa_proj(x, gain, w_b, cos_b, sin_b, seq, tm=min(512, seq))

    cw = jnp.zeros((8, conv_w.shape[1]), F32).at[:GDN_CONV].set(conv_w)
    oa = _gdn(proj, ab, cw, _lane_vec(a_log), _lane_vec(dt_bias), gdn_norm.reshape(1, -1), batch, seq,
              tc=min(256, seq))
    obs, lses = [], []
    for grp, (_, dil) in zip((g0, g1, g2), DSWA_GROUPS):
        o, lse = _dswa(grp, batch, seq, dil)
        obs.append(o)
        lses.append(lse)
    oc = _sb(proj, batch, seq)
    od = _ret(proj, cos_r, sin_r, ret_norm.reshape(1, -1), batch, seq, tc=min(512, seq))

    gw = DSWA_HPG * DSWA_DH
    wb = jnp.concatenate(
        [jnp.concatenate([w_br_b[g * gw:(g + 1) * gw], jnp.zeros((DSWA_OW - gw, D_MODEL), F32)], axis=0)
         for g in range(len(DSWA_GROUPS))], axis=0).astype(BF16)
    x1 = _merge(x, oa, obs, lses, oc, od, proj, w_br_a.astype(BF16), wb, w_br_c.astype(BF16),
                w_br_d.astype(BF16), w_o.astype(BF16), n_post_mix.reshape(1, -1), tm=min(256, T))
    return _mlp(x1, n_pre_mlp.reshape(1, -1), w_mlp_in.astype(BF16), w_mlp_out.astype(BF16),
                n_post_mlp.reshape(1, -1), tm=min(1024, T), tf=512)


def kernel(x, norm_pre_mix, norm_post_mix, norm_pre_mlp, norm_post_mlp, w_in, conv_w, a_log, dt_bias, gdn_norm,
           ret_norm, w_br_a, w_br_b, w_br_c, w_br_d, w_o, w_mlp_in, w_mlp_out):
    batch, seq, d = x.shape
    tabs = _rope_tables(seq)
    h = x.reshape(batch * seq, d)
    for l in range(norm_pre_mix.shape[0]):
        h = _layer(h, batch, seq, tabs, norm_pre_mix[l], norm_post_mix[l], norm_pre_mlp[l], norm_post_mlp[l],
                   w_in[l], conv_w[l], a_log[l], dt_bias[l], gdn_norm[l], ret_norm[l], w_br_a[l], w_br_b[l],
                   w_br_c[l], w_br_d[l], w_o[l], w_mlp_in[l], w_mlp_out[l])
    return h.reshape(batch, seq, d)
```

```python
import functools
import math

import numpy as np
import jax
import jax.numpy as jnp
from jax import lax
from jax.experimental import pallas as pl
from jax.experimental.pallas import tpu as pltpu

F32 = jnp.float32
BF16 = jnp.bfloat16

D_MODEL = 1024
N_LAYERS = 2
GDN_HEADS, GDN_DK, GDN_DV, GDN_CONV, GDN_CHUNK = 4, 128, 128, 4, 64
DSWA_GROUPS = ((128, 1), (512, 4), (2048, 16))
DSWA_HPG, DSWA_DH, DSWA_BLOCK = 3, 64, 128
DSWA_HEADS = DSWA_HPG * len(DSWA_GROUPS)
DSWA_ROPE_THETA, DSWA_ROPE_DIM = 500000.0, DSWA_DH // 4
SB_HEADS, SB_DH, SB_BLOCK = 8, 64, 128
SB_LOG_ZERO = -110.0
SB_LOG_DEAD = -1e30
SB_QBLOCKS_PER_STEP = 8
DSWA_BLOCKS_PER_STEP = 4
RET_HEADS, RET_DK, RET_DV, RET_THETA = 4, 64, 128, 10000.0
RET_CHUNK = 128
D_FF = 4 * D_MODEL
NORM_EPS = 1e-6
L2_EPS = 1e-6

_COL_SIZES = (1536, 512, 4, 4, 1728, 1536, 512, 512, 512, 4096)
_COL_OFF = np.concatenate([[0], np.cumsum(_COL_SIZES)])

PM_A_QKV, PM_A_Z, PM_C, PM_D_QK, PM_D_V, PM_D_G, PM_GATES, PM_WIDTH = 0, 1536, 2048, 3584, 4096, 4608, 5120, 9216
DSWA_GW = 640
DSWA_OW = 256

VMEM_LIMIT = 48 * 1024 * 1024
LANES = 128

_RET_LOG_GAMMA = [float(np.log1p(-np.exp2(np.float32(-5.0 - h))).astype(np.float32)) for h in range(RET_HEADS)]


def _cparams(*sem):
    return pltpu.CompilerParams(dimension_semantics=sem, vmem_limit_bytes=VMEM_LIMIT)


def _sigmoid(x):
    return 1.0 / (1.0 + jnp.exp(-x))


def _softplus(x):
    return jnp.maximum(x, 0.0) + jnp.log(1.0 + jnp.exp(-jnp.abs(x)))


def _dot(a, b):
    return jnp.dot(a, b, preferred_element_type=F32)


def _dot_nt(a, b):
    return lax.dot_general(a, b, (((1,), (1,)), ((), ())), preferred_element_type=F32)


def _split3(x):
    hi = x.astype(BF16)
    r = x - hi.astype(F32)
    mid = r.astype(BF16)
    lo = (r - mid.astype(F32)).astype(BF16)
    return hi, mid, lo


def _dot_exact_lhs(a_bf16, x):
    hi, mid, lo = _split3(x)
    return _dot(a_bf16, hi) + (_dot(a_bf16, mid) + _dot(a_bf16, lo))


def _dot_f32(a, b):
    ah, am, al = _split3(a)
    bh, bm, bl = _split3(b)
    return (_dot(ah, bh) + (_dot(ah, bm) + _dot(am, bh))
            + (_dot(am, bm) + _dot(ah, bl) + _dot(al, bh)))


def _rms(x, w):
    ms = jnp.mean(x * x, axis=-1, keepdims=True)
    return x * lax.rsqrt(ms + NORM_EPS) * w


def _norm_matmul_kernel(x_ref, g_ref, w_ref, o_ref, h_ref):
    @pl.when(pl.program_id(1) == 0)
    def _():
        h_ref[...] = _rms(x_ref[...], g_ref[...]).astype(BF16)

    o_ref[...] = _dot(h_ref[...], w_ref[...]).astype(o_ref.dtype)


def _norm_matmul(x, gain, w, tm, tn):
    T, K = x.shape
    N = w.shape[1]
    return pl.pallas_call(
        _norm_matmul_kernel,
        grid=(T // tm, N // tn),
        in_specs=[pl.BlockSpec((tm, K), lambda i, j: (i, 0)),
                  pl.BlockSpec((1, K), lambda i, j: (0, 0)),
                  pl.BlockSpec((K, tn), lambda i, j: (0, j))],
        out_specs=pl.BlockSpec((tm, tn), lambda i, j: (i, j)),
        out_shape=jax.ShapeDtypeStruct((T, N), BF16),
        scratch_shapes=[pltpu.VMEM((tm, K), BF16)],
        compiler_params=_cparams("parallel", "arbitrary"),
    )(x, gain, w)


def _dswa_proj_kernel(x_ref, g_ref, w_ref, cos_ref, sin_ref, o0_ref, o1_ref, o2_ref, oab_ref):
    h = _rms(x_ref[...], g_ref[...]).astype(BF16)
    p = _dot(h, w_ref[...])
    cos = cos_ref[...]
    sin = sin_ref[...]
    lane = lax.broadcasted_iota(jnp.int32, cos.shape, 1)
    first = (lane & (DSWA_DH - 1)) < (DSWA_ROPE_DIM // 2)
    for g, o_ref in enumerate((o0_ref, o1_ref, o2_ref)):
        base = g * DSWA_GW
        for t in range(3):
            xt = p[:, base + t * LANES: base + (t + 1) * LANES]
            sw = jnp.where(first, pltpu.roll(xt, LANES - DSWA_ROPE_DIM // 2, 1),
                           pltpu.roll(xt, DSWA_ROPE_DIM // 2, 1))
            o_ref[:, t * LANES:(t + 1) * LANES] = xt * cos + sw * sin
        o_ref[:, 3 * LANES:DSWA_GW] = p[:, base + 3 * LANES: base + DSWA_GW]
    oab_ref[...] = p[:, 3 * DSWA_GW:]


def _dswa_proj(x, gain, w, cos_t, sin_t, seq, tm):
    T, K = x.shape
    N = w.shape[1]
    n_s = seq // tm
    grp = jax.ShapeDtypeStruct((T, DSWA_GW), F32)
    return pl.pallas_call(
        _dswa_proj_kernel,
        grid=(T // tm,),
        in_specs=[pl.BlockSpec((tm, K), lambda i: (i, 0)),
                  pl.BlockSpec((1, K), lambda i: (0, 0)),
                  pl.BlockSpec((K, N), lambda i: (0, 0)),
                  pl.BlockSpec((tm, LANES), lambda i: (i % n_s, 0)),
                  pl.BlockSpec((tm, LANES), lambda i: (i % n_s, 0))],
        out_specs=[pl.BlockSpec((tm, DSWA_GW), lambda i: (i, 0))] * 3
                  + [pl.BlockSpec((tm, LANES), lambda i: (i, 0))],
        out_shape=[grp, grp, grp, jax.ShapeDtypeStruct((T, LANES), F32)],
        compiler_params=_cparams("parallel"),
    )(x, gain, w, cos_t, sin_t)


def _per_head_matmul(x_cat, p_cat, diag_mask):
    xh, xl = _split2(x_cat)
    ph, pl_ = [jnp.concatenate([t] * GDN_HEADS, axis=0) * diag_mask for t in _split2(p_cat)]
    return _dot(xh, ph) + (_dot(xh, pl_) + _dot(xl, ph))


def _unit_lower_inverse(lows, eye_cat, diag_mask):
    xs = [eye_cat - low for low in lows]
    ps = [_per_head_matmul(low, low, diag_mask) for low in lows]
    n = 2
    while True:
        xs = [x + _per_head_matmul(x, p, diag_mask) for x, p in zip(xs, ps)]
        n *= 2
        if n >= GDN_CHUNK:
            break
        ps = [_per_head_matmul(p, p, diag_mask) for p in ps]
    return xs


def _gdn_kernel(qkv_ref, z_ref, ab_ref, cw_ref, alog_ref, dtb_ref, nw_ref, o_ref,
                carry_ref, q_s, k_s, v_s, state_ref):
    tc = qkv_ref.shape[0]
    C = GDN_CHUNK
    H = GDN_HEADS
    HD = GDN_HEADS * GDN_DK

    @pl.when(pl.program_id(1) == 0)
    def _():
        carry_ref[...] = jnp.zeros_like(carry_ref)
        state_ref[...] = jnp.zeros_like(state_ref)

    for grp, dst in enumerate((q_s, k_s, v_s)):
        cols = slice(grp * HD, (grp + 1) * HD)
        x = qkv_ref[:, cols].astype(F32)
        xf = jnp.concatenate([carry_ref[:, cols], x], axis=0)
        y = x * cw_ref[GDN_CONV - 1:GDN_CONV, cols]
        for j in range(1, GDN_CONV):
            y = y + pltpu.roll(xf, j, 0)[8:] * cw_ref[GDN_CONV - 1 - j:GDN_CONV - j, cols]
        y = y * _sigmoid(y)
        if grp == 2:
            dst[...] = y
        else:
            for h in range(GDN_HEADS):
                yh = y[:, h * GDN_DK:(h + 1) * GDN_DK]
                yh = yh * lax.rsqrt(jnp.sum(yh * yh, axis=-1, keepdims=True) + L2_EPS)
                if grp == 0:
                    yh = yh * (GDN_DK ** -0.5)
                dst[:, h * GDN_DK:(h + 1) * GDN_DK] = yh
    carry_ref[...] = qkv_ref[tc - 16:tc, :].astype(F32)[8:]

    ri = lax.broadcasted_iota(jnp.int32, (C, C), 0)
    ci = lax.broadcasted_iota(jnp.int32, (C, C), 1)
    incl = ri >= ci
    strict = ri > ci
    a_low = jnp.where(incl, 1.0, 0.0).astype(BF16)
    eye_cat = jnp.concatenate([jnp.where(ri == ci, 1.0, 0.0).astype(F32)] * H, axis=1)
    rb = lax.broadcasted_iota(jnp.int32, (H * C, H * C), 0) // C
    cb = lax.broadcasted_iota(jnp.int32, (H * C, H * C), 1) // C
    diag_mask = jnp.where(rb == cb, 1.0, 0.0).astype(BF16)
    neg_exp_alog = -jnp.exp(alog_ref[...])
    dtb = dtb_ref[...]
    nw = nw_ref[...]
    n_chunks = tc // C
    heads = range(H)
    hsl = [slice(h * GDN_DK, (h + 1) * GDN_DK) for h in heads]

    chunks = range(n_chunks)
    ch = [(c, h) for c in chunks for h in heads]
    rows = [slice(c * C, (c + 1) * C) for c in chunks]
    ab = [ab_ref[rows[c], :] for c in chunks]
    gv = [neg_exp_alog * _softplus(ab[c] + dtb) for c in chunks]
    bv = [_sigmoid(ab[c]) for c in chunks]
    gc_all = [_dot_exact_lhs(a_low, gv[c]) for c in chunks]
    gc_t = [gc_all[c].T for c in chunks]
    q = {(c, h): q_s[rows[c], hsl[h]] for c, h in ch}
    k = {(c, h): k_s[rows[c], hsl[h]] for c, h in ch}
    v = {(c, h): v_s[rows[c], hsl[h]] for c, h in ch}
    gc = {(c, h): jnp.broadcast_to(gc_all[c][:, h:h + 1], (C, GDN_DK)) for c, h in ch}
    beta = {(c, h): jnp.broadcast_to(bv[c][:, H + h:H + h + 1], (C, GDN_DK)) for c, h in ch}
    decay = {(c, h): jnp.exp(jnp.where(incl, gc[c, h][:, :C] - gc_t[c][h:h + 1, :], -jnp.inf)) for c, h in ch}
    egc = {x: jnp.exp(gc[x]) for x in ch}
    gl = {x: gc[x][C - 1:C, :] for x in ch}
    kb = {x: k[x] * beta[x] for x in ch}
    kbf = {x: k[x].astype(BF16) for x in ch}
    kk = {x: _dot_nt(kb[x].astype(BF16), kbf[x]) for x in ch}
    low_cat = [jnp.concatenate([jnp.where(strict, kk[c, h] * decay[c, h], 0.0) for h in heads], axis=1)
               for c in chunks]
    t_cat = [t.astype(BF16) for t in _unit_lower_inverse(low_cat, eye_cat, diag_mask)]
    uw = {(c, h): _dot(t_cat[c][:, h * C:(h + 1) * C],
                       jnp.concatenate([v[c, h] * beta[c, h], kb[c, h] * egc[c, h]], axis=1).astype(BF16))
          for c, h in ch}
    attn = {x: (_dot_nt(q[x].astype(BF16), kbf[x]) * decay[x]).astype(BF16) for x in ch}
    q_dec = {x: (q[x] * egc[x]).astype(BF16) for x in ch}
    k_dec_t = {x: (k[x] * jnp.exp(gl[x] - gc[x])).T.astype(BF16) for x in ch}
    egl = {x: jnp.exp(gl[x]) for x in ch}

    st = [state_ref[h] for h in heads]
    gate = [[z_ref[c * C:(c + 1) * C, hsl[h]].astype(F32) for h in heads] for c in range(n_chunks)]
    outs = []
    for c in range(n_chunks):
        for h in heads:
            stb = st[h].astype(BF16)
            v_new = uw[c, h][:, :GDN_DV] - _dot(uw[c, h][:, GDN_DV:].astype(BF16), stb)
            vnb = v_new.astype(BF16)
            o = _dot(q_dec[c, h], stb) + _dot(attn[c, h], vnb)
            st[h] = st[h] * egl[c, h] + _dot(k_dec_t[c, h], vnb)
            zz = gate[c][h]
            outs.append(_rms(o, nw) * (zz * _sigmoid(zz)))
    for c in range(n_chunks):
        for h in heads:
            o_ref[c * C:(c + 1) * C, hsl[h]] = outs[c * H + h].astype(o_ref.dtype)
    for h in heads:
        state_ref[h] = st[h]


def _gdn(proj, ab, conv_w, alog_v, dtb_v, norm_w, batch, seq, tc):
    T = proj.shape[0]
    n_s = seq // tc
    HD = GDN_HEADS * GDN_DK
    return pl.pallas_call(
        _gdn_kernel,
        grid=(batch, n_s),
        in_specs=[pl.BlockSpec((tc, 3 * HD), lambda b, s: (b * n_s + s, PM_A_QKV // (3 * HD))),
                  pl.BlockSpec((tc, HD), lambda b, s: (b * n_s + s, PM_A_Z // HD)),
                  pl.BlockSpec((tc, LANES), lambda b, s: (b * n_s + s, 0)),
                  pl.BlockSpec((8, 3 * HD), lambda b, s: (0, 0)),
                  pl.BlockSpec((1, LANES), lambda b, s: (0, 0)),
                  pl.BlockSpec((1, LANES), lambda b, s: (0, 0)),
                  pl.BlockSpec((1, GDN_DV), lambda b, s: (0, 0))],
        out_specs=pl.BlockSpec((tc, HD), lambda b, s: (b * n_s + s, 0)),
        out_shape=jax.ShapeDtypeStruct((T, HD), BF16),
        scratch_shapes=[pltpu.VMEM((8, 3 * HD), F32),
                        pltpu.VMEM((tc, HD), F32), pltpu.VMEM((tc, HD), F32), pltpu.VMEM((tc, HD), F32),
                        pltpu.VMEM((GDN_HEADS, GDN_DK, GDN_DV), F32)],
        compiler_params=_cparams("parallel", "arbitrary"),
    )(proj, proj, ab, conv_w, alog_v, dtb_v, norm_w)


def _dswa_kernel(cur_ref, prev_ref, o_ref, lse_ref):
    BL = DSWA_BLOCK
    n_blk = cur_ref.shape[1] // BL
    n = pl.program_id(2)
    cur = cur_ref[0]
    prev = prev_ref[0]
    ii = lax.broadcasted_iota(jnp.int32, (BL, 2 * BL), 0)
    jj = lax.broadcasted_iota(jnp.int32, (BL, 2 * BL), 1)
    band = (jj >= ii) & (jj <= ii + BL)
    first_valid = jnp.where(n > 0, 0, BL)
    band_first = band & (jj >= first_valid)
    kw = DSWA_HPG * DSWA_DH
    for h in range(DSWA_HPG):
        hs = slice(h * DSWA_DH, (h + 1) * DSWA_DH)
        ks = slice(kw + h * DSWA_DH, kw + (h + 1) * DSWA_DH)
        vs = slice(2 * kw + h * DSWA_DH, 2 * kw + (h + 1) * DSWA_DH)
        k_all = jnp.concatenate([prev[:, ks], cur[:, ks]], axis=0).astype(BF16)
        v_all = jnp.concatenate([prev[:, vs], cur[:, vs]], axis=0).astype(BF16)
        for m in range(n_blk):
            rows = slice(m * BL, (m + 1) * BL)
            q = cur[rows, hs].astype(BF16)
            s = jnp.where(band_first if m == 0 else band, _dot_nt(q, k_all[m * BL:(m + 2) * BL]), -jnp.inf)
            mx = jnp.max(s, axis=-1, keepdims=True)
            p = jnp.exp(s - mx)
            l = jnp.sum(p, axis=-1, keepdims=True)
            o_ref[0, rows, hs] = _dot(p.astype(BF16), v_all[m * BL:(m + 2) * BL]) * (1.0 / l)
            lse_ref[0, rows, hs] = jnp.broadcast_to(mx + jnp.log(l), (BL, DSWA_DH))
    o_ref[0, :, kw:] = jnp.zeros((n_blk * BL, DSWA_OW - kw), F32)
    lse_ref[0, :, kw:] = jnp.zeros((n_blk * BL, DSWA_OW - kw), F32)


def _dswa(grp, batch, seq, dil):
    T = grp.shape[0]
    L = seq // dil
    nb = L // DSWA_BLOCK
    n_blk = min(DSWA_BLOCKS_PER_STEP, nb)
    g3 = grp.reshape(batch, L, dil * DSWA_GW)
    out = jax.ShapeDtypeStruct((batch, L, dil * DSWA_OW), F32)
    o, lse = pl.pallas_call(
        _dswa_kernel,
        grid=(batch, dil, nb // n_blk),
        in_specs=[pl.BlockSpec((1, n_blk * DSWA_BLOCK, DSWA_GW), lambda b, r, n: (b, n, r)),
                  pl.BlockSpec((1, DSWA_BLOCK, DSWA_GW), lambda b, r, n: (b, jnp.maximum(n * n_blk - 1, 0), r))],
        out_specs=[pl.BlockSpec((1, n_blk * DSWA_BLOCK, DSWA_OW), lambda b, r, n: (b, n, r))] * 2,
        out_shape=[out, out],
        compiler_params=_cparams("parallel", "parallel", "arbitrary"),
    )(g3, g3)
    return o.reshape(T, DSWA_OW), lse.reshape(T, DSWA_OW)


def _split2(x):
    hi = x.astype(BF16)
    lo = (x - hi.astype(F32)).astype(BF16)
    return hi, lo


def _sb_kernel(q_ref, k_ref, v_ref, o_ref, vt_s, acc_s):
    BLK, DH = SB_BLOCK, SB_DH
    W = 2 * BLK
    G = q_ref.shape[0] // BLK
    n_kblocks = k_ref.shape[0] // BLK
    step_id = pl.program_id(2)
    i0 = step_id * G

    @pl.when(step_id == 0)
    def _():
        for j in range(n_kblocks):
            vt_s[j] = v_ref[j * BLK:(j + 1) * BLK, :].astype(F32).T.astype(BF16)

    ri = lax.broadcasted_iota(jnp.int32, (BLK, W), 0)
    ci = lax.broadcasted_iota(jnp.int32, (BLK, W), 1)
    causal = ri < (ci & (BLK - 1))
    own_head = (ri < DH) == (ci < BLK)
    r2 = lax.broadcasted_iota(jnp.int32, (BLK, BLK), 0)
    c2 = lax.broadcasted_iota(jnp.int32, (BLK, BLK), 1)
    after = jnp.where(c2 > r2, 1.0, 0.0).astype(BF16)

    q_bd = []
    for g in range(G):
        qt = q_ref[g * BLK:(g + 1) * BLK, :].astype(F32).T
        q_bd.append(jnp.where(own_head, jnp.concatenate([qt, qt], axis=1), 0.0).astype(BF16))

    def visit(js, c_rows, mask):
        ks = [k_ref[pl.ds(pl.multiple_of(js[g] * BLK, BLK), BLK), :].astype(BF16) for g in range(G)]
        vts = [vt_s[js[g]] for g in range(G)]
        zs = [_dot(ks[g], q_bd[g]) for g in range(G)]
        sps = [jnp.maximum(z, 0.0) + jnp.log(1.0 + jnp.exp(-jnp.abs(z))) for z in zs]
        sp_ms = sps if mask is None else [jnp.where(mask, sp, 0.0) for sp in sps]
        splits = [_split2(sp_m) for sp_m in sp_ms]
        sufs = [_dot(after, hi) + _dot(after, lo) for hi, lo in splits]
        a_s = [jnp.exp(zs[g] - sps[g] - sufs[g] + c_rows[g]) for g in range(G)]
        if mask is not None:
            a_s = [jnp.where(mask, a, 0.0) for a in a_s]
        pvs = [_dot(vts[g], a_s[g].astype(BF16)) for g in range(G)]
        return pvs, [c_rows[g] - (sufs[g][0:1, :] + sp_ms[g][0:1, :]) for g in range(G)]

    pvs, cs = visit([i0 + g for g in range(G)], [jnp.zeros((1, W), F32)] * G, causal)
    for g in range(G):
        acc_s[g] = pvs[g]

    def cond(state):
        d, live, _ = state
        return (d < i0 + G) & live

    def body(state):
        d, _, cs = state
        js = [i0 + g - d for g in range(G)]
        c_in = [jnp.where(js[g] < 0, SB_LOG_DEAD, cs[g]) for g in range(G)]
        accs = [acc_s[g] for g in range(G)]
        pvs, new = visit([jnp.maximum(j, 0) for j in js], c_in, None)
        for g in range(G):
            acc_s[g] = accs[g] + pvs[g]
        c_max = jnp.max(functools.reduce(jnp.maximum, new))
        return d + 1, c_max > SB_LOG_ZERO, tuple(new)

    lax.while_loop(cond, body, (jnp.int32(1), jnp.bool_(True), tuple(cs)))
    for g in range(G):
        acc = acc_s[g]
        o_ref[g * BLK:(g + 1) * BLK, :] = jnp.concatenate([acc[:DH, :BLK], acc[DH:, BLK:]],
                                                          axis=0).T.astype(o_ref.dtype)


def _sb(proj, batch, seq):
    T = proj.shape[0]
    nq = seq // SB_BLOCK
    G = min(SB_QBLOCKS_PER_STEP, nq)
    n_steps = nq // G
    pairs = SB_HEADS // 2
    qo, ko, vo = PM_C // LANES, (PM_C + SB_HEADS * SB_DH) // LANES, (PM_C + 2 * SB_HEADS * SB_DH) // LANES
    return pl.pallas_call(
        _sb_kernel,
        grid=(batch, pairs, n_steps),
        in_specs=[pl.BlockSpec((G * SB_BLOCK, LANES), lambda b, p, i: (b * n_steps + i, qo + p)),
                  pl.BlockSpec((seq, LANES), lambda b, p, i: (b, ko + p)),
                  pl.BlockSpec((seq, LANES), lambda b, p, i: (b, vo + p))],
        out_specs=pl.BlockSpec((G * SB_BLOCK, LANES), lambda b, p, i: (b * n_steps + i, p)),
        out_shape=jax.ShapeDtypeStruct((T, SB_HEADS * SB_DH), BF16),
        scratch_shapes=[pltpu.VMEM((nq, LANES, SB_BLOCK), BF16),
                        pltpu.VMEM((G, LANES, 2 * SB_BLOCK), F32)],
        compiler_params=_cparams("parallel", "parallel", "arbitrary"),
    )(proj, proj, proj)


def _ret_kernel(qk_ref, v_ref, g_ref, cos_ref, sin_ref, nw_ref, o_ref, r_ref):
    tc = qk_ref.shape[0]
    C = RET_CHUNK
    half = RET_DK // 2

    @pl.when(pl.program_id(1) == 0)
    def _():
        r_ref[...] = jnp.zeros_like(r_ref)

    lane = lax.broadcasted_iota(jnp.int32, (C, LANES), 1)
    first = (lane & (RET_DK - 1)) < half
    ri = lax.broadcasted_iota(jnp.int32, (C, C), 0)
    ci = lax.broadcasted_iota(jnp.int32, (C, C), 1)
    diff = (ri - ci).astype(F32)
    causal = ri >= ci
    rowf = lax.broadcasted_iota(jnp.int32, (C, LANES), 0).astype(F32)
    nw = nw_ref[...]

    def chunk(c, _):
        rows = pl.ds(pl.multiple_of(c * C, C), C)
        cos = cos_ref[rows, :]
        sin = sin_ref[rows, :]

        def rope(x):
            sw = jnp.where(first, pltpu.roll(x, LANES - half, 1), pltpu.roll(x, half, 1))
            return x * cos + sw * sin

        for pair in range(RET_HEADS // 2):
            qp = rope(qk_ref[rows, pair * LANES:(pair + 1) * LANES].astype(F32))
            kp = rope(qk_ref[rows, RET_HEADS * RET_DK + pair * LANES:
                             RET_HEADS * RET_DK + (pair + 1) * LANES].astype(F32))
            for hh in range(2):
                h = 2 * pair + hh
                lg = _RET_LOG_GAMMA[h]
                hs = slice(h * RET_DV, (h + 1) * RET_DV)
                q = qp[:, hh * RET_DK:(hh + 1) * RET_DK]
                k = kp[:, hh * RET_DK:(hh + 1) * RET_DK]
                vb = v_ref[rows, hs].astype(BF16)
                qb = q.astype(BF16)
                dmat = jnp.where(causal, jnp.exp(diff * lg), 0.0)
                intra = _dot_nt(qb, k.astype(BF16)) * dmat
                xi = jnp.exp((rowf + 1.0) * lg)
                zeta = jnp.exp((C - 1.0 - rowf[:, :RET_DK]) * lg)
                r = r_ref[h]
                o = _dot(intra.astype(BF16), vb) + _dot(qb, r.astype(BF16)) * xi
                r_ref[h] = r * math.exp(C * lg) + _dot((k * zeta).T.astype(BF16), vb)
                gg = g_ref[rows, hs].astype(F32)
                o_ref[rows, hs] = (_rms(o, nw) * (gg * _sigmoid(gg))).astype(o_ref.dtype)
        return 0

    lax.fori_loop(0, tc // C, chunk, 0)


def _ret(proj, cos_t, sin_t, norm_w, batch, seq, tc):
    T = proj.shape[0]
    n_s = seq // tc
    W = RET_HEADS * RET_DV
    return pl.pallas_call(
        _ret_kernel,
        grid=(batch, n_s),
        in_specs=[pl.BlockSpec((tc, W), lambda b, s: (b * n_s + s, PM_D_QK // W)),
                  pl.BlockSpec((tc, W), lambda b, s: (b * n_s + s, PM_D_V // W)),
                  pl.BlockSpec((tc, W), lambda b, s: (b * n_s + s, PM_D_G // W)),
                  pl.BlockSpec((tc, LANES), lambda b, s: (s, 0)),
                  pl.BlockSpec((tc, LANES), lambda b, s: (s, 0)),
                  pl.BlockSpec((1, RET_DV), lambda b, s: (0, 0))],
        out_specs=pl.BlockSpec((tc, W), lambda b, s: (b * n_s + s, 0)),
        out_shape=jax.ShapeDtypeStruct((T, W), BF16),
        scratch_shapes=[pltpu.VMEM((RET_HEADS, RET_DK, RET_DV), F32)],
        compiler_params=_cparams("parallel", "arbitrary"),
    )(proj, proj, proj, cos_t, sin_t, norm_w)


def _merge_kernel(x_ref, oa_ref, ob0_ref, ob1_ref, ob2_ref, l0_ref, l1_ref, l2_ref, oc_ref, od_ref,
                  ga_ref, gb_ref, gc_ref, gd_ref, wa_ref, wb_ref, wc_ref, wd_ref, wo_ref, nw_ref, out_ref):
    l0, l1, l2 = l0_ref[...], l1_ref[...], l2_ref[...]
    m = jnp.maximum(jnp.maximum(l0, l1), l2)
    e0, e1, e2 = jnp.exp(l0 - m), jnp.exp(l1 - m), jnp.exp(l2 - m)
    inv = 1.0 / (e0 + e1 + e2)
    ob = jnp.concatenate([ob0_ref[...] * (e0 * inv), ob1_ref[...] * (e1 * inv), ob2_ref[...] * (e2 * inv)],
                         axis=1).astype(BF16)
    y = _sigmoid(ga_ref[...].astype(F32)) * _dot(oa_ref[...], wa_ref[...])
    y = y + _sigmoid(gb_ref[...].astype(F32)) * _dot(ob, wb_ref[...])
    y = y + _sigmoid(gc_ref[...].astype(F32)) * _dot(oc_ref[...], wc_ref[...])
    y = y + _sigmoid(gd_ref[...].astype(F32)) * _dot(od_ref[...], wd_ref[...])
    mixed = _dot(y.astype(BF16), wo_ref[...])
    out_ref[...] = x_ref[...] + _rms(mixed, nw_ref[...])


def _merge(x, oa, obs, lses, oc, od, proj, wa, wb, wc, wd, wo, nw, tm):
    T, D = x.shape
    row = lambda w: pl.BlockSpec((tm, w), lambda i: (i, 0))
    gate = lambda br: pl.BlockSpec((tm, D), lambda i: (i, PM_GATES // D + br))
    full = lambda a: pl.BlockSpec(a.shape, lambda i: (0, 0))
    return pl.pallas_call(
        _merge_kernel,
        grid=(T // tm,),
        in_specs=[row(D), row(oa.shape[1])] + [row(DSWA_OW)] * 6 + [row(oc.shape[1]), row(od.shape[1])]
                 + [gate(0), gate(1), gate(2), gate(3)] + [full(wa), full(wb), full(wc), full(wd), full(wo), full(nw)],
        out_specs=row(D),
        out_shape=jax.ShapeDtypeStruct((T, D), F32),
        compiler_params=_cparams("parallel"),
    )(x, oa, *obs, *lses, oc, od, proj, proj, proj, proj, wa, wb, wc, wd, wo, nw)


def _mlp_kernel(x_ref, n1_ref, w1_ref, w2_ref, n2_ref, out_ref, h_ref, acc_ref):
    f = pl.program_id(1)

    @pl.when(f == 0)
    def _():
        h_ref[...] = _rms(x_ref[...], n1_ref[...]).astype(BF16)
        acc_ref[...] = jnp.zeros_like(acc_ref)

    hid = jnp.maximum(_dot(h_ref[...], w1_ref[...]), 0.0)
    acc_ref[...] += _dot((hid * hid).astype(BF16), w2_ref[...])

    @pl.when(f == pl.num_programs(1) - 1)
    def _():
        out_ref[...] = x_ref[...] + _rms(acc_ref[...], n2_ref[...])


def _mlp(x, n1, w1, w2, n2, tm, tf):
    T, D = x.shape
    F = w1.shape[1]
    return pl.pallas_call(
        _mlp_kernel,
        grid=(T // tm, F // tf),
        in_specs=[pl.BlockSpec((tm, D), lambda i, f: (i, 0)),
                  pl.BlockSpec((1, D), lambda i, f: (0, 0)),
                  pl.BlockSpec((D, tf), lambda i, f: (0, f)),
                  pl.BlockSpec((tf, D), lambda i, f: (f, 0)),
                  pl.BlockSpec((1, D), lambda i, f: (0, 0))],
        out_specs=pl.BlockSpec((tm, D), lambda i, f: (i, 0)),
        out_shape=jax.ShapeDtypeStruct((T, D), F32),
        scratch_shapes=[pltpu.VMEM((tm, D), BF16), pltpu.VMEM((tm, D), F32)],
        compiler_params=_cparams("parallel", "arbitrary"),
    )(x, n1, w1, w2, n2)


def _prep_w_in(w_in):
    sec = [w_in[:, _COL_OFF[i]:_COL_OFF[i + 1]] for i in range(len(_COL_SIZES))]
    a_qkv, a_z, a_a, a_b, b_qkv, c_qkv, d_qk, d_v, d_g, gates = sec
    sbw = SB_HEADS * SB_DH
    c_qkv = jnp.concatenate([c_qkv[:, :sbw] * SB_DH ** -0.5, c_qkv[:, sbw:]], axis=1)
    rw = RET_HEADS * RET_DK
    d_qk = jnp.concatenate([d_qk[:, :rw], d_qk[:, rw:] * RET_DK ** -0.5], axis=1)
    w_main = jnp.concatenate([a_qkv, a_z, c_qkv, d_qk, d_v, d_g, gates], axis=1).astype(BF16)
    bw = DSWA_HEADS * DSWA_DH
    gw = DSWA_HPG * DSWA_DH
    K = w_in.shape[0]
    groups = []
    for g in range(len(DSWA_GROUPS)):
        q = b_qkv[:, g * gw:(g + 1) * gw] * DSWA_DH ** -0.5
        k = b_qkv[:, bw + g * gw: bw + (g + 1) * gw]
        v = b_qkv[:, 2 * bw + g * gw: 2 * bw + (g + 1) * gw]
        groups += [q, k, v, jnp.zeros((K, DSWA_GW - 3 * gw), F32)]
    w_b = jnp.concatenate(groups + [a_a, a_b, jnp.zeros((K, LANES - 2 * GDN_HEADS), F32)], axis=1).astype(BF16)
    return w_main, w_b


def _rope_tables(seq):
    pos = jnp.arange(seq, dtype=jnp.int32).astype(F32)[:, None]
    inv_b = DSWA_ROPE_THETA ** (-jnp.arange(0, DSWA_ROPE_DIM, 2, dtype=F32) / DSWA_ROPE_DIM)
    ang = pos * inv_b[None, :]
    cb, sb = jnp.cos(ang), jnp.sin(ang)
    rest = DSWA_DH - DSWA_ROPE_DIM
    cos_b = jnp.tile(jnp.concatenate([cb, cb, jnp.ones((seq, rest), F32)], axis=1), (1, LANES // DSWA_DH))
    sin_b = jnp.tile(jnp.concatenate([-sb, sb, jnp.zeros((seq, rest), F32)], axis=1), (1, LANES // DSWA_DH))
    inv_r = RET_THETA ** (-jnp.linspace(0.0, 1.0, RET_DK // 2, dtype=F32))
    ang = pos * inv_r[None, :]
    cr, sr = jnp.cos(ang), jnp.sin(ang)
    cos_r = jnp.tile(jnp.concatenate([cr, cr], axis=1), (1, LANES // RET_DK))
    sin_r = jnp.tile(jnp.concatenate([-sr, sr], axis=1), (1, LANES // RET_DK))
    return cos_b, sin_b, cos_r, sin_r


def _lane_vec(v):
    return jnp.zeros((1, LANES), F32).at[0, :v.shape[0]].set(v.astype(F32))


def _layer(x, batch, seq, tabs, n_pre_mix, n_post_mix, n_pre_mlp, n_post_mlp, w_in, conv_w, a_log, dt_bias,
           gdn_norm, ret_norm, w_br_a, w_br_b, w_br_c, w_br_d, w_o, w_mlp_in, w_mlp_out):
    T = x.shape[0]
    cos_b, sin_b, cos_r, sin_r = tabs
    w_main, w_b = _prep_w_in(w_in)
    gain = n_pre_mix.reshape(1, -1)
    proj = _norm_matmul(x, gain, w_main, tm=min(1024, T), tn=512)
    g0, g1, g2, ab = _dswa_proj(x, gain, w_b, cos_b, sin_b, seq, tm=min(512, seq))

    cw = jnp.zeros((8, conv_w.shape[1]), F32).at[:GDN_CONV].set(conv_w)
    oa = _gdn(proj, ab, cw, _lane_vec(a_log), _lane_vec(dt_bias), gdn_norm.reshape(1, -1), batch, seq,
              tc=min(256, seq))
    obs, lses = [], []
    for grp, (_, dil) in zip((g0, g1, g2), DSWA_GROUPS):
        o, lse = _dswa(grp, batch, seq, dil)
        obs.append(o)
        lses.append(lse)
    oc = _sb(proj, batch, seq)
    od = _ret(proj, cos_r, sin_r, ret_norm.reshape(1, -1), batch, seq, tc=min(512, seq))

    gw = DSWA_HPG * DSWA_DH
    wb = jnp.concatenate(
        [jnp.concatenate([w_br_b[g * gw:(g + 1) * gw], jnp.zeros((DSWA_OW - gw, D_MODEL), F32)], axis=0)
         for g in range(len(DSWA_GROUPS))], axis=0).astype(BF16)
    x1 = _merge(x, oa, obs, lses, oc, od, proj, w_br_a.astype(BF16), wb, w_br_c.astype(BF16),
                w_br_d.astype(BF16), w_o.astype(BF16), n_post_mix.reshape(1, -1), tm=min(512, T))
    return _mlp(x1, n_pre_mlp.reshape(1, -1), w_mlp_in.astype(BF16), w_mlp_out.astype(BF16),
                n_post_mlp.reshape(1, -1), tm=min(1024, T), tf=512)


def kernel(x, norm_pre_mix, norm_post_mix, norm_pre_mlp, norm_post_mlp, w_in, conv_w, a_log, dt_bias, gdn_norm,
           ret_norm, w_br_a, w_br_b, w_br_c, w_br_d, w_o, w_mlp_in, w_mlp_out):
    batch, seq, d = x.shape
    tabs = _rope_tables(seq)
    h = x.reshape(batch * seq, d)
    for l in range(norm_pre_mix.shape[0]):
        h = _layer(h, batch, seq, tabs, norm_pre_mix[l], norm_post_mix[l], norm_pre_mlp[l], norm_post_mlp[l],
                   w_in[l], conv_w[l], a_log[l], dt_bias[l], gdn_norm[l], ret_norm[l], w_br_a[l], w_br_b[l],
                   w_br_c[l], w_br_d[l], w_o[l], w_mlp_in[l], w_mlp_out[l])
    return h.reshape(batch, seq, d)
```

```python
import functools
import math

import numpy as np
import jax
import jax.numpy as jnp
from jax import lax
from jax.experimental import pallas as pl
from jax.experimental.pallas import tpu as pltpu

F32 = jnp.float32
BF16 = jnp.bfloat16

D_MODEL = 1024
N_LAYERS = 2
GDN_HEADS, GDN_DK, GDN_DV, GDN_CONV, GDN_CHUNK = 4, 128, 128, 4, 64
DSWA_GROUPS = ((128, 1), (512, 4), (2048, 16))
DSWA_HPG, DSWA_DH, DSWA_BLOCK = 3, 64, 128
DSWA_HEADS = DSWA_HPG * len(DSWA_GROUPS)
DSWA_ROPE_THETA, DSWA_ROPE_DIM = 500000.0, DSWA_DH // 4
SB_HEADS, SB_DH, SB_BLOCK = 8, 64, 128
SB_LOG_ZERO = -110.0
SB_LOG_DEAD = -1e30
SB_QBLOCKS_PER_STEP = 8
DSWA_TOKENS_PER_STEP = 2048
DSWA_CHAIN_GROUP = 12
RET_HEADS, RET_DK, RET_DV, RET_THETA = 4, 64, 128, 10000.0
RET_CHUNK = 128
D_FF = 4 * D_MODEL
NORM_EPS = 1e-6
L2_EPS = 1e-6

_COL_SIZES = (1536, 512, 4, 4, 1728, 1536, 512, 512, 512, 4096)
_COL_OFF = np.concatenate([[0], np.cumsum(_COL_SIZES)])

PM_A_QKV, PM_A_Z, PM_C, PM_D_QK, PM_D_V, PM_D_G, PM_GATES, PM_WIDTH = 0, 1536, 2048, 3584, 4096, 4608, 5120, 9216
DSWA_GW = 640
DSWA_OW = 256

VMEM_LIMIT = 48 * 1024 * 1024
LANES = 128

_RET_LOG_GAMMA = [float(np.log1p(-np.exp2(np.float32(-5.0 - h))).astype(np.float32)) for h in range(RET_HEADS)]


def _cparams(*sem):
    return pltpu.CompilerParams(dimension_semantics=sem, vmem_limit_bytes=VMEM_LIMIT)


def _sigmoid(x):
    return 1.0 / (1.0 + jnp.exp(-x))


def _softplus(x):
    return jnp.maximum(x, 0.0) + jnp.log(1.0 + jnp.exp(-jnp.abs(x)))


def _dot(a, b):
    return jnp.dot(a, b, preferred_element_type=F32)


def _dot_nt(a, b):
    return lax.dot_general(a, b, (((1,), (1,)), ((), ())), preferred_element_type=F32)


def _split3(x):
    hi = x.astype(BF16)
    r = x - hi.astype(F32)
    mid = r.astype(BF16)
    lo = (r - mid.astype(F32)).astype(BF16)
    return hi, mid, lo


def _dot_exact_lhs(a_bf16, x):
    hi, mid, lo = _split3(x)
    return _dot(a_bf16, hi) + (_dot(a_bf16, mid) + _dot(a_bf16, lo))


def _dot_f32(a, b):
    ah, am, al = _split3(a)
    bh, bm, bl = _split3(b)
    return (_dot(ah, bh) + (_dot(ah, bm) + _dot(am, bh))
            + (_dot(am, bm) + _dot(ah, bl) + _dot(al, bh)))


def _rms(x, w):
    ms = jnp.mean(x * x, axis=-1, keepdims=True)
    return x * lax.rsqrt(ms + NORM_EPS) * w


def _norm_matmul_kernel(x_ref, g_ref, w_ref, o_ref, h_ref):
    @pl.when(pl.program_id(1) == 0)
    def _():
        h_ref[...] = _rms(x_ref[...], g_ref[...]).astype(BF16)

    o_ref[...] = _dot(h_ref[...], w_ref[...]).astype(o_ref.dtype)


def _norm_matmul(x, gain, w, tm, tn):
    T, K = x.shape
    N = w.shape[1]
    return pl.pallas_call(
        _norm_matmul_kernel,
        grid=(T // tm, N // tn),
        in_specs=[pl.BlockSpec((tm, K), lambda i, j: (i, 0)),
                  pl.BlockSpec((1, K), lambda i, j: (0, 0)),
                  pl.BlockSpec((K, tn), lambda i, j: (0, j))],
        out_specs=pl.BlockSpec((tm, tn), lambda i, j: (i, j)),
        out_shape=jax.ShapeDtypeStruct((T, N), BF16),
        scratch_shapes=[pltpu.VMEM((tm, K), BF16)],
        compiler_params=_cparams("parallel", "arbitrary"),
    )(x, gain, w)


def _dswa_proj_kernel(x_ref, g_ref, w_ref, cos_ref, sin_ref, o0_ref, o1_ref, o2_ref, oab_ref):
    h = _rms(x_ref[...], g_ref[...]).astype(BF16)
    p = _dot(h, w_ref[...])
    cos = cos_ref[...]
    sin = sin_ref[...]
    lane = lax.broadcasted_iota(jnp.int32, cos.shape, 1)
    first = (lane & (DSWA_DH - 1)) < (DSWA_ROPE_DIM // 2)
    for g, o_ref in enumerate((o0_ref, o1_ref, o2_ref)):
        base = g * DSWA_GW
        for t in range(3):
            xt = p[:, base + t * LANES: base + (t + 1) * LANES]
            sw = jnp.where(first, pltpu.roll(xt, LANES - DSWA_ROPE_DIM // 2, 1),
                           pltpu.roll(xt, DSWA_ROPE_DIM // 2, 1))
            o_ref[t] = xt * cos + sw * sin
        for t in range(3, DSWA_GW // LANES):
            o_ref[t] = p[:, base + t * LANES: base + (t + 1) * LANES]
    oab_ref[...] = p[:, 3 * DSWA_GW:]


def _dswa_proj(x, gain, w, cos_t, sin_t, seq, tm):
    T, K = x.shape
    N = w.shape[1]
    n_s = seq // tm
    grp = jax.ShapeDtypeStruct((DSWA_GW // LANES, T, LANES), F32)
    return pl.pallas_call(
        _dswa_proj_kernel,
        grid=(T // tm,),
        in_specs=[pl.BlockSpec((tm, K), lambda i: (i, 0)),
                  pl.BlockSpec((1, K), lambda i: (0, 0)),
                  pl.BlockSpec((K, N), lambda i: (0, 0)),
                  pl.BlockSpec((tm, LANES), lambda i: (i % n_s, 0)),
                  pl.BlockSpec((tm, LANES), lambda i: (i % n_s, 0))],
        out_specs=[pl.BlockSpec((DSWA_GW // LANES, tm, LANES), lambda i: (0, i, 0))] * 3
                  + [pl.BlockSpec((tm, LANES), lambda i: (i, 0))],
        out_shape=[grp, grp, grp, jax.ShapeDtypeStruct((T, LANES), F32)],
        compiler_params=_cparams("parallel"),
    )(x, gain, w, cos_t, sin_t)


def _per_head_matmul(x_cat, p_cat, diag_mask):
    xh, xl = _split2(x_cat)
    ph, pl_ = [jnp.concatenate([t] * GDN_HEADS, axis=0) * diag_mask for t in _split2(p_cat)]
    return _dot(xh, ph) + (_dot(xh, pl_) + _dot(xl, ph))


def _unit_lower_inverse(lows, eye_cat, diag_mask):
    xs = [eye_cat - low for low in lows]
    ps = [_per_head_matmul(low, low, diag_mask) for low in lows]
    n = 2
    while True:
        xs = [x + _per_head_matmul(x, p, diag_mask) for x, p in zip(xs, ps)]
        n *= 2
        if n >= GDN_CHUNK:
            break
        ps = [_per_head_matmul(p, p, diag_mask) for p in ps]
    return xs


def _gdn_kernel(qkv_ref, z_ref, ab_ref, cw_ref, alog_ref, dtb_ref, nw_ref, o_ref,
                carry_ref, q_s, k_s, v_s, state_ref):
    tc = qkv_ref.shape[0]
    C = GDN_CHUNK
    H = GDN_HEADS
    HD = GDN_HEADS * GDN_DK

    @pl.when(pl.program_id(1) == 0)
    def _():
        carry_ref[...] = jnp.zeros_like(carry_ref)
        state_ref[...] = jnp.zeros_like(state_ref)

    for grp, dst in enumerate((q_s, k_s, v_s)):
        cols = slice(grp * HD, (grp + 1) * HD)
        x = qkv_ref[:, cols].astype(F32)
        xf = jnp.concatenate([carry_ref[:, cols], x], axis=0)
        y = x * cw_ref[GDN_CONV - 1:GDN_CONV, cols]
        for j in range(1, GDN_CONV):
            y = y + pltpu.roll(xf, j, 0)[8:] * cw_ref[GDN_CONV - 1 - j:GDN_CONV - j, cols]
        y = y * _sigmoid(y)
        if grp == 2:
            dst[...] = y
        else:
            for h in range(GDN_HEADS):
                yh = y[:, h * GDN_DK:(h + 1) * GDN_DK]
                yh = yh * lax.rsqrt(jnp.sum(yh * yh, axis=-1, keepdims=True) + L2_EPS)
                if grp == 0:
                    yh = yh * (GDN_DK ** -0.5)
                dst[:, h * GDN_DK:(h + 1) * GDN_DK] = yh
    carry_ref[...] = qkv_ref[tc - 16:tc, :].astype(F32)[8:]

    ri = lax.broadcasted_iota(jnp.int32, (C, C), 0)
    ci = lax.broadcasted_iota(jnp.int32, (C, C), 1)
    incl = ri >= ci
    strict = ri > ci
    a_low = jnp.where(incl, 1.0, 0.0).astype(BF16)
    eye_cat = jnp.concatenate([jnp.where(ri == ci, 1.0, 0.0).astype(F32)] * H, axis=1)
    rb = lax.broadcasted_iota(jnp.int32, (H * C, H * C), 0) // C
    cb = lax.broadcasted_iota(jnp.int32, (H * C, H * C), 1) // C
    diag_mask = jnp.where(rb == cb, 1.0, 0.0).astype(BF16)
    neg_exp_alog = -jnp.exp(alog_ref[...])
    dtb = dtb_ref[...]
    nw = nw_ref[...]
    n_chunks = tc // C
    heads = range(H)
    hsl = [slice(h * GDN_DK, (h + 1) * GDN_DK) for h in heads]

    chunks = range(n_chunks)
    ch = [(c, h) for c in chunks for h in heads]
    rows = [slice(c * C, (c + 1) * C) for c in chunks]
    ab = [ab_ref[rows[c], :] for c in chunks]
    gv = [neg_exp_alog * _softplus(ab[c] + dtb) for c in chunks]
    bv = [_sigmoid(ab[c]) for c in chunks]
    gc_all = [_dot_exact_lhs(a_low, gv[c]) for c in chunks]
    gc_t = [gc_all[c].T for c in chunks]
    q = {(c, h): q_s[rows[c], hsl[h]] for c, h in ch}
    k = {(c, h): k_s[rows[c], hsl[h]] for c, h in ch}
    v = {(c, h): v_s[rows[c], hsl[h]] for c, h in ch}
    gc = {(c, h): jnp.broadcast_to(gc_all[c][:, h:h + 1], (C, GDN_DK)) for c, h in ch}
    beta = {(c, h): jnp.broadcast_to(bv[c][:, H + h:H + h + 1], (C, GDN_DK)) for c, h in ch}
    decay = {(c, h): jnp.exp(jnp.where(incl, gc[c, h][:, :C] - gc_t[c][h:h + 1, :], -jnp.inf)) for c, h in ch}
    egc = {x: jnp.exp(gc[x]) for x in ch}
    gl = {x: gc[x][C - 1:C, :] for x in ch}
    kb = {x: k[x] * beta[x] for x in ch}
    kbf = {x: k[x].astype(BF16) for x in ch}
    kk = {x: _dot_nt(kb[x].astype(BF16), kbf[x]) for x in ch}
    low_cat = [jnp.concatenate([jnp.where(strict, kk[c, h] * decay[c, h], 0.0) for h in heads], axis=1)
               for c in chunks]
    t_cat = [t.astype(BF16) for t in _unit_lower_inverse(low_cat, eye_cat, diag_mask)]
    uw = {(c, h): _dot(t_cat[c][:, h * C:(h + 1) * C],
                       jnp.concatenate([v[c, h] * beta[c, h], kb[c, h] * egc[c, h]], axis=1).astype(BF16))
          for c, h in ch}
    attn = {x: (_dot_nt(q[x].astype(BF16), kbf[x]) * decay[x]).astype(BF16) for x in ch}
    q_dec = {x: (q[x] * egc[x]).astype(BF16) for x in ch}
    k_dec_t = {x: (k[x] * jnp.exp(gl[x] - gc[x])).T.astype(BF16) for x in ch}
    egl = {x: jnp.exp(gl[x]) for x in ch}

    st = [state_ref[h] for h in heads]
    gate = [[z_ref[c * C:(c + 1) * C, hsl[h]].astype(F32) for h in heads] for c in range(n_chunks)]
    outs = []
    for c in range(n_chunks):
        for h in heads:
            stb = st[h].astype(BF16)
            v_new = uw[c, h][:, :GDN_DV] - _dot(uw[c, h][:, GDN_DV:].astype(BF16), stb)
            vnb = v_new.astype(BF16)
            o = _dot(q_dec[c, h], stb) + _dot(attn[c, h], vnb)
            st[h] = st[h] * egl[c, h] + _dot(k_dec_t[c, h], vnb)
            zz = gate[c][h]
            outs.append(_rms(o, nw) * (zz * _sigmoid(zz)))
    for c in range(n_chunks):
        for h in heads:
            o_ref[c * C:(c + 1) * C, hsl[h]] = outs[c * H + h].astype(o_ref.dtype)
    for h in heads:
        state_ref[h] = st[h]


def _gdn(proj, ab, conv_w, alog_v, dtb_v, norm_w, batch, seq, tc):
    T = proj.shape[0]
    n_s = seq // tc
    HD = GDN_HEADS * GDN_DK
    return pl.pallas_call(
        _gdn_kernel,
        grid=(batch, n_s),
        in_specs=[pl.BlockSpec((tc, 3 * HD), lambda b, s: (b * n_s + s, PM_A_QKV // (3 * HD))),
                  pl.BlockSpec((tc, HD), lambda b, s: (b * n_s + s, PM_A_Z // HD)),
                  pl.BlockSpec((tc, LANES), lambda b, s: (b * n_s + s, 0)),
                  pl.BlockSpec((8, 3 * HD), lambda b, s: (0, 0)),
                  pl.BlockSpec((1, LANES), lambda b, s: (0, 0)),
                  pl.BlockSpec((1, LANES), lambda b, s: (0, 0)),
                  pl.BlockSpec((1, GDN_DV), lambda b, s: (0, 0))],
        out_specs=pl.BlockSpec((tc, HD), lambda b, s: (b * n_s + s, 0)),
        out_shape=jax.ShapeDtypeStruct((T, HD), BF16),
        scratch_shapes=[pltpu.VMEM((8, 3 * HD), F32),
                        pltpu.VMEM((tc, HD), F32), pltpu.VMEM((tc, HD), F32), pltpu.VMEM((tc, HD), F32),
                        pltpu.VMEM((GDN_HEADS, GDN_DK, GDN_DV), F32)],
        compiler_params=_cparams("parallel", "arbitrary"),
    )(proj, proj, ab, conv_w, alog_v, dtb_v, norm_w)


def _dswa_kernel(cur_ref, prev_ref, o_ref, lse_ref, *, dil):
    BL = DSWA_BLOCK
    n_blk = cur_ref.shape[1] // (BL * dil)
    n = pl.program_id(1)

    def class_rows(ref, r, blk0, n_blocks):
        start, size = r + blk0 * BL * dil, n_blocks * BL
        rows = pl.ds(start, size, stride=dil) if dil > 1 else slice(start, start + size)
        return rows, jnp.concatenate([ref[t, rows, :] for t in range(ref.shape[0])], axis=1)

    ii = lax.broadcasted_iota(jnp.int32, (BL, 2 * BL), 0)
    jj = lax.broadcasted_iota(jnp.int32, (BL, 2 * BL), 1)
    band = (jj >= ii) & (jj <= ii + BL)
    first_valid = jnp.where(n > 0, 0, BL)
    band_first = band & (jj >= first_valid)
    kw = DSWA_HPG * DSWA_DH
    pad = jnp.zeros((BL, DSWA_OW - kw), F32)
    hsl = [slice(h * DSWA_DH, (h + 1) * DSWA_DH) for h in range(DSWA_HPG)]

    mc = min(n_blk, DSWA_CHAIN_GROUP // DSWA_HPG)
    units = [(r, m0) for r in range(dil) for m0 in range(0, n_blk, mc)]
    per_group = max(1, DSWA_CHAIN_GROUP // (mc * DSWA_HPG))
    for u0 in range(0, len(units), per_group):
        group = units[u0:u0 + per_group]
        rows, cur, k_all, v_all = {}, {}, {}, {}
        for u in group:
            r, m0 = u
            rows[u], cur[u] = class_rows(cur_ref, r, m0, mc)
            before = class_rows(prev_ref, r, 0, 1)[1] if m0 == 0 else class_rows(cur_ref, r, m0 - 1, 1)[1]
            k_all[u] = jnp.concatenate([before[:, kw:2 * kw], cur[u][:, kw:2 * kw]], axis=0).astype(BF16)
            v_all[u] = jnp.concatenate([before[:, 2 * kw:3 * kw], cur[u][:, 2 * kw:3 * kw]], axis=0).astype(BF16)
        chains = [(u, m, h) for u in group for m in range(mc) for h in range(DSWA_HPG)]
        keys = {m: slice(m * BL, (m + 2) * BL) for m in range(mc)}
        q = {(u, m, h): cur[u][m * BL:(m + 1) * BL, hsl[h]].astype(BF16) for u, m, h in chains}
        s = {(u, m, h): jnp.where(band_first if (u[1] == 0 and m == 0) else band,
                                  _dot_nt(q[u, m, h], k_all[u][keys[m], hsl[h]]), -jnp.inf) for u, m, h in chains}
        mx = {x: jnp.max(s[x], axis=-1, keepdims=True) for x in chains}
        p = {x: jnp.exp(s[x] - mx[x]) for x in chains}
        l = {x: jnp.sum(p[x], axis=-1, keepdims=True) for x in chains}
        o = {(u, m, h): _dot(p[u, m, h].astype(BF16), v_all[u][keys[m], hsl[h]]) * (1.0 / l[u, m, h])
             for u, m, h in chains}
        lse = {x: jnp.broadcast_to(mx[x] + jnp.log(l[x]), (BL, DSWA_DH)) for x in chains}
        for u in group:
            o_u = jnp.concatenate([jnp.concatenate([o[u, m, h] for h in range(DSWA_HPG)] + [pad], axis=1)
                                   for m in range(mc)], axis=0)
            lse_u = jnp.concatenate([jnp.concatenate([lse[u, m, h] for h in range(DSWA_HPG)] + [pad], axis=1)
                                     for m in range(mc)], axis=0)
            for t in range(DSWA_OW // LANES):
                o_ref[t, rows[u], :] = o_u[:, t * LANES:(t + 1) * LANES]
                lse_ref[t, rows[u], :] = lse_u[:, t * LANES:(t + 1) * LANES]


def _dswa(grp, batch, seq, dil):
    n_in, T, _ = grp.shape
    n_out = DSWA_OW // LANES
    span = DSWA_BLOCK * dil
    n_blk = max(1, min(DSWA_TOKENS_PER_STEP, seq) // span)
    step = n_blk * span
    n_steps = seq // step
    out = jax.ShapeDtypeStruct((n_out, T, LANES), F32)
    return pl.pallas_call(
        functools.partial(_dswa_kernel, dil=dil),
        grid=(batch, n_steps),
        in_specs=[pl.BlockSpec((n_in, step, LANES), lambda b, n: (0, b * n_steps + n, 0)),
                  pl.BlockSpec((n_in, span, LANES),
                               lambda b, n: (0, b * (seq // span) + jnp.maximum(n * n_blk - 1, 0), 0))],
        out_specs=[pl.BlockSpec((n_out, step, LANES), lambda b, n: (0, b * n_steps + n, 0))] * 2,
        out_shape=[out, out],
        compiler_params=_cparams("parallel", "arbitrary"),
    )(grp, grp)


def _split2(x):
    hi = x.astype(BF16)
    lo = (x - hi.astype(F32)).astype(BF16)
    return hi, lo


def _sb_kernel(q_ref, k_ref, v_ref, o_ref, vt_s, acc_s):
    BLK, DH = SB_BLOCK, SB_DH
    W = 2 * BLK
    G = q_ref.shape[0] // BLK
    n_kblocks = k_ref.shape[0] // BLK
    step_id = pl.program_id(2)
    i0 = step_id * G

    @pl.when(step_id == 0)
    def _():
        for j in range(n_kblocks):
            vt_s[j] = v_ref[j * BLK:(j + 1) * BLK, :].astype(F32).T.astype(BF16)

    ri = lax.broadcasted_iota(jnp.int32, (BLK, W), 0)
    ci = lax.broadcasted_iota(jnp.int32, (BLK, W), 1)
    causal = ri < (ci & (BLK - 1))
    own_head = (ri < DH) == (ci < BLK)
    r2 = lax.broadcasted_iota(jnp.int32, (BLK, BLK), 0)
    c2 = lax.broadcasted_iota(jnp.int32, (BLK, BLK), 1)
    after = jnp.where(c2 > r2, 1.0, 0.0).astype(BF16)

    q_bd = []
    for g in range(G):
        qt = q_ref[g * BLK:(g + 1) * BLK, :].astype(F32).T
        q_bd.append(jnp.where(own_head, jnp.concatenate([qt, qt], axis=1), 0.0).astype(BF16))

    def visit(js, c_rows, mask):
        ks = [k_ref[pl.ds(pl.multiple_of(js[g] * BLK, BLK), BLK), :].astype(BF16) for g in range(G)]
        vts = [vt_s[js[g]] for g in range(G)]
        zs = [_dot(ks[g], q_bd[g]) for g in range(G)]
        sps = [jnp.maximum(z, 0.0) + jnp.log(1.0 + jnp.exp(-jnp.abs(z))) for z in zs]
        sp_ms = sps if mask is None else [jnp.where(mask, sp, 0.0) for sp in sps]
        splits = [_split2(sp_m) for sp_m in sp_ms]
        sufs = [_dot(after, hi) + _dot(after, lo) for hi, lo in splits]
        a_s = [jnp.exp(zs[g] - sps[g] - sufs[g] + c_rows[g]) for g in range(G)]
        if mask is not None:
            a_s = [jnp.where(mask, a, 0.0) for a in a_s]
        pvs = [_dot(vts[g], a_s[g].astype(BF16)) for g in range(G)]
        return pvs, [c_rows[g] - (sufs[g][0:1, :] + sp_ms[g][0:1, :]) for g in range(G)]

    pvs, cs = visit([i0 + g for g in range(G)], [jnp.zeros((1, W), F32)] * G, causal)
    for g in range(G):
        acc_s[g] = pvs[g]

    def cond(state):
        d, live, _ = state
        return (d < i0 + G) & live

    def body(state):
        d, _, cs = state
        js = [i0 + g - d for g in range(G)]
        c_in = [jnp.where(js[g] < 0, SB_LOG_DEAD, cs[g]) for g in range(G)]
        accs = [acc_s[g] for g in range(G)]
        pvs, new = visit([jnp.maximum(j, 0) for j in js], c_in, None)
        for g in range(G):
            acc_s[g] = accs[g] + pvs[g]
        c_max = jnp.max(functools.reduce(jnp.maximum, new))
        return d + 1, c_max > SB_LOG_ZERO, tuple(new)

    lax.while_loop(cond, body, (jnp.int32(1), jnp.bool_(True), tuple(cs)))
    for g in range(G):
        acc = acc_s[g]
        o_ref[g * BLK:(g + 1) * BLK, :] = jnp.concatenate([acc[:DH, :BLK], acc[DH:, BLK:]],
                                                          axis=0).T.astype(o_ref.dtype)


def _sb(proj, batch, seq):
    T = proj.shape[0]
    nq = seq // SB_BLOCK
    G = min(SB_QBLOCKS_PER_STEP, nq)
    n_steps = nq // G
    pairs = SB_HEADS // 2
    qo, ko, vo = PM_C // LANES, (PM_C + SB_HEADS * SB_DH) // LANES, (PM_C + 2 * SB_HEADS * SB_DH) // LANES
    return pl.pallas_call(
        _sb_kernel,
        grid=(batch, pairs, n_steps),
        in_specs=[pl.BlockSpec((G * SB_BLOCK, LANES), lambda b, p, i: (b * n_steps + i, qo + p)),
                  pl.BlockSpec((seq, LANES), lambda b, p, i: (b, ko + p)),
                  pl.BlockSpec((seq, LANES), lambda b, p, i: (b, vo + p))],
        out_specs=pl.BlockSpec((G * SB_BLOCK, LANES), lambda b, p, i: (b * n_steps + i, p)),
        out_shape=jax.ShapeDtypeStruct((T, SB_HEADS * SB_DH), BF16),
        scratch_shapes=[pltpu.VMEM((nq, LANES, SB_BLOCK), BF16),
                        pltpu.VMEM((G, LANES, 2 * SB_BLOCK), F32)],
        compiler_params=_cparams("parallel", "parallel", "arbitrary"),
    )(proj, proj, proj)


def _ret_kernel(qk_ref, v_ref, g_ref, cos_ref, sin_ref, nw_ref, o_ref, r_ref):
    tc = qk_ref.shape[0]
    C = RET_CHUNK
    half = RET_DK // 2

    @pl.when(pl.program_id(1) == 0)
    def _():
        r_ref[...] = jnp.zeros_like(r_ref)

    lane = lax.broadcasted_iota(jnp.int32, (C, LANES), 1)
    first = (lane & (RET_DK - 1)) < half
    ri = lax.broadcasted_iota(jnp.int32, (C, C), 0)
    ci = lax.broadcasted_iota(jnp.int32, (C, C), 1)
    diff = (ri - ci).astype(F32)
    causal = ri >= ci
    rowf = lax.broadcasted_iota(jnp.int32, (C, LANES), 0).astype(F32)
    nw = nw_ref[...]

    heads = range(RET_HEADS)
    chunks = range(tc // C)
    ch = [(c, h) for c in chunks for h in heads]
    rows = [slice(c * C, (c + 1) * C) for c in chunks]
    hsl = [slice(h * RET_DV, (h + 1) * RET_DV) for h in heads]
    dmat = [jnp.where(causal, jnp.exp(diff * _RET_LOG_GAMMA[h]), 0.0) for h in heads]
    xi = [jnp.exp((rowf + 1.0) * _RET_LOG_GAMMA[h]) for h in heads]
    zeta = [jnp.exp((C - 1.0 - rowf[:, :RET_DK]) * _RET_LOG_GAMMA[h]) for h in heads]

    def rope(x, c):
        sw = jnp.where(first, pltpu.roll(x, LANES - half, 1), pltpu.roll(x, half, 1))
        return x * cos_ref[rows[c], :] + sw * sin_ref[rows[c], :]

    kw = RET_HEADS * RET_DK
    qp = {(c, p): rope(qk_ref[rows[c], p * LANES:(p + 1) * LANES].astype(F32), c)
          for c in chunks for p in range(RET_HEADS // 2)}
    kp = {(c, p): rope(qk_ref[rows[c], kw + p * LANES:kw + (p + 1) * LANES].astype(F32), c)
          for c in chunks for p in range(RET_HEADS // 2)}
    q = {(c, h): qp[c, h // 2][:, (h % 2) * RET_DK:(h % 2 + 1) * RET_DK].astype(BF16) for c, h in ch}
    k = {(c, h): kp[c, h // 2][:, (h % 2) * RET_DK:(h % 2 + 1) * RET_DK] for c, h in ch}
    vb = {(c, h): v_ref[rows[c], hsl[h]].astype(BF16) for c, h in ch}
    gate = {(c, h): g_ref[rows[c], hsl[h]].astype(F32) for c, h in ch}
    intra = {(c, h): (_dot_nt(q[c, h], k[c, h].astype(BF16)) * dmat[h]).astype(BF16) for c, h in ch}
    delta = {(c, h): _dot((k[c, h] * zeta[h]).T.astype(BF16), vb[c, h]) for c, h in ch}
    r_in = {}
    for h in heads:
        r = r_ref[h]
        for c in chunks:
            r_in[c, h] = r.astype(BF16)
            r = r * math.exp(C * _RET_LOG_GAMMA[h]) + delta[c, h]
        r_ref[h] = r
    o = {(c, h): _dot(intra[c, h], vb[c, h]) + _dot(q[c, h], r_in[c, h]) * xi[h] for c, h in ch}
    for c, h in ch:
        gg = gate[c, h]
        o_ref[rows[c], hsl[h]] = (_rms(o[c, h], nw) * (gg * _sigmoid(gg))).astype(o_ref.dtype)


def _ret(proj, cos_t, sin_t, norm_w, batch, seq, tc):
    T = proj.shape[0]
    n_s = seq // tc
    W = RET_HEADS * RET_DV
    return pl.pallas_call(
        _ret_kernel,
        grid=(batch, n_s),
        in_specs=[pl.BlockSpec((tc, W), lambda b, s: (b * n_s + s, PM_D_QK // W)),
                  pl.BlockSpec((tc, W), lambda b, s: (b * n_s + s, PM_D_V // W)),
                  pl.BlockSpec((tc, W), lambda b, s: (b * n_s + s, PM_D_G // W)),
                  pl.BlockSpec((tc, LANES), lambda b, s: (s, 0)),
                  pl.BlockSpec((tc, LANES), lambda b, s: (s, 0)),
                  pl.BlockSpec((1, RET_DV), lambda b, s: (0, 0))],
        out_specs=pl.BlockSpec((tc, W), lambda b, s: (b * n_s + s, 0)),
        out_shape=jax.ShapeDtypeStruct((T, W), BF16),
        scratch_shapes=[pltpu.VMEM((RET_HEADS, RET_DK, RET_DV), F32)],
        compiler_params=_cparams("parallel", "arbitrary"),
    )(proj, proj, proj, cos_t, sin_t, norm_w)


def _merge_kernel(x_ref, oa_ref, ob0_ref, ob1_ref, ob2_ref, l0_ref, l1_ref, l2_ref, oc_ref, od_ref,
                  ga_ref, gb_ref, gc_ref, gd_ref, wa_ref, wb_ref, wc_ref, wd_ref, wo_ref, nw_ref, out_ref):
    def slabs(ref):
        return jnp.concatenate([ref[t] for t in range(ref.shape[0])], axis=1)

    l0, l1, l2 = slabs(l0_ref), slabs(l1_ref), slabs(l2_ref)
    m = jnp.maximum(jnp.maximum(l0, l1), l2)
    e0, e1, e2 = jnp.exp(l0 - m), jnp.exp(l1 - m), jnp.exp(l2 - m)
    inv = 1.0 / (e0 + e1 + e2)
    ob = jnp.concatenate([slabs(ob0_ref) * (e0 * inv), slabs(ob1_ref) * (e1 * inv), slabs(ob2_ref) * (e2 * inv)],
                         axis=1).astype(BF16)
    y = _sigmoid(ga_ref[...].astype(F32)) * _dot(oa_ref[...], wa_ref[...])
    y = y + _sigmoid(gb_ref[...].astype(F32)) * _dot(ob, wb_ref[...])
    y = y + _sigmoid(gc_ref[...].astype(F32)) * _dot(oc_ref[...], wc_ref[...])
    y = y + _sigmoid(gd_ref[...].astype(F32)) * _dot(od_ref[...], wd_ref[...])
    mixed = _dot(y.astype(BF16), wo_ref[...])
    out_ref[...] = x_ref[...] + _rms(mixed, nw_ref[...])


def _merge(x, oa, obs, lses, oc, od, proj, wa, wb, wc, wd, wo, nw, tm):
    T, D = x.shape
    row = lambda w: pl.BlockSpec((tm, w), lambda i: (i, 0))
    gate = lambda br: pl.BlockSpec((tm, D), lambda i: (i, PM_GATES // D + br))
    full = lambda a: pl.BlockSpec(a.shape, lambda i: (0, 0))
    return pl.pallas_call(
        _merge_kernel,
        grid=(T // tm,),
        in_specs=[row(D), row(oa.shape[1])]
                 + [pl.BlockSpec((DSWA_OW // LANES, tm, LANES), lambda i: (0, i, 0))] * 6
                 + [row(oc.shape[1]), row(od.shape[1])]
                 + [gate(0), gate(1), gate(2), gate(3)] + [full(wa), full(wb), full(wc), full(wd), full(wo), full(nw)],
        out_specs=row(D),
        out_shape=jax.ShapeDtypeStruct((T, D), F32),
        compiler_params=_cparams("parallel"),
    )(x, oa, *obs, *lses, oc, od, proj, proj, proj, proj, wa, wb, wc, wd, wo, nw)


def _mlp_kernel(x_ref, n1_ref, w1_ref, w2_ref, n2_ref, out_ref, h_ref, acc_ref):
    f = pl.program_id(1)

    @pl.when(f == 0)
    def _():
        h_ref[...] = _rms(x_ref[...], n1_ref[...]).astype(BF16)
        acc_ref[...] = jnp.zeros_like(acc_ref)

    hid = jnp.maximum(_dot(h_ref[...], w1_ref[...]), 0.0)
    acc_ref[...] += _dot((hid * hid).astype(BF16), w2_ref[...])

    @pl.when(f == pl.num_programs(1) - 1)
    def _():
        out_ref[...] = x_ref[...] + _rms(acc_ref[...], n2_ref[...])


def _mlp(x, n1, w1, w2, n2, tm, tf):
    T, D = x.shape
    F = w1.shape[1]
    return pl.pallas_call(
        _mlp_kernel,
        grid=(T // tm, F // tf),
        in_specs=[pl.BlockSpec((tm, D), lambda i, f: (i, 0)),
                  pl.BlockSpec((1, D), lambda i, f: (0, 0)),
                  pl.BlockSpec((D, tf), lambda i, f: (0, f)),
                  pl.BlockSpec((tf, D), lambda i, f: (f, 0)),
                  pl.BlockSpec((1, D), lambda i, f: (0, 0))],
        out_specs=pl.BlockSpec((tm, D), lambda i, f: (i, 0)),
        out_shape=jax.ShapeDtypeStruct((T, D), F32),
        scratch_shapes=[pltpu.VMEM((tm, D), BF16), pltpu.VMEM((tm, D), F32)],
        compiler_params=_cparams("parallel", "arbitrary"),
    )(x, n1, w1, w2, n2)


def _prep_w_in(w_in):
    sec = [w_in[:, _COL_OFF[i]:_COL_OFF[i + 1]] for i in range(len(_COL_SIZES))]
    a_qkv, a_z, a_a, a_b, b_qkv, c_qkv, d_qk, d_v, d_g, gates = sec
    sbw = SB_HEADS * SB_DH
    c_qkv = jnp.concatenate([c_qkv[:, :sbw] * SB_DH ** -0.5, c_qkv[:, sbw:]], axis=1)
    rw = RET_HEADS * RET_DK
    d_qk = jnp.concatenate([d_qk[:, :rw], d_qk[:, rw:] * RET_DK ** -0.5], axis=1)
    w_main = jnp.concatenate([a_qkv, a_z, c_qkv, d_qk, d_v, d_g, gates], axis=1).astype(BF16)
    bw = DSWA_HEADS * DSWA_DH
    gw = DSWA_HPG * DSWA_DH
    K = w_in.shape[0]
    groups = []
    for g in range(len(DSWA_GROUPS)):
        q = b_qkv[:, g * gw:(g + 1) * gw] * DSWA_DH ** -0.5
        k = b_qkv[:, bw + g * gw: bw + (g + 1) * gw]
        v = b_qkv[:, 2 * bw + g * gw: 2 * bw + (g + 1) * gw]
        groups += [q, k, v, jnp.zeros((K, DSWA_GW - 3 * gw), F32)]
    w_b = jnp.concatenate(groups + [a_a, a_b, jnp.zeros((K, LANES - 2 * GDN_HEADS), F32)], axis=1).astype(BF16)
    return w_main, w_b


def _rope_tables(seq):
    pos = jnp.arange(seq, dtype=jnp.int32).astype(F32)[:, None]
    inv_b = DSWA_ROPE_THETA ** (-jnp.arange(0, DSWA_ROPE_DIM, 2, dtype=F32) / DSWA_ROPE_DIM)
    ang = pos * inv_b[None, :]
    cb, sb = jnp.cos(ang), jnp.sin(ang)
    rest = DSWA_DH - DSWA_ROPE_DIM
    cos_b = jnp.tile(jnp.concatenate([cb, cb, jnp.ones((seq, rest), F32)], axis=1), (1, LANES // DSWA_DH))
    sin_b = jnp.tile(jnp.concatenate([-sb, sb, jnp.zeros((seq, rest), F32)], axis=1), (1, LANES // DSWA_DH))
    inv_r = RET_THETA ** (-jnp.linspace(0.0, 1.0, RET_DK // 2, dtype=F32))
    ang = pos * inv_r[None, :]
    cr, sr = jnp.cos(ang), jnp.sin(ang)
    cos_r = jnp.tile(jnp.concatenate([cr, cr], axis=1), (1, LANES // RET_DK))
    sin_r = jnp.tile(jnp.concatenate([-sr, sr], axis=1), (1, LANES // RET_DK))
    return cos_b, sin_b, cos_r, sin_r


def _lane_vec(v):
    return jnp.zeros((1, LANES), F32).at[0, :v.shape[0]].set(v.astype(F32))


def _layer(x, batch, seq, tabs, n_pre_mix, n_post_mix, n_pre_mlp, n_post_mlp, w_in, conv_w, a_log, dt_bias,
           gdn_norm, ret_norm, w_br_a, w_br_b, w_br_c, w_br_d, w_o, w_mlp_in, w_mlp_out):
    T = x.shape[0]
    cos_b, sin_b, cos_r, sin_r = tabs
    w_main, w_b = _prep_w_in(w_in)
    gain = n_pre_mix.reshape(1, -1)
    proj = _norm_matmul(x, gain, w_main, tm=min(1024, T), tn=1536)
    g0, g1, g2, ab = _dswa_proj(x, gain, w_b, cos_b, sin_b, seq, tm=min(512, seq))

    cw = jnp.zeros((8, conv_w.shape[1]), F32).at[:GDN_CONV].set(conv_w)
    oa = _gdn(proj, ab, cw, _lane_vec(a_log), _lane_vec(dt_bias), gdn_norm.reshape(1, -1), batch, seq,
              tc=min(256, seq))
    obs, lses = [], []
    for grp, (_, dil) in zip((g0, g1, g2), DSWA_GROUPS):
        o, lse = _dswa(grp, batch, seq, dil)
        obs.append(o)
        lses.append(lse)
    oc = _sb(proj, batch, seq)
    od = _ret(proj, cos_r, sin_r, ret_norm.reshape(1, -1), batch, seq, tc=min(512, seq))

    gw = DSWA_HPG * DSWA_DH
    wb = jnp.concatenate(
        [jnp.concatenate([w_br_b[g * gw:(g + 1) * gw], jnp.zeros((DSWA_OW - gw, D_MODEL), F32)], axis=0)
         for g in range(len(DSWA_GROUPS))], axis=0).astype(BF16)
    x1 = _merge(x, oa, obs, lses, oc, od, proj, w_br_a.astype(BF16), wb, w_br_c.astype(BF16),
                w_br_d.astype(BF16), w_o.astype(BF16), n_post_mix.reshape(1, -1), tm=min(512, T))
    return _mlp(x1, n_pre_mlp.reshape(1, -1), w_mlp_in.astype(BF16), w_mlp_out.astype(BF16),
                n_post_mlp.reshape(1, -1), tm=min(1024, T), tf=512)


def kernel(x, norm_pre_mix, norm_post_mix, norm_pre_mlp, norm_post_mlp, w_in, conv_w, a_log, dt_bias, gdn_norm,
           ret_norm, w_br_a, w_br_b, w_br_c, w_br_d, w_o, w_mlp_in, w_mlp_out):
    batch, seq, d = x.shape
    tabs = _rope_tables(seq)
    h = x.reshape(batch * seq, d)
    for l in range(norm_pre_mix.shape[0]):
        h = _layer(h, batch, seq, tabs, norm_pre_mix[l], norm_post_mix[l], norm_pre_mlp[l], norm_post_mlp[l],
                   w_in[l], conv_w[l], a_log[l], dt_bias[l], gdn_norm[l], ret_norm[l], w_br_a[l], w_br_b[l],
                   w_br_c[l], w_br_d[l], w_o[l], w_mlp_in[l], w_mlp_out[l])
    return h.reshape(batch, seq, d)
```

```python
import functools
import math

import numpy as np
import jax
import jax.numpy as jnp
from jax import lax
from jax.experimental import pallas as pl
from jax.experimental.pallas import tpu as pltpu

F32 = jnp.float32
BF16 = jnp.bfloat16

D_MODEL = 1024
N_LAYERS = 2
GDN_HEADS, GDN_DK, GDN_DV, GDN_CONV, GDN_CHUNK = 4, 128, 128, 4, 64
DSWA_GROUPS = ((128, 1), (512, 4), (2048, 16))
DSWA_HPG, DSWA_DH, DSWA_BLOCK = 3, 64, 128
DSWA_HEADS = DSWA_HPG * len(DSWA_GROUPS)
DSWA_ROPE_THETA, DSWA_ROPE_DIM = 500000.0, DSWA_DH // 4
SB_HEADS, SB_DH, SB_BLOCK = 8, 64, 128
SB_LOG_ZERO = -110.0
SB_LOG_DEAD = -1e30
SB_QBLOCKS_PER_STEP = 8
DSWA_TOKENS_PER_STEP = 2048
DSWA_CHAIN_GROUP = 12
RET_HEADS, RET_DK, RET_DV, RET_THETA = 4, 64, 128, 10000.0
RET_CHUNK = 128
D_FF = 4 * D_MODEL
NORM_EPS = 1e-6
L2_EPS = 1e-6

_COL_SIZES = (1536, 512, 4, 4, 1728, 1536, 512, 512, 512, 4096)
_COL_OFF = np.concatenate([[0], np.cumsum(_COL_SIZES)])

PM_A_QKV, PM_A_Z, PM_C, PM_D_QK, PM_D_V, PM_D_G, PM_GATES, PM_WIDTH = 0, 1536, 2048, 3584, 4096, 4608, 5120, 9216
DSWA_GW = 640
DSWA_OW = 256

VMEM_LIMIT = 48 * 1024 * 1024
LANES = 128

_RET_LOG_GAMMA = [float(np.log1p(-np.exp2(np.float32(-5.0 - h))).astype(np.float32)) for h in range(RET_HEADS)]


def _cparams(*sem):
    return pltpu.CompilerParams(dimension_semantics=sem, vmem_limit_bytes=VMEM_LIMIT)


def _sigmoid(x):
    return 1.0 / (1.0 + jnp.exp(-x))


def _softplus(x):
    return jnp.maximum(x, 0.0) + jnp.log(1.0 + jnp.exp(-jnp.abs(x)))


def _dot(a, b):
    return jnp.dot(a, b, preferred_element_type=F32)


def _dot_nt(a, b):
    return lax.dot_general(a, b, (((1,), (1,)), ((), ())), preferred_element_type=F32)


def _split3(x):
    hi = x.astype(BF16)
    r = x - hi.astype(F32)
    mid = r.astype(BF16)
    lo = (r - mid.astype(F32)).astype(BF16)
    return hi, mid, lo


def _dot_exact_lhs(a_bf16, x):
    hi, mid, lo = _split3(x)
    return _dot(a_bf16, hi) + (_dot(a_bf16, mid) + _dot(a_bf16, lo))


def _dot_f32(a, b):
    ah, am, al = _split3(a)
    bh, bm, bl = _split3(b)
    return (_dot(ah, bh) + (_dot(ah, bm) + _dot(am, bh))
            + (_dot(am, bm) + _dot(ah, bl) + _dot(al, bh)))


def _rms(x, w):
    ms = jnp.mean(x * x, axis=-1, keepdims=True)
    return x * lax.rsqrt(ms + NORM_EPS) * w


def _norm_matmul_kernel(x_ref, g_ref, w_ref, o_ref, h_ref):
    @pl.when(pl.program_id(1) == 0)
    def _():
        h_ref[...] = _rms(x_ref[...], g_ref[...]).astype(BF16)

    o_ref[...] = _dot(h_ref[...], w_ref[...]).astype(o_ref.dtype)


def _norm_matmul(x, gain, w, tm, tn):
    T, K = x.shape
    N = w.shape[1]
    return pl.pallas_call(
        _norm_matmul_kernel,
        grid=(T // tm, N // tn),
        in_specs=[pl.BlockSpec((tm, K), lambda i, j: (i, 0)),
                  pl.BlockSpec((1, K), lambda i, j: (0, 0)),
                  pl.BlockSpec((K, tn), lambda i, j: (0, j))],
        out_specs=pl.BlockSpec((tm, tn), lambda i, j: (i, j)),
        out_shape=jax.ShapeDtypeStruct((T, N), BF16),
        scratch_shapes=[pltpu.VMEM((tm, K), BF16)],
        compiler_params=_cparams("parallel", "arbitrary"),
    )(x, gain, w)


def _dswa_proj_kernel(x_ref, g_ref, w_ref, cos_ref, sin_ref, o0_ref, o1_ref, o2_ref, oab_ref):
    h = _rms(x_ref[...], g_ref[...]).astype(BF16)
    p = _dot(h, w_ref[...])
    cos = cos_ref[...]
    sin = sin_ref[...]
    lane = lax.broadcasted_iota(jnp.int32, cos.shape, 1)
    first = (lane & (DSWA_DH - 1)) < (DSWA_ROPE_DIM // 2)
    for g, o_ref in enumerate((o0_ref, o1_ref, o2_ref)):
        base = g * DSWA_GW
        for t in range(3):
            xt = p[:, base + t * LANES: base + (t + 1) * LANES]
            sw = jnp.where(first, pltpu.roll(xt, LANES - DSWA_ROPE_DIM // 2, 1),
                           pltpu.roll(xt, DSWA_ROPE_DIM // 2, 1))
            o_ref[t] = xt * cos + sw * sin
        for t in range(3, DSWA_GW // LANES):
            o_ref[t] = p[:, base + t * LANES: base + (t + 1) * LANES]
    oab_ref[...] = p[:, 3 * DSWA_GW:]


def _dswa_proj(x, gain, w, cos_t, sin_t, seq, tm):
    T, K = x.shape
    N = w.shape[1]
    n_s = seq // tm
    grp = jax.ShapeDtypeStruct((DSWA_GW // LANES, T, LANES), F32)
    return pl.pallas_call(
        _dswa_proj_kernel,
        grid=(T // tm,),
        in_specs=[pl.BlockSpec((tm, K), lambda i: (i, 0)),
                  pl.BlockSpec((1, K), lambda i: (0, 0)),
                  pl.BlockSpec((K, N), lambda i: (0, 0)),
                  pl.BlockSpec((tm, LANES), lambda i: (i % n_s, 0)),
                  pl.BlockSpec((tm, LANES), lambda i: (i % n_s, 0))],
        out_specs=[pl.BlockSpec((DSWA_GW // LANES, tm, LANES), lambda i: (0, i, 0))] * 3
                  + [pl.BlockSpec((tm, LANES), lambda i: (i, 0))],
        out_shape=[grp, grp, grp, jax.ShapeDtypeStruct((T, LANES), F32)],
        compiler_params=_cparams("parallel"),
    )(x, gain, w, cos_t, sin_t)


def _per_head_matmul(xs, p_cat, diag_mask):
    C = GDN_CHUNK
    ph, pl_ = [jnp.concatenate([t] * GDN_HEADS, axis=0) * diag_mask for t in _split2(p_cat)]
    parts = [_split2(x) for x in xs]
    his = [hi for hi, _ in parts]
    by_hi = _dot(jnp.concatenate(his + [lo for _, lo in parts], axis=0), ph)
    by_lo = _dot(jnp.concatenate(his, axis=0), pl_)
    n = len(xs)
    return [by_hi[i * C:(i + 1) * C] + (by_hi[(n + i) * C:(n + i + 1) * C] + by_lo[i * C:(i + 1) * C])
            for i in range(n)]


def _unit_lower_inverse(lows, eye_cat, diag_mask):
    xs = [eye_cat - low for low in lows]
    ps = [_per_head_matmul([low], low, diag_mask)[0] for low in lows]
    n = 4
    while n < GDN_CHUNK:
        both = [_per_head_matmul([x, p], p, diag_mask) for x, p in zip(xs, ps)]
        xs = [x + xp for x, (xp, _) in zip(xs, both)]
        ps = [pp for _, pp in both]
        n *= 2
    return [x + _per_head_matmul([x], p, diag_mask)[0] for x, p in zip(xs, ps)]


def _gdn_kernel(qkv_ref, z_ref, ab_ref, cw_ref, alog_ref, dtb_ref, nw_ref, o_ref,
                carry_ref, q_s, k_s, v_s, state_ref):
    tc = qkv_ref.shape[0]
    C = GDN_CHUNK
    H = GDN_HEADS
    HD = GDN_HEADS * GDN_DK

    @pl.when(pl.program_id(1) == 0)
    def _():
        carry_ref[0:8, :] = jnp.zeros((8, 3 * HD), F32)
        state_ref[...] = jnp.zeros_like(state_ref)

    carry_ref[8:, :] = qkv_ref[...].astype(F32)
    for grp, dst in enumerate((q_s, k_s, v_s)):
        cols = slice(grp * HD, (grp + 1) * HD)
        y = carry_ref[8:, cols] * cw_ref[GDN_CONV - 1:GDN_CONV, cols]
        for j in range(1, GDN_CONV):
            y = y + carry_ref[8 - j:8 - j + tc, cols] * cw_ref[GDN_CONV - 1 - j:GDN_CONV - j, cols]
        y = y * _sigmoid(y)
        if grp == 2:
            dst[...] = y
        else:
            for h in range(GDN_HEADS):
                yh = y[:, h * GDN_DK:(h + 1) * GDN_DK]
                yh = yh * lax.rsqrt(jnp.sum(yh * yh, axis=-1, keepdims=True) + L2_EPS)
                if grp == 0:
                    yh = yh * (GDN_DK ** -0.5)
                dst[:, h * GDN_DK:(h + 1) * GDN_DK] = yh
    carry_ref[0:8, :] = carry_ref[tc:tc + 8, :]

    ri = lax.broadcasted_iota(jnp.int32, (C, C), 0)
    ci = lax.broadcasted_iota(jnp.int32, (C, C), 1)
    incl = ri >= ci
    strict = ri > ci
    a_low = jnp.where(incl, 1.0, 0.0).astype(BF16)
    eye_cat = jnp.concatenate([jnp.where(ri == ci, 1.0, 0.0).astype(F32)] * H, axis=1)
    rb = lax.broadcasted_iota(jnp.int32, (H * C, H * C), 0) // C
    cb = lax.broadcasted_iota(jnp.int32, (H * C, H * C), 1) // C
    diag_mask = jnp.where(rb == cb, 1.0, 0.0).astype(BF16)
    neg_exp_alog = -jnp.exp(alog_ref[...])
    dtb = dtb_ref[...]
    nw = nw_ref[...]
    n_chunks = tc // C
    heads = range(H)
    hsl = [slice(h * GDN_DK, (h + 1) * GDN_DK) for h in heads]

    chunks = range(n_chunks)
    ch = [(c, h) for c in chunks for h in heads]
    rows = [slice(c * C, (c + 1) * C) for c in chunks]
    ab = [ab_ref[rows[c], :] for c in chunks]
    gv = [neg_exp_alog * _softplus(ab[c] + dtb) for c in chunks]
    bv = [_sigmoid(ab[c]) for c in chunks]
    gc_all = [_dot_exact_lhs(a_low, gv[c]) for c in chunks]
    gc_t = [gc_all[c].T for c in chunks]
    q = {(c, h): q_s[rows[c], hsl[h]] for c, h in ch}
    k = {(c, h): k_s[rows[c], hsl[h]] for c, h in ch}
    v = {(c, h): v_s[rows[c], hsl[h]] for c, h in ch}
    gc = {(c, h): jnp.broadcast_to(gc_all[c][:, h:h + 1], (C, GDN_DK)) for c, h in ch}
    beta = {(c, h): jnp.broadcast_to(bv[c][:, H + h:H + h + 1], (C, GDN_DK)) for c, h in ch}
    decay = {(c, h): jnp.exp(jnp.where(incl, gc[c, h][:, :C] - gc_t[c][h:h + 1, :], -jnp.inf)) for c, h in ch}
    egc = {x: jnp.exp(gc[x]) for x in ch}
    gl = {x: gc[x][C - 1:C, :] for x in ch}
    kb = {x: k[x] * beta[x] for x in ch}
    kbf = {x: k[x].astype(BF16) for x in ch}
    kk = {x: _dot_nt(kb[x].astype(BF16), kbf[x]) for x in ch}
    low_cat = [jnp.concatenate([jnp.where(strict, kk[c, h] * decay[c, h], 0.0) for h in heads], axis=1)
               for c in chunks]
    t_cat = [t.astype(BF16) for t in _unit_lower_inverse(low_cat, eye_cat, diag_mask)]
    uw = {(c, h): _dot(t_cat[c][:, h * C:(h + 1) * C],
                       jnp.concatenate([v[c, h] * beta[c, h], kb[c, h] * egc[c, h]], axis=1).astype(BF16))
          for c, h in ch}
    attn = {x: (_dot_nt(q[x].astype(BF16), kbf[x]) * decay[x]).astype(BF16) for x in ch}
    q_dec = {x: (q[x] * egc[x]).astype(BF16) for x in ch}
    k_dec_t = {x: (k[x] * jnp.exp(gl[x] - gc[x])).T.astype(BF16) for x in ch}
    egl = {x: jnp.exp(gl[x]) for x in ch}

    st = [state_ref[h] for h in heads]
    gate = [[z_ref[c * C:(c + 1) * C, hsl[h]].astype(F32) for h in heads] for c in range(n_chunks)]
    outs = []
    for c in range(n_chunks):
        for h in heads:
            stb = st[h].astype(BF16)
            v_new = uw[c, h][:, :GDN_DV] - _dot(uw[c, h][:, GDN_DV:].astype(BF16), stb)
            vnb = v_new.astype(BF16)
            o = _dot(q_dec[c, h], stb) + _dot(attn[c, h], vnb)
            st[h] = st[h] * egl[c, h] + _dot(k_dec_t[c, h], vnb)
            zz = gate[c][h]
            outs.append(_rms(o, nw) * (zz * _sigmoid(zz)))
    for c in range(n_chunks):
        for h in heads:
            o_ref[c * C:(c + 1) * C, hsl[h]] = outs[c * H + h].astype(o_ref.dtype)
    for h in heads:
        state_ref[h] = st[h]


def _gdn(proj, ab, conv_w, alog_v, dtb_v, norm_w, batch, seq, tc):
    T = proj.shape[0]
    n_s = seq // tc
    HD = GDN_HEADS * GDN_DK
    return pl.pallas_call(
        _gdn_kernel,
        grid=(batch, n_s),
        in_specs=[pl.BlockSpec((tc, 3 * HD), lambda b, s: (b * n_s + s, PM_A_QKV // (3 * HD))),
                  pl.BlockSpec((tc, HD), lambda b, s: (b * n_s + s, PM_A_Z // HD)),
                  pl.BlockSpec((tc, LANES), lambda b, s: (b * n_s + s, 0)),
                  pl.BlockSpec((8, 3 * HD), lambda b, s: (0, 0)),
                  pl.BlockSpec((1, LANES), lambda b, s: (0, 0)),
                  pl.BlockSpec((1, LANES), lambda b, s: (0, 0)),
                  pl.BlockSpec((1, GDN_DV), lambda b, s: (0, 0))],
        out_specs=pl.BlockSpec((tc, HD), lambda b, s: (b * n_s + s, 0)),
        out_shape=jax.ShapeDtypeStruct((T, HD), BF16),
        scratch_shapes=[pltpu.VMEM((tc + 8, 3 * HD), F32),
                        pltpu.VMEM((tc, HD), F32), pltpu.VMEM((tc, HD), F32), pltpu.VMEM((tc, HD), F32),
                        pltpu.VMEM((GDN_HEADS, GDN_DK, GDN_DV), F32)],
        compiler_params=_cparams("parallel", "arbitrary"),
    )(proj, proj, ab, conv_w, alog_v, dtb_v, norm_w)


def _dswa_kernel(cur_ref, prev_ref, o_ref, lse_ref, *, dil):
    BL = DSWA_BLOCK
    n_blk = cur_ref.shape[1] // (BL * dil)
    n = pl.program_id(1)

    def class_rows(ref, r, blk0, n_blocks):
        start, size = r + blk0 * BL * dil, n_blocks * BL
        rows = pl.ds(start, size, stride=dil) if dil > 1 else slice(start, start + size)
        return rows, jnp.concatenate([ref[t, rows, :] for t in range(ref.shape[0])], axis=1)

    ii = lax.broadcasted_iota(jnp.int32, (BL, 2 * BL), 0)
    jj = lax.broadcasted_iota(jnp.int32, (BL, 2 * BL), 1)
    band = (jj >= ii) & (jj <= ii + BL)
    first_valid = jnp.where(n > 0, 0, BL)
    band_first = band & (jj >= first_valid)
    kw = DSWA_HPG * DSWA_DH
    pad = jnp.zeros((BL, DSWA_OW - kw), F32)
    hsl = [slice(h * DSWA_DH, (h + 1) * DSWA_DH) for h in range(DSWA_HPG)]

    mc = min(n_blk, DSWA_CHAIN_GROUP // DSWA_HPG)
    units = [(r, m0) for r in range(dil) for m0 in range(0, n_blk, mc)]
    per_group = max(1, DSWA_CHAIN_GROUP // (mc * DSWA_HPG))
    for u0 in range(0, len(units), per_group):
        group = units[u0:u0 + per_group]
        rows, cur, k_all, v_all = {}, {}, {}, {}
        for u in group:
            r, m0 = u
            rows[u], cur[u] = class_rows(cur_ref, r, m0, mc)
            before = class_rows(prev_ref, r, 0, 1)[1] if m0 == 0 else class_rows(cur_ref, r, m0 - 1, 1)[1]
            k_all[u] = jnp.concatenate([before[:, kw:2 * kw], cur[u][:, kw:2 * kw]], axis=0).astype(BF16)
            v_all[u] = jnp.concatenate([before[:, 2 * kw:3 * kw], cur[u][:, 2 * kw:3 * kw]], axis=0).astype(BF16)
        chains = [(u, m, h) for u in group for m in range(mc) for h in range(DSWA_HPG)]
        keys = {m: slice(m * BL, (m + 2) * BL) for m in range(mc)}
        q = {(u, m, h): cur[u][m * BL:(m + 1) * BL, hsl[h]].astype(BF16) for u, m, h in chains}
        s = {(u, m, h): jnp.where(band_first if (u[1] == 0 and m == 0) else band,
                                  _dot_nt(q[u, m, h], k_all[u][keys[m], hsl[h]]), -jnp.inf) for u, m, h in chains}
        mx = {x: jnp.max(s[x], axis=-1, keepdims=True) for x in chains}
        p = {x: jnp.exp(s[x] - mx[x]) for x in chains}
        l = {x: jnp.sum(p[x], axis=-1, keepdims=True) for x in chains}
        o = {(u, m, h): _dot(p[u, m, h].astype(BF16), v_all[u][keys[m], hsl[h]]) * (1.0 / l[u, m, h])
             for u, m, h in chains}
        lse = {x: jnp.broadcast_to(mx[x] + jnp.log(l[x]), (BL, DSWA_DH)) for x in chains}
        for u in group:
            o_u = jnp.concatenate([jnp.concatenate([o[u, m, h] for h in range(DSWA_HPG)] + [pad], axis=1)
                                   for m in range(mc)], axis=0)
            lse_u = jnp.concatenate([jnp.concatenate([lse[u, m, h] for h in range(DSWA_HPG)] + [pad], axis=1)
                                     for m in range(mc)], axis=0)
            for t in range(DSWA_OW // LANES):
                o_ref[t, rows[u], :] = o_u[:, t * LANES:(t + 1) * LANES]
                lse_ref[t, rows[u], :] = lse_u[:, t * LANES:(t + 1) * LANES]


def _dswa(grp, batch, seq, dil):
    n_in, T, _ = grp.shape
    n_out = DSWA_OW // LANES
    span = DSWA_BLOCK * dil
    n_blk = max(1, min(DSWA_TOKENS_PER_STEP, seq) // span)
    step = n_blk * span
    n_steps = seq // step
    out = jax.ShapeDtypeStruct((n_out, T, LANES), F32)
    return pl.pallas_call(
        functools.partial(_dswa_kernel, dil=dil),
        grid=(batch, n_steps),
        in_specs=[pl.BlockSpec((n_in, step, LANES), lambda b, n: (0, b * n_steps + n, 0)),
                  pl.BlockSpec((n_in, span, LANES),
                               lambda b, n: (0, b * (seq // span) + jnp.maximum(n * n_blk - 1, 0), 0))],
        out_specs=[pl.BlockSpec((n_out, step, LANES), lambda b, n: (0, b * n_steps + n, 0))] * 2,
        out_shape=[out, out],
        compiler_params=_cparams("parallel", "arbitrary"),
    )(grp, grp)


def _split2(x):
    hi = x.astype(BF16)
    lo = (x - hi.astype(F32)).astype(BF16)
    return hi, lo


def _sb_kernel(q_ref, k_ref, v_ref, o_ref, vt_s, acc_s):
    BLK, DH = SB_BLOCK, SB_DH
    W = 2 * BLK
    G = q_ref.shape[0] // BLK
    n_kblocks = k_ref.shape[0] // BLK
    step_id = pl.program_id(2)
    i0 = step_id * G

    @pl.when(step_id == 0)
    def _():
        for j in range(n_kblocks):
            vt_s[j] = v_ref[j * BLK:(j + 1) * BLK, :].astype(F32).T.astype(BF16)

    ri = lax.broadcasted_iota(jnp.int32, (BLK, W), 0)
    ci = lax.broadcasted_iota(jnp.int32, (BLK, W), 1)
    causal = ri < (ci & (BLK - 1))
    own_head = (ri < DH) == (ci < BLK)
    r2 = lax.broadcasted_iota(jnp.int32, (BLK, BLK), 0)
    c2 = lax.broadcasted_iota(jnp.int32, (BLK, BLK), 1)
    after = jnp.where(c2 > r2, 1.0, 0.0).astype(BF16)
    after2 = jnp.concatenate([after, after], axis=1)

    q_bd = []
    for g in range(G):
        qt = q_ref[g * BLK:(g + 1) * BLK, :].astype(F32).T
        q_bd.append(jnp.where(own_head, jnp.concatenate([qt, qt], axis=1), 0.0).astype(BF16))

    def visit(js, c_rows, mask):
        ks = [k_ref[pl.ds(pl.multiple_of(js[g] * BLK, BLK), BLK), :].astype(BF16) for g in range(G)]
        vts = [vt_s[js[g]] for g in range(G)]
        zs = [_dot(ks[g], q_bd[g]) for g in range(G)]
        sps = [jnp.maximum(z, 0.0) + jnp.log(1.0 + jnp.exp(-jnp.abs(z))) for z in zs]
        sp_ms = sps if mask is None else [jnp.where(mask, sp, 0.0) for sp in sps]
        splits = [_split2(sp_m) for sp_m in sp_ms]
        sufs = [_dot(after2, jnp.concatenate([hi, lo], axis=0)) for hi, lo in splits]
        a_s = [jnp.exp(zs[g] - sps[g] - sufs[g] + c_rows[g]) for g in range(G)]
        if mask is not None:
            a_s = [jnp.where(mask, a, 0.0) for a in a_s]
        pvs = [_dot(vts[g], a_s[g].astype(BF16)) for g in range(G)]
        return pvs, [c_rows[g] - (sufs[g][0:1, :] + sp_ms[g][0:1, :]) for g in range(G)]

    pvs, cs = visit([i0 + g for g in range(G)], [jnp.zeros((1, W), F32)] * G, causal)
    for g in range(G):
        acc_s[g] = pvs[g]

    def cond(state):
        d, live, _ = state
        return (d < i0 + G) & live

    def body(state):
        d, _, cs = state
        js = [i0 + g - d for g in range(G)]
        c_in = [jnp.where(js[g] < 0, SB_LOG_DEAD, cs[g]) for g in range(G)]
        accs = [acc_s[g] for g in range(G)]
        pvs, new = visit([jnp.maximum(j, 0) for j in js], c_in, None)
        for g in range(G):
            acc_s[g] = accs[g] + pvs[g]
        c_max = jnp.max(functools.reduce(jnp.maximum, new))
        return d + 1, c_max > SB_LOG_ZERO, tuple(new)

    lax.while_loop(cond, body, (jnp.int32(1), jnp.bool_(True), tuple(cs)))
    for g in range(G):
        acc = acc_s[g]
        o_ref[g * BLK:(g + 1) * BLK, :] = jnp.concatenate([acc[:DH, :BLK], acc[DH:, BLK:]],
                                                          axis=0).T.astype(o_ref.dtype)


def _sb(proj, batch, seq):
    T = proj.shape[0]
    nq = seq // SB_BLOCK
    G = min(SB_QBLOCKS_PER_STEP, nq)
    n_steps = nq // G
    pairs = SB_HEADS // 2
    qo, ko, vo = PM_C // LANES, (PM_C + SB_HEADS * SB_DH) // LANES, (PM_C + 2 * SB_HEADS * SB_DH) // LANES
    return pl.pallas_call(
        _sb_kernel,
        grid=(batch, pairs, n_steps),
        in_specs=[pl.BlockSpec((G * SB_BLOCK, LANES), lambda b, p, i: (b * n_steps + i, qo + p)),
                  pl.BlockSpec((seq, LANES), lambda b, p, i: (b, ko + p)),
                  pl.BlockSpec((seq, LANES), lambda b, p, i: (b, vo + p))],
        out_specs=pl.BlockSpec((G * SB_BLOCK, LANES), lambda b, p, i: (b * n_steps + i, p)),
        out_shape=jax.ShapeDtypeStruct((T, SB_HEADS * SB_DH), BF16),
        scratch_shapes=[pltpu.VMEM((nq, LANES, SB_BLOCK), BF16),
                        pltpu.VMEM((G, LANES, 2 * SB_BLOCK), F32)],
        compiler_params=_cparams("parallel", "parallel", "arbitrary"),
    )(proj, proj, proj)


def _ret_kernel(qk_ref, v_ref, g_ref, cos_ref, sin_ref, nw_ref, o_ref, r_ref):
    tc = qk_ref.shape[0]
    C = RET_CHUNK
    half = RET_DK // 2

    @pl.when(pl.program_id(1) == 0)
    def _():
        r_ref[...] = jnp.zeros_like(r_ref)

    lane = lax.broadcasted_iota(jnp.int32, (C, LANES), 1)
    first = (lane & (RET_DK - 1)) < half
    ri = lax.broadcasted_iota(jnp.int32, (C, C), 0)
    ci = lax.broadcasted_iota(jnp.int32, (C, C), 1)
    diff = (ri - ci).astype(F32)
    causal = ri >= ci
    rowf = lax.broadcasted_iota(jnp.int32, (C, LANES), 0).astype(F32)
    nw = nw_ref[...]

    heads = range(RET_HEADS)
    chunks = range(tc // C)
    ch = [(c, h) for c in chunks for h in heads]
    rows = [slice(c * C, (c + 1) * C) for c in chunks]
    hsl = [slice(h * RET_DV, (h + 1) * RET_DV) for h in heads]
    dmat = [jnp.where(causal, jnp.exp(diff * _RET_LOG_GAMMA[h]), 0.0) for h in heads]
    xi = [jnp.exp((rowf + 1.0) * _RET_LOG_GAMMA[h]) for h in heads]
    zeta = [jnp.exp((C - 1.0 - rowf[:, :RET_DK]) * _RET_LOG_GAMMA[h]) for h in heads]

    def rope(x, c):
        sw = jnp.where(first, pltpu.roll(x, LANES - half, 1), pltpu.roll(x, half, 1))
        return x * cos_ref[rows[c], :] + sw * sin_ref[rows[c], :]

    kw = RET_HEADS * RET_DK
    qp = {(c, p): rope(qk_ref[rows[c], p * LANES:(p + 1) * LANES].astype(F32), c)
          for c in chunks for p in range(RET_HEADS // 2)}
    kp = {(c, p): rope(qk_ref[rows[c], kw + p * LANES:kw + (p + 1) * LANES].astype(F32), c)
          for c in chunks for p in range(RET_HEADS // 2)}
    q = {(c, h): qp[c, h // 2][:, (h % 2) * RET_DK:(h % 2 + 1) * RET_DK].astype(BF16) for c, h in ch}
    k = {(c, h): kp[c, h // 2][:, (h % 2) * RET_DK:(h % 2 + 1) * RET_DK] for c, h in ch}
    vb = {(c, h): v_ref[rows[c], hsl[h]].astype(BF16) for c, h in ch}
    gate = {(c, h): g_ref[rows[c], hsl[h]].astype(F32) for c, h in ch}
    intra = {(c, h): (_dot_nt(q[c, h], k[c, h].astype(BF16)) * dmat[h]).astype(BF16) for c, h in ch}
    delta = {(c, h): _dot((k[c, h] * zeta[h]).T.astype(BF16), vb[c, h]) for c, h in ch}
    r_in = {}
    for h in heads:
        r = r_ref[h]
        for c in chunks:
            r_in[c, h] = r.astype(BF16)
            r = r * math.exp(C * _RET_LOG_GAMMA[h]) + delta[c, h]
        r_ref[h] = r
    o = {(c, h): _dot(intra[c, h], vb[c, h]) + _dot(q[c, h], r_in[c, h]) * xi[h] for c, h in ch}
    for c, h in ch:
        gg = gate[c, h]
        o_ref[rows[c], hsl[h]] = (_rms(o[c, h], nw) * (gg * _sigmoid(gg))).astype(o_ref.dtype)


def _ret(proj, cos_t, sin_t, norm_w, batch, seq, tc):
    T = proj.shape[0]
    n_s = seq // tc
    W = RET_HEADS * RET_DV
    return pl.pallas_call(
        _ret_kernel,
        grid=(batch, n_s),
        in_specs=[pl.BlockSpec((tc, W), lambda b, s: (b * n_s + s, PM_D_QK // W)),
                  pl.BlockSpec((tc, W), lambda b, s: (b * n_s + s, PM_D_V // W)),
                  pl.BlockSpec((tc, W), lambda b, s: (b * n_s + s, PM_D_G // W)),
                  pl.BlockSpec((tc, LANES), lambda b, s: (s, 0)),
                  pl.BlockSpec((tc, LANES), lambda b, s: (s, 0)),
                  pl.BlockSpec((1, RET_DV), lambda b, s: (0, 0))],
        out_specs=pl.BlockSpec((tc, W), lambda b, s: (b * n_s + s, 0)),
        out_shape=jax.ShapeDtypeStruct((T, W), BF16),
        scratch_shapes=[pltpu.VMEM((RET_HEADS, RET_DK, RET_DV), F32)],
        compiler_params=_cparams("parallel", "arbitrary"),
    )(proj, proj, proj, cos_t, sin_t, norm_w)


def _merge_kernel(x_ref, oa_ref, ob0_ref, ob1_ref, ob2_ref, l0_ref, l1_ref, l2_ref, oc_ref, od_ref,
                  ga_ref, gb_ref, gc_ref, gd_ref, wa_ref, wb_ref, wc_ref, wd_ref, wo_ref, nw_ref, out_ref):
    def slabs(ref):
        return jnp.concatenate([ref[t] for t in range(ref.shape[0])], axis=1)

    l0, l1, l2 = slabs(l0_ref), slabs(l1_ref), slabs(l2_ref)
    m = jnp.maximum(jnp.maximum(l0, l1), l2)
    e0, e1, e2 = jnp.exp(l0 - m), jnp.exp(l1 - m), jnp.exp(l2 - m)
    inv = 1.0 / (e0 + e1 + e2)
    ob = jnp.concatenate([slabs(ob0_ref) * (e0 * inv), slabs(ob1_ref) * (e1 * inv), slabs(ob2_ref) * (e2 * inv)],
                         axis=1).astype(BF16)
    y = _sigmoid(ga_ref[...].astype(F32)) * _dot(oa_ref[...], wa_ref[...])
    y = y + _sigmoid(gb_ref[...].astype(F32)) * _dot(ob, wb_ref[...])
    y = y + _sigmoid(gc_ref[...].astype(F32)) * _dot(oc_ref[...], wc_ref[...])
    y = y + _sigmoid(gd_ref[...].astype(F32)) * _dot(od_ref[...], wd_ref[...])
    mixed = _dot(y.astype(BF16), wo_ref[...])
    out_ref[...] = x_ref[...] + _rms(mixed, nw_ref[...])


def _merge(x, oa, obs, lses, oc, od, proj, wa, wb, wc, wd, wo, nw, tm):
    T, D = x.shape
    row = lambda w: pl.BlockSpec((tm, w), lambda i: (i, 0))
    gate = lambda br: pl.BlockSpec((tm, D), lambda i: (i, PM_GATES // D + br))
    full = lambda a: pl.BlockSpec(a.shape, lambda i: (0, 0))
    return pl.pallas_call(
        _merge_kernel,
        grid=(T // tm,),
        in_specs=[row(D), row(oa.shape[1])]
                 + [pl.BlockSpec((DSWA_OW // LANES, tm, LANES), lambda i: (0, i, 0))] * 6
                 + [row(oc.shape[1]), row(od.shape[1])]
                 + [gate(0), gate(1), gate(2), gate(3)] + [full(wa), full(wb), full(wc), full(wd), full(wo), full(nw)],
        out_specs=row(D),
        out_shape=jax.ShapeDtypeStruct((T, D), F32),
        compiler_params=_cparams("parallel"),
    )(x, oa, *obs, *lses, oc, od, proj, proj, proj, proj, wa, wb, wc, wd, wo, nw)


def _mlp_kernel(x_ref, n1_ref, w1_ref, w2_ref, n2_ref, out_ref, h_ref, acc_ref):
    f = pl.program_id(1)

    @pl.when(f == 0)
    def _():
        h_ref[...] = _rms(x_ref[...], n1_ref[...]).astype(BF16)
        acc_ref[...] = jnp.zeros_like(acc_ref)

    hid = jnp.maximum(_dot(h_ref[...], w1_ref[...]), 0.0)
    acc_ref[...] += _dot((hid * hid).astype(BF16), w2_ref[...])

    @pl.when(f == pl.num_programs(1) - 1)
    def _():
        out_ref[...] = x_ref[...] + _rms(acc_ref[...], n2_ref[...])


def _mlp(x, n1, w1, w2, n2, tm, tf):
    T, D = x.shape
    F = w1.shape[1]
    return pl.pallas_call(
        _mlp_kernel,
        grid=(T // tm, F // tf),
        in_specs=[pl.BlockSpec((tm, D), lambda i, f: (i, 0)),
                  pl.BlockSpec((1, D), lambda i, f: (0, 0)),
                  pl.BlockSpec((D, tf), lambda i, f: (0, f)),
                  pl.BlockSpec((tf, D), lambda i, f: (f, 0)),
                  pl.BlockSpec((1, D), lambda i, f: (0, 0))],
        out_specs=pl.BlockSpec((tm, D), lambda i, f: (i, 0)),
        out_shape=jax.ShapeDtypeStruct((T, D), F32),
        scratch_shapes=[pltpu.VMEM((tm, D), BF16), pltpu.VMEM((tm, D), F32)],
        compiler_params=_cparams("parallel", "arbitrary"),
    )(x, n1, w1, w2, n2)


def _prep_w_in(w_in):
    sec = [w_in[:, _COL_OFF[i]:_COL_OFF[i + 1]] for i in range(len(_COL_SIZES))]
    a_qkv, a_z, a_a, a_b, b_qkv, c_qkv, d_qk, d_v, d_g, gates = sec
    sbw = SB_HEADS * SB_DH
    c_qkv = jnp.concatenate([c_qkv[:, :sbw] * SB_DH ** -0.5, c_qkv[:, sbw:]], axis=1)
    rw = RET_HEADS * RET_DK
    d_qk = jnp.concatenate([d_qk[:, :rw], d_qk[:, rw:] * RET_DK ** -0.5], axis=1)
    w_main = jnp.concatenate([a_qkv, a_z, c_qkv, d_qk, d_v, d_g, gates], axis=1).astype(BF16)
    bw = DSWA_HEADS * DSWA_DH
    gw = DSWA_HPG * DSWA_DH
    K = w_in.shape[0]
    groups = []
    for g in range(len(DSWA_GROUPS)):
        q = b_qkv[:, g * gw:(g + 1) * gw] * DSWA_DH ** -0.5
        k = b_qkv[:, bw + g * gw: bw + (g + 1) * gw]
        v = b_qkv[:, 2 * bw + g * gw: 2 * bw + (g + 1) * gw]
        groups += [q, k, v, jnp.zeros((K, DSWA_GW - 3 * gw), F32)]
    w_b = jnp.concatenate(groups + [a_a, a_b, jnp.zeros((K, LANES - 2 * GDN_HEADS), F32)], axis=1).astype(BF16)
    return w_main, w_b


def _rope_tables(seq):
    pos = jnp.arange(seq, dtype=jnp.int32).astype(F32)[:, None]
    inv_b = DSWA_ROPE_THETA ** (-jnp.arange(0, DSWA_ROPE_DIM, 2, dtype=F32) / DSWA_ROPE_DIM)
    ang = pos * inv_b[None, :]
    cb, sb = jnp.cos(ang), jnp.sin(ang)
    rest = DSWA_DH - DSWA_ROPE_DIM
    cos_b = jnp.tile(jnp.concatenate([cb, cb, jnp.ones((seq, rest), F32)], axis=1), (1, LANES // DSWA_DH))
    sin_b = jnp.tile(jnp.concatenate([-sb, sb, jnp.zeros((seq, rest), F32)], axis=1), (1, LANES // DSWA_DH))
    inv_r = RET_THETA ** (-jnp.linspace(0.0, 1.0, RET_DK // 2, dtype=F32))
    ang = pos * inv_r[None, :]
    cr, sr = jnp.cos(ang), jnp.sin(ang)
    cos_r = jnp.tile(jnp.concatenate([cr, cr], axis=1), (1, LANES // RET_DK))
    sin_r = jnp.tile(jnp.concatenate([-sr, sr], axis=1), (1, LANES // RET_DK))
    return cos_b, sin_b, cos_r, sin_r


def _lane_vec(v):
    return jnp.zeros((1, LANES), F32).at[0, :v.shape[0]].set(v.astype(F32))


def _layer(x, batch, seq, tabs, n_pre_mix, n_post_mix, n_pre_mlp, n_post_mlp, w_in, conv_w, a_log, dt_bias,
           gdn_norm, ret_norm, w_br_a, w_br_b, w_br_c, w_br_d, w_o, w_mlp_in, w_mlp_out):
    T = x.shape[0]
    cos_b, sin_b, cos_r, sin_r = tabs
    w_main, w_b = _prep_w_in(w_in)
    gain = n_pre_mix.reshape(1, -1)
    proj = _norm_matmul(x, gain, w_main, tm=min(1024, T), tn=1536)
    g0, g1, g2, ab = _dswa_proj(x, gain, w_b, cos_b, sin_b, seq, tm=min(512, seq))

    cw = jnp.zeros((8, conv_w.shape[1]), F32).at[:GDN_CONV].set(conv_w)
    oa = _gdn(proj, ab, cw, _lane_vec(a_log), _lane_vec(dt_bias), gdn_norm.reshape(1, -1), batch, seq,
              tc=min(256, seq))
    obs, lses = [], []
    for grp, (_, dil) in zip((g0, g1, g2), DSWA_GROUPS):
        o, lse = _dswa(grp, batch, seq, dil)
        obs.append(o)
        lses.append(lse)
    oc = _sb(proj, batch, seq)
    od = _ret(proj, cos_r, sin_r, ret_norm.reshape(1, -1), batch, seq, tc=min(512, seq))

    gw = DSWA_HPG * DSWA_DH
    wb = jnp.concatenate(
        [jnp.concatenate([w_br_b[g * gw:(g + 1) * gw], jnp.zeros((DSWA_OW - gw, D_MODEL), F32)], axis=0)
         for g in range(len(DSWA_GROUPS))], axis=0).astype(BF16)
    x1 = _merge(x, oa, obs, lses, oc, od, proj, w_br_a.astype(BF16), wb, w_br_c.astype(BF16),
                w_br_d.astype(BF16), w_o.astype(BF16), n_post_mix.reshape(1, -1), tm=min(512, T))
    return _mlp(x1, n_pre_mlp.reshape(1, -1), w_mlp_in.astype(BF16), w_mlp_out.astype(BF16),
                n_post_mlp.reshape(1, -1), tm=min(1024, T), tf=1024)


def kernel(x, norm_pre_mix, norm_post_mix, norm_pre_mlp, norm_post_mlp, w_in, conv_w, a_log, dt_bias, gdn_norm,
           ret_norm, w_br_a, w_br_b, w_br_c, w_br_d, w_o, w_mlp_in, w_mlp_out):
    batch, seq, d = x.shape
    tabs = _rope_tables(seq)
    h = x.reshape(batch * seq, d)
    for l in range(norm_pre_mix.shape[0]):
        h = _layer(h, batch, seq, tabs, norm_pre_mix[l], norm_post_mix[l], norm_pre_mlp[l], norm_post_mlp[l],
                   w_in[l], conv_w[l], a_log[l], dt_bias[l], gdn_norm[l], ret_norm[l], w_br_a[l], w_br_b[l],
                   w_br_c[l], w_br_d[l], w_o[l], w_mlp_in[l], w_mlp_out[l])
    return h.reshape(batch, seq, d)
```

```python
import functools
import math

import numpy as np
import jax
import jax.numpy as jnp
from jax import lax
from jax.experimental import pallas as pl
from jax.experimental.pallas import tpu as pltpu

F32 = jnp.float32
BF16 = jnp.bfloat16

D_MODEL = 1024
N_LAYERS = 2
GDN_HEADS, GDN_DK, GDN_DV, GDN_CONV, GDN_CHUNK = 4, 128, 128, 4, 64
DSWA_GROUPS = ((128, 1), (512, 4), (2048, 16))
DSWA_HPG, DSWA_DH, DSWA_BLOCK = 3, 64, 128
DSWA_HEADS = DSWA_HPG * len(DSWA_GROUPS)
DSWA_ROPE_THETA, DSWA_ROPE_DIM = 500000.0, DSWA_DH // 4
SB_HEADS, SB_DH, SB_BLOCK = 8, 64, 128
SB_LOG_ZERO = -110.0
SB_LOG_DEAD = -1e30
SB_QBLOCKS_PER_STEP = 8
DSWA_TOKENS_PER_STEP = 2048
DSWA_CHAIN_GROUP = 12
RET_HEADS, RET_DK, RET_DV, RET_THETA = 4, 64, 128, 10000.0
RET_CHUNK = 128
D_FF = 4 * D_MODEL
NORM_EPS = 1e-6
L2_EPS = 1e-6

_COL_SIZES = (1536, 512, 4, 4, 1728, 1536, 512, 512, 512, 4096)
_COL_OFF = np.concatenate([[0], np.cumsum(_COL_SIZES)])

PM_A_QKV, PM_A_Z, PM_C, PM_D_QK, PM_D_V, PM_D_G, PM_GATES, PM_WIDTH = 0, 1536, 2048, 3584, 4096, 4608, 5120, 9216
DSWA_GW = 640
DSWA_OW = 256

VMEM_LIMIT = 48 * 1024 * 1024
LANES = 128

_RET_LOG_GAMMA = [float(np.log1p(-np.exp2(np.float32(-5.0 - h))).astype(np.float32)) for h in range(RET_HEADS)]


def _cparams(*sem):
    return pltpu.CompilerParams(dimension_semantics=sem, vmem_limit_bytes=VMEM_LIMIT)


def _sigmoid(x):
    return 1.0 / (1.0 + jnp.exp(-x))


def _softplus(x):
    return jnp.maximum(x, 0.0) + jnp.log(1.0 + jnp.exp(-jnp.abs(x)))


def _dot(a, b):
    return jnp.dot(a, b, preferred_element_type=F32)


def _dot_nt(a, b):
    return lax.dot_general(a, b, (((1,), (1,)), ((), ())), preferred_element_type=F32)


def _split3(x):
    hi = x.astype(BF16)
    r = x - hi.astype(F32)
    mid = r.astype(BF16)
    lo = (r - mid.astype(F32)).astype(BF16)
    return hi, mid, lo


def _dot_exact_lhs(a_bf16, x):
    hi, mid, lo = _split3(x)
    return _dot(a_bf16, hi) + (_dot(a_bf16, mid) + _dot(a_bf16, lo))


def _dot_f32(a, b):
    ah, am, al = _split3(a)
    bh, bm, bl = _split3(b)
    return (_dot(ah, bh) + (_dot(ah, bm) + _dot(am, bh))
            + (_dot(am, bm) + _dot(ah, bl) + _dot(al, bh)))


def _rms(x, w):
    ms = jnp.mean(x * x, axis=-1, keepdims=True)
    return x * lax.rsqrt(ms + NORM_EPS) * w


def _norm_matmul_kernel(x_ref, g_ref, w_ref, o_ref, h_ref):
    @pl.when(pl.program_id(1) == 0)
    def _():
        h_ref[...] = _rms(x_ref[...], g_ref[...]).astype(BF16)

    o_ref[...] = _dot(h_ref[...], w_ref[...]).astype(o_ref.dtype)


def _norm_matmul(x, gain, w, tm, tn):
    T, K = x.shape
    N = w.shape[1]
    return pl.pallas_call(
        _norm_matmul_kernel,
        grid=(T // tm, N // tn),
        in_specs=[pl.BlockSpec((tm, K), lambda i, j: (i, 0)),
                  pl.BlockSpec((1, K), lambda i, j: (0, 0)),
                  pl.BlockSpec((K, tn), lambda i, j: (0, j))],
        out_specs=pl.BlockSpec((tm, tn), lambda i, j: (i, j)),
        out_shape=jax.ShapeDtypeStruct((T, N), BF16),
        scratch_shapes=[pltpu.VMEM((tm, K), BF16)],
        compiler_params=_cparams("parallel", "arbitrary"),
    )(x, gain, w)


def _dswa_proj_kernel(x_ref, g_ref, w_ref, cos_ref, sin_ref, o0_ref, o1_ref, o2_ref, oab_ref):
    h = _rms(x_ref[...], g_ref[...]).astype(BF16)
    p = _dot(h, w_ref[...])
    cos = cos_ref[...]
    sin = sin_ref[...]
    lane = lax.broadcasted_iota(jnp.int32, cos.shape, 1)
    first = (lane & (DSWA_DH - 1)) < (DSWA_ROPE_DIM // 2)
    for g, o_ref in enumerate((o0_ref, o1_ref, o2_ref)):
        base = g * DSWA_GW
        for t in range(3):
            xt = p[:, base + t * LANES: base + (t + 1) * LANES]
            sw = jnp.where(first, pltpu.roll(xt, LANES - DSWA_ROPE_DIM // 2, 1),
                           pltpu.roll(xt, DSWA_ROPE_DIM // 2, 1))
            o_ref[t] = xt * cos + sw * sin
        for t in range(3, DSWA_GW // LANES):
            o_ref[t] = p[:, base + t * LANES: base + (t + 1) * LANES]
    oab_ref[...] = p[:, 3 * DSWA_GW:]


def _dswa_proj(x, gain, w, cos_t, sin_t, seq, tm):
    T, K = x.shape
    N = w.shape[1]
    n_s = seq // tm
    grp = jax.ShapeDtypeStruct((DSWA_GW // LANES, T, LANES), F32)
    return pl.pallas_call(
        _dswa_proj_kernel,
        grid=(T // tm,),
        in_specs=[pl.BlockSpec((tm, K), lambda i: (i, 0)),
                  pl.BlockSpec((1, K), lambda i: (0, 0)),
                  pl.BlockSpec((K, N), lambda i: (0, 0)),
                  pl.BlockSpec((tm, LANES), lambda i: (i % n_s, 0)),
                  pl.BlockSpec((tm, LANES), lambda i: (i % n_s, 0))],
        out_specs=[pl.BlockSpec((DSWA_GW // LANES, tm, LANES), lambda i: (0, i, 0))] * 3
                  + [pl.BlockSpec((tm, LANES), lambda i: (i, 0))],
        out_shape=[grp, grp, grp, jax.ShapeDtypeStruct((T, LANES), F32)],
        compiler_params=_cparams("parallel"),
    )(x, gain, w, cos_t, sin_t)


def _per_head_matmul(xs, p_cat, diag_mask):
    C = GDN_CHUNK
    ph, pl_ = [jnp.concatenate([t] * GDN_HEADS, axis=0) * diag_mask for t in _split2(p_cat)]
    parts = [_split2(x) for x in xs]
    his = [hi for hi, _ in parts]
    by_hi = _dot(jnp.concatenate(his + [lo for _, lo in parts], axis=0), ph)
    by_lo = _dot(jnp.concatenate(his, axis=0), pl_)
    n = len(xs)
    return [by_hi[i * C:(i + 1) * C] + (by_hi[(n + i) * C:(n + i + 1) * C] + by_lo[i * C:(i + 1) * C])
            for i in range(n)]


def _unit_lower_inverse(lows, eye_cat, diag_mask):
    xs = [eye_cat - low for low in lows]
    ps = [_per_head_matmul([low], low, diag_mask)[0] for low in lows]
    yield
    n = 4
    while n < GDN_CHUNK:
        both = [_per_head_matmul([x, p], p, diag_mask) for x, p in zip(xs, ps)]
        xs = [x + xp for x, (xp, _) in zip(xs, both)]
        ps = [pp for _, pp in both]
        n *= 2
        yield
    return [x + _per_head_matmul([x], p, diag_mask)[0] for x, p in zip(xs, ps)]


def _interleave(*gens):
    results = [None] * len(gens)
    live = list(range(len(gens)))
    while live:
        for i in list(live):
            try:
                next(gens[i])
            except StopIteration as stop:
                results[i] = stop.value
                live.remove(i)
    return results


def _gdn_kernel(qkv_ref, z_ref, ab_ref, cw_ref, alog_ref, dtb_ref, nw_ref, o_ref,
                carry_ref, q_s, k_s, v_s, state_ref, uw_s, attn_s, qd_s, kdt_s, egl_s):
    tc = qkv_ref.shape[0]
    C = GDN_CHUNK
    H = GDN_HEADS
    HD = GDN_HEADS * GDN_DK
    step = pl.program_id(1)
    n_tiles = pl.num_programs(1) - 1
    n_chunks = tc // C
    heads = range(H)
    chunks = range(n_chunks)
    ch = [(c, h) for c in chunks for h in heads]
    hsl = [slice(h * GDN_DK, (h + 1) * GDN_DK) for h in heads]
    nw = nw_ref[...]

    @pl.when(step == 0)
    def _():
        carry_ref[0:8, :] = jnp.zeros((8, 3 * HD), F32)
        for ref in (state_ref, uw_s, attn_s, qd_s, kdt_s, egl_s):
            ref[...] = jnp.zeros_like(ref)

    def recurrence():
        st = [state_ref[h] for h in heads]
        uw = {(c, h): uw_s[c * H + h] for c, h in ch}
        attn = {(c, h): attn_s[c * H + h] for c, h in ch}
        q_dec = {(c, h): qd_s[c * H + h] for c, h in ch}
        k_dec_t = {(c, h): kdt_s[c * H + h] for c, h in ch}
        egl = {(c, h): egl_s[c * H + h][0:1, :] for c, h in ch}
        gate = {(c, h): z_ref[c * C:(c + 1) * C, hsl[h]].astype(F32) for c, h in ch}
        yield
        outs = {}
        for c in chunks:
            stb = [st[h].astype(BF16) for h in heads]
            w_st = [_dot(uw[c, h][:, GDN_DV:].astype(BF16), stb[h]) for h in heads]
            q_st = [_dot(q_dec[c, h], stb[h]) for h in heads]
            yield
            vnb = [(uw[c, h][:, :GDN_DV] - w_st[h]).astype(BF16) for h in heads]
            o = [q_st[h] + _dot(attn[c, h], vnb[h]) for h in heads]
            upd = [_dot(k_dec_t[c, h], vnb[h]) for h in heads]
            yield
            for h in heads:
                st[h] = st[h] * egl[c, h] + upd[h]
                zz = gate[c, h]
                outs[c, h] = _rms(o[h], nw) * (zz * _sigmoid(zz))
            yield
        return outs, st

    def store_recurrence(outs, st):
        for c, h in ch:
            o_ref[c * C:(c + 1) * C, hsl[h]] = outs[c, h].astype(o_ref.dtype)
        for h in heads:
            state_ref[h] = st[h]

    @pl.when(step == n_tiles)
    def _():
        store_recurrence(*_interleave(recurrence())[0])

    @pl.when(step < n_tiles)
    def _():
        (outs, st), pre = _interleave(
            recurrence(), _gdn_prepare(qkv_ref, ab_ref, cw_ref, alog_ref, dtb_ref, carry_ref, q_s, k_s, v_s))
        store_recurrence(outs, st)
        for c, h in ch:
            uw_s[c * H + h] = pre["uw"][c, h]
            attn_s[c * H + h] = pre["attn"][c, h]
            qd_s[c * H + h] = pre["q_dec"][c, h]
            kdt_s[c * H + h] = pre["k_dec_t"][c, h]
            egl_s[c * H + h] = jnp.broadcast_to(pre["egl"][c, h], (8, GDN_DK))


def _gdn_prepare(qkv_ref, ab_ref, cw_ref, alog_ref, dtb_ref, carry_ref, q_s, k_s, v_s):
    tc = qkv_ref.shape[0]
    C = GDN_CHUNK
    H = GDN_HEADS
    HD = GDN_HEADS * GDN_DK
    carry_ref[8:, :] = qkv_ref[...].astype(F32)
    for grp, dst in enumerate((q_s, k_s, v_s)):
        cols = slice(grp * HD, (grp + 1) * HD)
        y = carry_ref[8:, cols] * cw_ref[GDN_CONV - 1:GDN_CONV, cols]
        for j in range(1, GDN_CONV):
            y = y + carry_ref[8 - j:8 - j + tc, cols] * cw_ref[GDN_CONV - 1 - j:GDN_CONV - j, cols]
        y = y * _sigmoid(y)
        if grp == 2:
            dst[...] = y
        else:
            for h in range(GDN_HEADS):
                yh = y[:, h * GDN_DK:(h + 1) * GDN_DK]
                yh = yh * lax.rsqrt(jnp.sum(yh * yh, axis=-1, keepdims=True) + L2_EPS)
                if grp == 0:
                    yh = yh * (GDN_DK ** -0.5)
                dst[:, h * GDN_DK:(h + 1) * GDN_DK] = yh
        yield
    carry_ref[0:8, :] = carry_ref[tc:tc + 8, :]

    ri = lax.broadcasted_iota(jnp.int32, (C, C), 0)
    ci = lax.broadcasted_iota(jnp.int32, (C, C), 1)
    incl = ri >= ci
    strict = ri > ci
    a_low = jnp.where(incl, 1.0, 0.0).astype(BF16)
    eye_cat = jnp.concatenate([jnp.where(ri == ci, 1.0, 0.0).astype(F32)] * H, axis=1)
    rb = lax.broadcasted_iota(jnp.int32, (H * C, H * C), 0) // C
    cb = lax.broadcasted_iota(jnp.int32, (H * C, H * C), 1) // C
    diag_mask = jnp.where(rb == cb, 1.0, 0.0).astype(BF16)
    neg_exp_alog = -jnp.exp(alog_ref[...])
    dtb = dtb_ref[...]
    n_chunks = tc // C
    heads = range(H)
    hsl = [slice(h * GDN_DK, (h + 1) * GDN_DK) for h in heads]

    chunks = range(n_chunks)
    ch = [(c, h) for c in chunks for h in heads]
    rows = [slice(c * C, (c + 1) * C) for c in chunks]
    ab = [ab_ref[rows[c], :] for c in chunks]
    gv = [neg_exp_alog * _softplus(ab[c] + dtb) for c in chunks]
    bv = [_sigmoid(ab[c]) for c in chunks]
    gc_all = [_dot_exact_lhs(a_low, gv[c]) for c in chunks]
    gc_t = [gc_all[c].T for c in chunks]
    yield
    q = {(c, h): q_s[rows[c], hsl[h]] for c, h in ch}
    k = {(c, h): k_s[rows[c], hsl[h]] for c, h in ch}
    v = {(c, h): v_s[rows[c], hsl[h]] for c, h in ch}
    gc = {(c, h): jnp.broadcast_to(gc_all[c][:, h:h + 1], (C, GDN_DK)) for c, h in ch}
    beta = {(c, h): jnp.broadcast_to(bv[c][:, H + h:H + h + 1], (C, GDN_DK)) for c, h in ch}
    decay = {(c, h): jnp.exp(jnp.where(incl, gc[c, h][:, :C] - gc_t[c][h:h + 1, :], -jnp.inf)) for c, h in ch}
    egc = {x: jnp.exp(gc[x]) for x in ch}
    gl = {x: gc[x][C - 1:C, :] for x in ch}
    kb = {x: k[x] * beta[x] for x in ch}
    kbf = {x: k[x].astype(BF16) for x in ch}
    kk = {x: _dot_nt(kb[x].astype(BF16), kbf[x]) for x in ch}
    low_cat = [jnp.concatenate([jnp.where(strict, kk[c, h] * decay[c, h], 0.0) for h in heads], axis=1)
               for c in chunks]
    yield
    t_cat = [t.astype(BF16) for t in (yield from _unit_lower_inverse(low_cat, eye_cat, diag_mask))]
    uw = {(c, h): _dot(t_cat[c][:, h * C:(h + 1) * C],
                       jnp.concatenate([v[c, h] * beta[c, h], kb[c, h] * egc[c, h]], axis=1).astype(BF16))
          for c, h in ch}
    yield
    attn = {x: (_dot_nt(q[x].astype(BF16), kbf[x]) * decay[x]).astype(BF16) for x in ch}
    q_dec = {x: (q[x] * egc[x]).astype(BF16) for x in ch}
    k_dec_t = {x: (k[x] * jnp.exp(gl[x] - gc[x])).T.astype(BF16) for x in ch}
    egl = {x: jnp.exp(gl[x]) for x in ch}
    return dict(uw=uw, attn=attn, q_dec=q_dec, k_dec_t=k_dec_t, egl=egl)


def _gdn(proj, ab, conv_w, alog_v, dtb_v, norm_w, batch, seq, tc):
    T = proj.shape[0]
    n_s = seq // tc
    HD = GDN_HEADS * GDN_DK
    n_pairs = (tc // GDN_CHUNK) * GDN_HEADS
    cur = lambda b, s: b * n_s + jnp.minimum(s, n_s - 1)
    lag = lambda b, s: b * n_s + jnp.maximum(s - 1, 0)
    return pl.pallas_call(
        _gdn_kernel,
        grid=(batch, n_s + 1),
        in_specs=[pl.BlockSpec((tc, 3 * HD), lambda b, s: (cur(b, s), PM_A_QKV // (3 * HD))),
                  pl.BlockSpec((tc, HD), lambda b, s: (lag(b, s), PM_A_Z // HD)),
                  pl.BlockSpec((tc, LANES), lambda b, s: (cur(b, s), 0)),
                  pl.BlockSpec((8, 3 * HD), lambda b, s: (0, 0)),
                  pl.BlockSpec((1, LANES), lambda b, s: (0, 0)),
                  pl.BlockSpec((1, LANES), lambda b, s: (0, 0)),
                  pl.BlockSpec((1, GDN_DV), lambda b, s: (0, 0))],
        out_specs=pl.BlockSpec((tc, HD), lambda b, s: (lag(b, s), 0)),
        out_shape=jax.ShapeDtypeStruct((T, HD), BF16),
        scratch_shapes=[pltpu.VMEM((tc + 8, 3 * HD), F32),
                        pltpu.VMEM((tc, HD), F32), pltpu.VMEM((tc, HD), F32), pltpu.VMEM((tc, HD), F32),
                        pltpu.VMEM((GDN_HEADS, GDN_DK, GDN_DV), F32),
                        pltpu.VMEM((n_pairs, GDN_CHUNK, 2 * GDN_DK), F32),
                        pltpu.VMEM((n_pairs, GDN_CHUNK, GDN_CHUNK), BF16),
                        pltpu.VMEM((n_pairs, GDN_CHUNK, GDN_DK), BF16),
                        pltpu.VMEM((n_pairs, GDN_DK, GDN_CHUNK), BF16),
                        pltpu.VMEM((n_pairs, 8, GDN_DK), F32)],
        compiler_params=_cparams("parallel", "arbitrary"),
    )(proj, proj, ab, conv_w, alog_v, dtb_v, norm_w)


def _dswa_kernel(cur_ref, prev_ref, o_ref, lse_ref, *, dil):
    BL = DSWA_BLOCK
    n_blk = cur_ref.shape[1] // (BL * dil)
    n = pl.program_id(1)

    def class_rows(ref, r, blk0, n_blocks):
        start, size = r + blk0 * BL * dil, n_blocks * BL
        rows = pl.ds(start, size, stride=dil) if dil > 1 else slice(start, start + size)
        return rows, jnp.concatenate([ref[t, rows, :] for t in range(ref.shape[0])], axis=1)

    ii = lax.broadcasted_iota(jnp.int32, (BL, 2 * BL), 0)
    jj = lax.broadcasted_iota(jnp.int32, (BL, 2 * BL), 1)
    band = (jj >= ii) & (jj <= ii + BL)
    first_valid = jnp.where(n > 0, 0, BL)
    band_first = band & (jj >= first_valid)
    kw = DSWA_HPG * DSWA_DH
    pad = jnp.zeros((BL, DSWA_OW - kw), F32)
    hsl = [slice(h * DSWA_DH, (h + 1) * DSWA_DH) for h in range(DSWA_HPG)]

    mc = min(n_blk, DSWA_CHAIN_GROUP // DSWA_HPG)
    units = [(r, m0) for r in range(dil) for m0 in range(0, n_blk, mc)]
    per_group = max(1, DSWA_CHAIN_GROUP // (mc * DSWA_HPG))
    for u0 in range(0, len(units), per_group):
        group = units[u0:u0 + per_group]
        rows, cur, k_all, v_all = {}, {}, {}, {}
        for u in group:
            r, m0 = u
            rows[u], cur[u] = class_rows(cur_ref, r, m0, mc)
            before = class_rows(prev_ref, r, 0, 1)[1] if m0 == 0 else class_rows(cur_ref, r, m0 - 1, 1)[1]
            k_all[u] = jnp.concatenate([before[:, kw:2 * kw], cur[u][:, kw:2 * kw]], axis=0).astype(BF16)
            v_all[u] = jnp.concatenate([before[:, 2 * kw:3 * kw], cur[u][:, 2 * kw:3 * kw]], axis=0).astype(BF16)
        chains = [(u, m, h) for u in group for m in range(mc) for h in range(DSWA_HPG)]
        keys = {m: slice(m * BL, (m + 2) * BL) for m in range(mc)}
        q = {(u, m, h): cur[u][m * BL:(m + 1) * BL, hsl[h]].astype(BF16) for u, m, h in chains}
        s = {(u, m, h): jnp.where(band_first if (u[1] == 0 and m == 0) else band,
                                  _dot_nt(q[u, m, h], k_all[u][keys[m], hsl[h]]), -jnp.inf) for u, m, h in chains}
        mx = {x: jnp.max(s[x], axis=-1, keepdims=True) for x in chains}
        p = {x: jnp.exp(s[x] - mx[x]) for x in chains}
        l = {x: jnp.sum(p[x], axis=-1, keepdims=True) for x in chains}
        o = {(u, m, h): _dot(p[u, m, h].astype(BF16), v_all[u][keys[m], hsl[h]]) * (1.0 / l[u, m, h])
             for u, m, h in chains}
        lse = {x: jnp.broadcast_to(mx[x] + jnp.log(l[x]), (BL, DSWA_DH)) for x in chains}
        for u in group:
            o_u = jnp.concatenate([jnp.concatenate([o[u, m, h] for h in range(DSWA_HPG)] + [pad], axis=1)
                                   for m in range(mc)], axis=0)
            lse_u = jnp.concatenate([jnp.concatenate([lse[u, m, h] for h in range(DSWA_HPG)] + [pad], axis=1)
                                     for m in range(mc)], axis=0)
            for t in range(DSWA_OW // LANES):
                o_ref[t, rows[u], :] = o_u[:, t * LANES:(t + 1) * LANES]
                lse_ref[t, rows[u], :] = lse_u[:, t * LANES:(t + 1) * LANES]


def _dswa(grp, batch, seq, dil):
    n_in, T, _ = grp.shape
    n_out = DSWA_OW // LANES
    span = DSWA_BLOCK * dil
    n_blk = max(1, min(DSWA_TOKENS_PER_STEP, seq) // span)
    step = n_blk * span
    n_steps = seq // step
    out = jax.ShapeDtypeStruct((n_out, T, LANES), F32)
    return pl.pallas_call(
        functools.partial(_dswa_kernel, dil=dil),
        grid=(batch, n_steps),
        in_specs=[pl.BlockSpec((n_in, step, LANES), lambda b, n: (0, b * n_steps + n, 0)),
                  pl.BlockSpec((n_in, span, LANES),
                               lambda b, n: (0, b * (seq // span) + jnp.maximum(n * n_blk - 1, 0), 0))],
        out_specs=[pl.BlockSpec((n_out, step, LANES), lambda b, n: (0, b * n_steps + n, 0))] * 2,
        out_shape=[out, out],
        compiler_params=_cparams("parallel", "arbitrary"),
    )(grp, grp)


def _split2(x):
    hi = x.astype(BF16)
    lo = (x - hi.astype(F32)).astype(BF16)
    return hi, lo


def _sb_kernel(q_ref, k_ref, v_ref, o_ref, vt_s, acc_s):
    BLK, DH = SB_BLOCK, SB_DH
    W = 2 * BLK
    G = q_ref.shape[0] // BLK
    n_kblocks = k_ref.shape[0] // BLK
    step_id = pl.program_id(2)
    i0 = step_id * G

    @pl.when(step_id == 0)
    def _():
        for j in range(n_kblocks):
            vt_s[j] = v_ref[j * BLK:(j + 1) * BLK, :].astype(F32).T.astype(BF16)

    ri = lax.broadcasted_iota(jnp.int32, (BLK, W), 0)
    ci = lax.broadcasted_iota(jnp.int32, (BLK, W), 1)
    causal = ri < (ci & (BLK - 1))
    own_head = (ri < DH) == (ci < BLK)
    r2 = lax.broadcasted_iota(jnp.int32, (BLK, BLK), 0)
    c2 = lax.broadcasted_iota(jnp.int32, (BLK, BLK), 1)
    after = jnp.where(c2 > r2, 1.0, 0.0).astype(BF16)
    after2 = jnp.concatenate([after, after], axis=1)

    q_bd = []
    for g in range(G):
        qt = q_ref[g * BLK:(g + 1) * BLK, :].astype(F32).T
        q_bd.append(jnp.where(own_head, jnp.concatenate([qt, qt], axis=1), 0.0).astype(BF16))

    def visit(js, c_rows, mask):
        ks = [k_ref[pl.ds(pl.multiple_of(js[g] * BLK, BLK), BLK), :].astype(BF16) for g in range(G)]
        vts = [vt_s[js[g]] for g in range(G)]
        zs = [_dot(ks[g], q_bd[g]) for g in range(G)]
        sps = [jnp.maximum(z, 0.0) + jnp.log(1.0 + jnp.exp(-jnp.abs(z))) for z in zs]
        sp_ms = sps if mask is None else [jnp.where(mask, sp, 0.0) for sp in sps]
        splits = [_split2(sp_m) for sp_m in sp_ms]
        sufs = [_dot(after2, jnp.concatenate([hi, lo], axis=0)) for hi, lo in splits]
        a_s = [jnp.exp(zs[g] - sps[g] - sufs[g] + c_rows[g]) for g in range(G)]
        if mask is not None:
            a_s = [jnp.where(mask, a, 0.0) for a in a_s]
        pvs = [_dot(vts[g], a_s[g].astype(BF16)) for g in range(G)]
        return pvs, [c_rows[g] - (sufs[g][0:1, :] + sp_ms[g][0:1, :]) for g in range(G)]

    pvs, cs = visit([i0 + g for g in range(G)], [jnp.zeros((1, W), F32)] * G, causal)
    for g in range(G):
        acc_s[g] = pvs[g]

    def cond(state):
        d, live, _ = state
        return (d < i0 + G) & live

    def body(state):
        d, _, cs = state
        js = [i0 + g - d for g in range(G)]
        c_in = [jnp.where(js[g] < 0, SB_LOG_DEAD, cs[g]) for g in range(G)]
        accs = [acc_s[g] for g in range(G)]
        pvs, new = visit([jnp.maximum(j, 0) for j in js], c_in, None)
        for g in range(G):
            acc_s[g] = accs[g] + pvs[g]
        c_max = jnp.max(functools.reduce(jnp.maximum, new))
        return d + 1, c_max > SB_LOG_ZERO, tuple(new)

    lax.while_loop(cond, body, (jnp.int32(1), jnp.bool_(True), tuple(cs)))
    for g in range(G):
        acc = acc_s[g]
        o_ref[g * BLK:(g + 1) * BLK, :] = jnp.concatenate([acc[:DH, :BLK], acc[DH:, BLK:]],
                                                          axis=0).T.astype(o_ref.dtype)


def _sb(proj, batch, seq):
    T = proj.shape[0]
    nq = seq // SB_BLOCK
    G = min(SB_QBLOCKS_PER_STEP, nq)
    n_steps = nq // G
    pairs = SB_HEADS // 2
    qo, ko, vo = PM_C // LANES, (PM_C + SB_HEADS * SB_DH) // LANES, (PM_C + 2 * SB_HEADS * SB_DH) // LANES
    return pl.pallas_call(
        _sb_kernel,
        grid=(batch, pairs, n_steps),
        in_specs=[pl.BlockSpec((G * SB_BLOCK, LANES), lambda b, p, i: (b * n_steps + i, qo + p)),
                  pl.BlockSpec((seq, LANES), lambda b, p, i: (b, ko + p)),
                  pl.BlockSpec((seq, LANES), lambda b, p, i: (b, vo + p))],
        out_specs=pl.BlockSpec((G * SB_BLOCK, LANES), lambda b, p, i: (b * n_steps + i, p)),
        out_shape=jax.ShapeDtypeStruct((T, SB_HEADS * SB_DH), BF16),
        scratch_shapes=[pltpu.VMEM((nq, LANES, SB_BLOCK), BF16),
                        pltpu.VMEM((G, LANES, 2 * SB_BLOCK), F32)],
        compiler_params=_cparams("parallel", "parallel", "arbitrary"),
    )(proj, proj, proj)


def _ret_kernel(qk_ref, v_ref, g_ref, cos_ref, sin_ref, nw_ref, o_ref, r_ref):
    tc = qk_ref.shape[0]
    C = RET_CHUNK
    half = RET_DK // 2

    @pl.when(pl.program_id(1) == 0)
    def _():
        r_ref[...] = jnp.zeros_like(r_ref)

    lane = lax.broadcasted_iota(jnp.int32, (C, LANES), 1)
    first = (lane & (RET_DK - 1)) < half
    ri = lax.broadcasted_iota(jnp.int32, (C, C), 0)
    ci = lax.broadcasted_iota(jnp.int32, (C, C), 1)
    diff = (ri - ci).astype(F32)
    causal = ri >= ci
    rowf = lax.broadcasted_iota(jnp.int32, (C, LANES), 0).astype(F32)
    nw = nw_ref[...]

    heads = range(RET_HEADS)
    chunks = range(tc // C)
    ch = [(c, h) for c in chunks for h in heads]
    rows = [slice(c * C, (c + 1) * C) for c in chunks]
    hsl = [slice(h * RET_DV, (h + 1) * RET_DV) for h in heads]
    dmat = [jnp.where(causal, jnp.exp(diff * _RET_LOG_GAMMA[h]), 0.0) for h in heads]
    xi = [jnp.exp((rowf + 1.0) * _RET_LOG_GAMMA[h]) for h in heads]
    zeta = [jnp.exp((C - 1.0 - rowf[:, :RET_DK]) * _RET_LOG_GAMMA[h]) for h in heads]

    def rope(x, c):
        sw = jnp.where(first, pltpu.roll(x, LANES - half, 1), pltpu.roll(x, half, 1))
        return x * cos_ref[rows[c], :] + sw * sin_ref[rows[c], :]

    kw = RET_HEADS * RET_DK
    qp = {(c, p): rope(qk_ref[rows[c], p * LANES:(p + 1) * LANES].astype(F32), c)
          for c in chunks for p in range(RET_HEADS // 2)}
    kp = {(c, p): rope(qk_ref[rows[c], kw + p * LANES:kw + (p + 1) * LANES].astype(F32), c)
          for c in chunks for p in range(RET_HEADS // 2)}
    q = {(c, h): qp[c, h // 2][:, (h % 2) * RET_DK:(h % 2 + 1) * RET_DK].astype(BF16) for c, h in ch}
    k = {(c, h): kp[c, h // 2][:, (h % 2) * RET_DK:(h % 2 + 1) * RET_DK] for c, h in ch}
    vb = {(c, h): v_ref[rows[c], hsl[h]].astype(BF16) for c, h in ch}
    gate = {(c, h): g_ref[rows[c], hsl[h]].astype(F32) for c, h in ch}
    intra = {(c, h): (_dot_nt(q[c, h], k[c, h].astype(BF16)) * dmat[h]).astype(BF16) for c, h in ch}
    delta = {(c, h): _dot((k[c, h] * zeta[h]).T.astype(BF16), vb[c, h]) for c, h in ch}
    r_in = {}
    for h in heads:
        r = r_ref[h]
        for c in chunks:
            r_in[c, h] = r.astype(BF16)
            r = r * math.exp(C * _RET_LOG_GAMMA[h]) + delta[c, h]
        r_ref[h] = r
    o = {(c, h): _dot(intra[c, h], vb[c, h]) + _dot(q[c, h], r_in[c, h]) * xi[h] for c, h in ch}
    for c, h in ch:
        gg = gate[c, h]
        o_ref[rows[c], hsl[h]] = (_rms(o[c, h], nw) * (gg * _sigmoid(gg))).astype(o_ref.dtype)


def _ret(proj, cos_t, sin_t, norm_w, batch, seq, tc):
    T = proj.shape[0]
    n_s = seq // tc
    W = RET_HEADS * RET_DV
    return pl.pallas_call(
        _ret_kernel,
        grid=(batch, n_s),
        in_specs=[pl.BlockSpec((tc, W), lambda b, s: (b * n_s + s, PM_D_QK // W)),
                  pl.BlockSpec((tc, W), lambda b, s: (b * n_s + s, PM_D_V // W)),
                  pl.BlockSpec((tc, W), lambda b, s: (b * n_s + s, PM_D_G // W)),
                  pl.BlockSpec((tc, LANES), lambda b, s: (s, 0)),
                  pl.BlockSpec((tc, LANES), lambda b, s: (s, 0)),
                  pl.BlockSpec((1, RET_DV), lambda b, s: (0, 0))],
        out_specs=pl.BlockSpec((tc, W), lambda b, s: (b * n_s + s, 0)),
        out_shape=jax.ShapeDtypeStruct((T, W), BF16),
        scratch_shapes=[pltpu.VMEM((RET_HEADS, RET_DK, RET_DV), F32)],
        compiler_params=_cparams("parallel", "arbitrary"),
    )(proj, proj, proj, cos_t, sin_t, norm_w)


def _merge_kernel(x_ref, oa_ref, ob0_ref, ob1_ref, ob2_ref, l0_ref, l1_ref, l2_ref, oc_ref, od_ref,
                  ga_ref, gb_ref, gc_ref, gd_ref, wa_ref, wb_ref, wc_ref, wd_ref, wo_ref, nw_ref, out_ref):
    def slabs(ref):
        return jnp.concatenate([ref[t] for t in range(ref.shape[0])], axis=1)

    l0, l1, l2 = slabs(l0_ref), slabs(l1_ref), slabs(l2_ref)
    m = jnp.maximum(jnp.maximum(l0, l1), l2)
    e0, e1, e2 = jnp.exp(l0 - m), jnp.exp(l1 - m), jnp.exp(l2 - m)
    inv = 1.0 / (e0 + e1 + e2)
    ob = jnp.concatenate([slabs(ob0_ref) * (e0 * inv), slabs(ob1_ref) * (e1 * inv), slabs(ob2_ref) * (e2 * inv)],
                         axis=1).astype(BF16)
    y = _sigmoid(ga_ref[...].astype(F32)) * _dot(oa_ref[...], wa_ref[...])
    y = y + _sigmoid(gb_ref[...].astype(F32)) * _dot(ob, wb_ref[...])
    y = y + _sigmoid(gc_ref[...].astype(F32)) * _dot(oc_ref[...], wc_ref[...])
    y = y + _sigmoid(gd_ref[...].astype(F32)) * _dot(od_ref[...], wd_ref[...])
    mixed = _dot(y.astype(BF16), wo_ref[...])
    out_ref[...] = x_ref[...] + _rms(mixed, nw_ref[...])


def _merge(x, oa, obs, lses, oc, od, proj, wa, wb, wc, wd, wo, nw, tm):
    T, D = x.shape
    row = lambda w: pl.BlockSpec((tm, w), lambda i: (i, 0))
    gate = lambda br: pl.BlockSpec((tm, D), lambda i: (i, PM_GATES // D + br))
    full = lambda a: pl.BlockSpec(a.shape, lambda i: (0, 0))
    return pl.pallas_call(
        _merge_kernel,
        grid=(T // tm,),
        in_specs=[row(D), row(oa.shape[1])]
                 + [pl.BlockSpec((DSWA_OW // LANES, tm, LANES), lambda i: (0, i, 0))] * 6
                 + [row(oc.shape[1]), row(od.shape[1])]
                 + [gate(0), gate(1), gate(2), gate(3)] + [full(wa), full(wb), full(wc), full(wd), full(wo), full(nw)],
        out_specs=row(D),
        out_shape=jax.ShapeDtypeStruct((T, D), F32),
        compiler_params=_cparams("parallel"),
    )(x, oa, *obs, *lses, oc, od, proj, proj, proj, proj, wa, wb, wc, wd, wo, nw)


def _mlp_kernel(x_ref, n1_ref, w1_ref, w2_ref, n2_ref, out_ref, h_ref, acc_ref):
    f = pl.program_id(1)

    @pl.when(f == 0)
    def _():
        h_ref[...] = _rms(x_ref[...], n1_ref[...]).astype(BF16)
        acc_ref[...] = jnp.zeros_like(acc_ref)

    hid = jnp.maximum(_dot(h_ref[...], w1_ref[...]), 0.0)
    acc_ref[...] += _dot((hid * hid).astype(BF16), w2_ref[...])

    @pl.when(f == pl.num_programs(1) - 1)
    def _():
        out_ref[...] = x_ref[...] + _rms(acc_ref[...], n2_ref[...])


def _mlp(x, n1, w1, w2, n2, tm, tf):
    T, D = x.shape
    F = w1.shape[1]
    return pl.pallas_call(
        _mlp_kernel,
        grid=(T // tm, F // tf),
        in_specs=[pl.BlockSpec((tm, D), lambda i, f: (i, 0)),
                  pl.BlockSpec((1, D), lambda i, f: (0, 0)),
                  pl.BlockSpec((D, tf), lambda i, f: (0, f)),
                  pl.BlockSpec((tf, D), lambda i, f: (f, 0)),
                  pl.BlockSpec((1, D), lambda i, f: (0, 0))],
        out_specs=pl.BlockSpec((tm, D), lambda i, f: (i, 0)),
        out_shape=jax.ShapeDtypeStruct((T, D), F32),
        scratch_shapes=[pltpu.VMEM((tm, D), BF16), pltpu.VMEM((tm, D), F32)],
        compiler_params=_cparams("parallel", "arbitrary"),
    )(x, n1, w1, w2, n2)


def _prep_w_in(w_in):
    sec = [w_in[:, _COL_OFF[i]:_COL_OFF[i + 1]] for i in range(len(_COL_SIZES))]
    a_qkv, a_z, a_a, a_b, b_qkv, c_qkv, d_qk, d_v, d_g, gates = sec
    sbw = SB_HEADS * SB_DH
    c_qkv = jnp.concatenate([c_qkv[:, :sbw] * SB_DH ** -0.5, c_qkv[:, sbw:]], axis=1)
    rw = RET_HEADS * RET_DK
    d_qk = jnp.concatenate([d_qk[:, :rw], d_qk[:, rw:] * RET_DK ** -0.5], axis=1)
    w_main = jnp.concatenate([a_qkv, a_z, c_qkv, d_qk, d_v, d_g, gates], axis=1).astype(BF16)
    bw = DSWA_HEADS * DSWA_DH
    gw = DSWA_HPG * DSWA_DH
    K = w_in.shape[0]
    groups = []
    for g in range(len(DSWA_GROUPS)):
        q = b_qkv[:, g * gw:(g + 1) * gw] * DSWA_DH ** -0.5
        k = b_qkv[:, bw + g * gw: bw + (g + 1) * gw]
        v = b_qkv[:, 2 * bw + g * gw: 2 * bw + (g + 1) * gw]
        groups += [q, k, v, jnp.zeros((K, DSWA_GW - 3 * gw), F32)]
    w_b = jnp.concatenate(groups + [a_a, a_b, jnp.zeros((K, LANES - 2 * GDN_HEADS), F32)], axis=1).astype(BF16)
    return w_main, w_b


def _rope_tables(seq):
    pos = jnp.arange(seq, dtype=jnp.int32).astype(F32)[:, None]
    inv_b = DSWA_ROPE_THETA ** (-jnp.arange(0, DSWA_ROPE_DIM, 2, dtype=F32) / DSWA_ROPE_DIM)
    ang = pos * inv_b[None, :]
    cb, sb = jnp.cos(ang), jnp.sin(ang)
    rest = DSWA_DH - DSWA_ROPE_DIM
    cos_b = jnp.tile(jnp.concatenate([cb, cb, jnp.ones((seq, rest), F32)], axis=1), (1, LANES // DSWA_DH))
    sin_b = jnp.tile(jnp.concatenate([-sb, sb, jnp.zeros((seq, rest), F32)], axis=1), (1, LANES // DSWA_DH))
    inv_r = RET_THETA ** (-jnp.linspace(0.0, 1.0, RET_DK // 2, dtype=F32))
    ang = pos * inv_r[None, :]
    cr, sr = jnp.cos(ang), jnp.sin(ang)
    cos_r = jnp.tile(jnp.concatenate([cr, cr], axis=1), (1, LANES // RET_DK))
    sin_r = jnp.tile(jnp.concatenate([-sr, sr], axis=1), (1, LANES // RET_DK))
    return cos_b, sin_b, cos_r, sin_r


def _lane_vec(v):
    return jnp.zeros((1, LANES), F32).at[0, :v.shape[0]].set(v.astype(F32))


def _layer(x, batch, seq, tabs, n_pre_mix, n_post_mix, n_pre_mlp, n_post_mlp, w_in, conv_w, a_log, dt_bias,
           gdn_norm, ret_norm, w_br_a, w_br_b, w_br_c, w_br_d, w_o, w_mlp_in, w_mlp_out):
    T = x.shape[0]
    cos_b, sin_b, cos_r, sin_r = tabs
    w_main, w_b = _prep_w_in(w_in)
    gain = n_pre_mix.reshape(1, -1)
    proj = _norm_matmul(x, gain, w_main, tm=min(1024, T), tn=1536)
    g0, g1, g2, ab = _dswa_proj(x, gain, w_b, cos_b, sin_b, seq, tm=min(512, seq))

    cw = jnp.zeros((8, conv_w.shape[1]), F32).at[:GDN_CONV].set(conv_w)
    oa = _gdn(proj, ab, cw, _lane_vec(a_log), _lane_vec(dt_bias), gdn_norm.reshape(1, -1), batch, seq,
              tc=min(256, seq))
    obs, lses = [], []
    for grp, (_, dil) in zip((g0, g1, g2), DSWA_GROUPS):
        o, lse = _dswa(grp, batch, seq, dil)
        obs.append(o)
        lses.append(lse)
    oc = _sb(proj, batch, seq)
    od = _ret(proj, cos_r, sin_r, ret_norm.reshape(1, -1), batch, seq, tc=min(512, seq))

    gw = DSWA_HPG * DSWA_DH
    wb = jnp.concatenate(
        [jnp.concatenate([w_br_b[g * gw:(g + 1) * gw], jnp.zeros((DSWA_OW - gw, D_MODEL), F32)], axis=0)
         for g in range(len(DSWA_GROUPS))], axis=0).astype(BF16)
    x1 = _merge(x, oa, obs, lses, oc, od, proj, w_br_a.astype(BF16), wb, w_br_c.astype(BF16),
                w_br_d.astype(BF16), w_o.astype(BF16), n_post_mix.reshape(1, -1), tm=min(512, T))
    return _mlp(x1, n_pre_mlp.reshape(1, -1), w_mlp_in.astype(BF16), w_mlp_out.astype(BF16),
                n_post_mlp.reshape(1, -1), tm=min(1024, T), tf=1024)


def kernel(x, norm_pre_mix, norm_post_mix, norm_pre_mlp, norm_post_mlp, w_in, conv_w, a_log, dt_bias, gdn_norm,
           ret_norm, w_br_a, w_br_b, w_br_c, w_br_d, w_o, w_mlp_in, w_mlp_out):
    batch, seq, d = x.shape
    tabs = _rope_tables(seq)
    h = x.reshape(batch * seq, d)
    for l in range(norm_pre_mix.shape[0]):
        h = _layer(h, batch, seq, tabs, norm_pre_mix[l], norm_post_mix[l], norm_pre_mlp[l], norm_post_mlp[l],
                   w_in[l], conv_w[l], a_log[l], dt_bias[l], gdn_norm[l], ret_norm[l], w_br_a[l], w_br_b[l],
                   w_br_c[l], w_br_d[l], w_o[l], w_mlp_in[l], w_mlp_out[l])
    return h.reshape(batch, seq, d)
```

```python
import functools
import math

import numpy as np
import jax
import jax.numpy as jnp
from jax import lax
from jax.experimental import pallas as pl
from jax.experimental.pallas import tpu as pltpu

F32 = jnp.float32
BF16 = jnp.bfloat16

D_MODEL = 1024
N_LAYERS = 2
GDN_HEADS, GDN_DK, GDN_DV, GDN_CONV, GDN_CHUNK = 4, 128, 128, 4, 64
DSWA_GROUPS = ((128, 1), (512, 4), (2048, 16))
DSWA_HPG, DSWA_DH, DSWA_BLOCK = 3, 64, 128
DSWA_HEADS = DSWA_HPG * len(DSWA_GROUPS)
DSWA_ROPE_THETA, DSWA_ROPE_DIM = 500000.0, DSWA_DH // 4
SB_HEADS, SB_DH, SB_BLOCK = 8, 64, 128
SB_LOG_ZERO = -110.0
SB_LOG_DEAD = -1e30
SB_QBLOCKS_PER_STEP = 16
DSWA_TOKENS_PER_STEP = 2048
DSWA_CHAIN_GROUP = 12
RET_HEADS, RET_DK, RET_DV, RET_THETA = 4, 64, 128, 10000.0
RET_CHUNK = 128
D_FF = 4 * D_MODEL
NORM_EPS = 1e-6
L2_EPS = 1e-6

_COL_SIZES = (1536, 512, 4, 4, 1728, 1536, 512, 512, 512, 4096)
_COL_OFF = np.concatenate([[0], np.cumsum(_COL_SIZES)])

PM_A_QKV, PM_A_Z, PM_C, PM_D_QK, PM_D_V, PM_D_G, PM_GATES, PM_WIDTH = 0, 1536, 2048, 3584, 4096, 4608, 5120, 9216
DSWA_GW = 640
DSWA_OW = 256

VMEM_LIMIT = 48 * 1024 * 1024
LANES = 128

_RET_LOG_GAMMA = [float(np.log1p(-np.exp2(np.float32(-5.0 - h))).astype(np.float32)) for h in range(RET_HEADS)]


def _cparams(*sem):
    return pltpu.CompilerParams(dimension_semantics=sem, vmem_limit_bytes=VMEM_LIMIT)


def _sigmoid(x):
    return 1.0 / (1.0 + jnp.exp(-x))


def _softplus(x):
    return jnp.maximum(x, 0.0) + jnp.log(1.0 + jnp.exp(-jnp.abs(x)))


def _dot(a, b):
    return jnp.dot(a, b, preferred_element_type=F32)


def _dot_nt(a, b):
    return lax.dot_general(a, b, (((1,), (1,)), ((), ())), preferred_element_type=F32)


def _split3(x):
    hi = x.astype(BF16)
    r = x - hi.astype(F32)
    mid = r.astype(BF16)
    lo = (r - mid.astype(F32)).astype(BF16)
    return hi, mid, lo


def _dot_exact_lhs(a_bf16, x):
    hi, mid, lo = _split3(x)
    return _dot(a_bf16, hi) + (_dot(a_bf16, mid) + _dot(a_bf16, lo))


def _dot_f32(a, b):
    ah, am, al = _split3(a)
    bh, bm, bl = _split3(b)
    return (_dot(ah, bh) + (_dot(ah, bm) + _dot(am, bh))
            + (_dot(am, bm) + _dot(ah, bl) + _dot(al, bh)))


def _rms(x, w):
    ms = jnp.mean(x * x, axis=-1, keepdims=True)
    return x * lax.rsqrt(ms + NORM_EPS) * w


def _norm_matmul_kernel(x_ref, g_ref, w_ref, o_ref, h_ref):
    @pl.when(pl.program_id(1) == 0)
    def _():
        h_ref[...] = _rms(x_ref[...], g_ref[...]).astype(BF16)

    o_ref[...] = _dot(h_ref[...], w_ref[...]).astype(o_ref.dtype)


def _norm_matmul(x, gain, w, tm, tn):
    T, K = x.shape
    N = w.shape[1]
    return pl.pallas_call(
        _norm_matmul_kernel,
        grid=(T // tm, N // tn),
        in_specs=[pl.BlockSpec((tm, K), lambda i, j: (i, 0)),
                  pl.BlockSpec((1, K), lambda i, j: (0, 0)),
                  pl.BlockSpec((K, tn), lambda i, j: (0, j))],
        out_specs=pl.BlockSpec((tm, tn), lambda i, j: (i, j)),
        out_shape=jax.ShapeDtypeStruct((T, N), BF16),
        scratch_shapes=[pltpu.VMEM((tm, K), BF16)],
        compiler_params=_cparams("parallel", "arbitrary"),
    )(x, gain, w)


def _dswa_proj_kernel(x_ref, g_ref, w_ref, cos_ref, sin_ref, o0_ref, o1_ref, o2_ref, oab_ref):
    h = _rms(x_ref[...], g_ref[...]).astype(BF16)
    p = _dot(h, w_ref[...])
    cos = cos_ref[...]
    sin = sin_ref[...]
    lane = lax.broadcasted_iota(jnp.int32, cos.shape, 1)
    first = (lane & (DSWA_DH - 1)) < (DSWA_ROPE_DIM // 2)
    for g, o_ref in enumerate((o0_ref, o1_ref, o2_ref)):
        base = g * DSWA_GW
        for t in range(3):
            xt = p[:, base + t * LANES: base + (t + 1) * LANES]
            sw = jnp.where(first, pltpu.roll(xt, LANES - DSWA_ROPE_DIM // 2, 1),
                           pltpu.roll(xt, DSWA_ROPE_DIM // 2, 1))
            o_ref[t] = xt * cos + sw * sin
        for t in range(3, DSWA_GW // LANES):
            o_ref[t] = p[:, base + t * LANES: base + (t + 1) * LANES]
    oab_ref[...] = p[:, 3 * DSWA_GW:]


def _dswa_proj(x, gain, w, cos_t, sin_t, seq, tm):
    T, K = x.shape
    N = w.shape[1]
    n_s = seq // tm
    grp = jax.ShapeDtypeStruct((DSWA_GW // LANES, T, LANES), F32)
    return pl.pallas_call(
        _dswa_proj_kernel,
        grid=(T // tm,),
        in_specs=[pl.BlockSpec((tm, K), lambda i: (i, 0)),
                  pl.BlockSpec((1, K), lambda i: (0, 0)),
                  pl.BlockSpec((K, N), lambda i: (0, 0)),
                  pl.BlockSpec((tm, LANES), lambda i: (i % n_s, 0)),
                  pl.BlockSpec((tm, LANES), lambda i: (i % n_s, 0))],
        out_specs=[pl.BlockSpec((DSWA_GW // LANES, tm, LANES), lambda i: (0, i, 0))] * 3
                  + [pl.BlockSpec((tm, LANES), lambda i: (i, 0))],
        out_shape=[grp, grp, grp, jax.ShapeDtypeStruct((T, LANES), F32)],
        compiler_params=_cparams("parallel"),
    )(x, gain, w, cos_t, sin_t)


def _per_head_matmul(xs, p_cat, diag_mask):
    C = GDN_CHUNK
    ph, pl_ = [jnp.concatenate([t] * GDN_HEADS, axis=0) * diag_mask for t in _split2(p_cat)]
    parts = [_split2(x) for x in xs]
    his = [hi for hi, _ in parts]
    by_hi = _dot(jnp.concatenate(his + [lo for _, lo in parts], axis=0), ph)
    by_lo = _dot(jnp.concatenate(his, axis=0), pl_)
    n = len(xs)
    return [by_hi[i * C:(i + 1) * C] + (by_hi[(n + i) * C:(n + i + 1) * C] + by_lo[i * C:(i + 1) * C])
            for i in range(n)]


def _unit_lower_inverse(lows, eye_cat, diag_mask):
    xs = [eye_cat - low for low in lows]
    ps = [_per_head_matmul([low], low, diag_mask)[0] for low in lows]
    yield
    n = 4
    while n < GDN_CHUNK:
        both = [_per_head_matmul([x, p], p, diag_mask) for x, p in zip(xs, ps)]
        xs = [x + xp for x, (xp, _) in zip(xs, both)]
        ps = [pp for _, pp in both]
        n *= 2
        yield
    return [x + _per_head_matmul([x], p, diag_mask)[0] for x, p in zip(xs, ps)]


def _interleave(*gens):
    results = [None] * len(gens)
    live = list(range(len(gens)))
    while live:
        for i in list(live):
            try:
                next(gens[i])
            except StopIteration as stop:
                results[i] = stop.value
                live.remove(i)
    return results


def _gdn_kernel(qkv_ref, z_ref, ab_ref, cw_ref, alog_ref, dtb_ref, nw_ref, o_ref,
                carry_ref, q_s, k_s, v_s, state_ref, uw_s, attn_s, qd_s, kdt_s, egl_s):
    tc = qkv_ref.shape[0]
    C = GDN_CHUNK
    H = GDN_HEADS
    HD = GDN_HEADS * GDN_DK
    step = pl.program_id(1)
    n_tiles = pl.num_programs(1) - 1
    n_chunks = tc // C
    heads = range(H)
    chunks = range(n_chunks)
    ch = [(c, h) for c in chunks for h in heads]
    hsl = [slice(h * GDN_DK, (h + 1) * GDN_DK) for h in heads]
    nw = nw_ref[...]

    @pl.when(step == 0)
    def _():
        carry_ref[0:8, :] = jnp.zeros((8, 3 * HD), F32)
        for ref in (state_ref, uw_s, attn_s, qd_s, kdt_s, egl_s):
            ref[...] = jnp.zeros_like(ref)

    def recurrence():
        st = [state_ref[h] for h in heads]
        uw = {(c, h): uw_s[c * H + h] for c, h in ch}
        attn = {(c, h): attn_s[c * H + h] for c, h in ch}
        q_dec = {(c, h): qd_s[c * H + h] for c, h in ch}
        k_dec_t = {(c, h): kdt_s[c * H + h] for c, h in ch}
        egl = {(c, h): egl_s[c * H + h][0:1, :] for c, h in ch}
        gate = {(c, h): z_ref[c * C:(c + 1) * C, hsl[h]].astype(F32) for c, h in ch}
        yield
        outs = {}
        for c in chunks:
            stb = [st[h].astype(BF16) for h in heads]
            w_st = [_dot(uw[c, h][:, GDN_DV:].astype(BF16), stb[h]) for h in heads]
            q_st = [_dot(q_dec[c, h], stb[h]) for h in heads]
            yield
            vnb = [(uw[c, h][:, :GDN_DV] - w_st[h]).astype(BF16) for h in heads]
            o = [q_st[h] + _dot(attn[c, h], vnb[h]) for h in heads]
            upd = [_dot(k_dec_t[c, h], vnb[h]) for h in heads]
            yield
            for h in heads:
                st[h] = st[h] * egl[c, h] + upd[h]
                zz = gate[c, h]
                outs[c, h] = _rms(o[h], nw) * (zz * _sigmoid(zz))
            yield
        return outs, st

    def store_recurrence(outs, st):
        for c, h in ch:
            o_ref[c * C:(c + 1) * C, hsl[h]] = outs[c, h].astype(o_ref.dtype)
        for h in heads:
            state_ref[h] = st[h]

    @pl.when(step == n_tiles)
    def _():
        store_recurrence(*_interleave(recurrence())[0])

    @pl.when(step < n_tiles)
    def _():
        (outs, st), pre = _interleave(
            recurrence(), _gdn_prepare(qkv_ref, ab_ref, cw_ref, alog_ref, dtb_ref, carry_ref, q_s, k_s, v_s))
        store_recurrence(outs, st)
        for c, h in ch:
            uw_s[c * H + h] = pre["uw"][c, h]
            attn_s[c * H + h] = pre["attn"][c, h]
            qd_s[c * H + h] = pre["q_dec"][c, h]
            kdt_s[c * H + h] = pre["k_dec_t"][c, h]
            egl_s[c * H + h] = jnp.broadcast_to(pre["egl"][c, h], (8, GDN_DK))


def _gdn_prepare(qkv_ref, ab_ref, cw_ref, alog_ref, dtb_ref, carry_ref, q_s, k_s, v_s):
    tc = qkv_ref.shape[0]
    C = GDN_CHUNK
    H = GDN_HEADS
    HD = GDN_HEADS * GDN_DK
    carry_ref[8:, :] = qkv_ref[...].astype(F32)
    for grp, dst in enumerate((q_s, k_s, v_s)):
        cols = slice(grp * HD, (grp + 1) * HD)
        y = carry_ref[8:, cols] * cw_ref[GDN_CONV - 1:GDN_CONV, cols]
        for j in range(1, GDN_CONV):
            y = y + carry_ref[8 - j:8 - j + tc, cols] * cw_ref[GDN_CONV - 1 - j:GDN_CONV - j, cols]
        y = y * _sigmoid(y)
        if grp == 2:
            dst[...] = y
        else:
            for h in range(GDN_HEADS):
                yh = y[:, h * GDN_DK:(h + 1) * GDN_DK]
                yh = yh * lax.rsqrt(jnp.sum(yh * yh, axis=-1, keepdims=True) + L2_EPS)
                if grp == 0:
                    yh = yh * (GDN_DK ** -0.5)
                dst[:, h * GDN_DK:(h + 1) * GDN_DK] = yh
        yield
    carry_ref[0:8, :] = carry_ref[tc:tc + 8, :]

    ri = lax.broadcasted_iota(jnp.int32, (C, C), 0)
    ci = lax.broadcasted_iota(jnp.int32, (C, C), 1)
    incl = ri >= ci
    strict = ri > ci
    a_low = jnp.where(incl, 1.0, 0.0).astype(BF16)
    eye_cat = jnp.concatenate([jnp.where(ri == ci, 1.0, 0.0).astype(F32)] * H, axis=1)
    rb = lax.broadcasted_iota(jnp.int32, (H * C, H * C), 0) // C
    cb = lax.broadcasted_iota(jnp.int32, (H * C, H * C), 1) // C
    diag_mask = jnp.where(rb == cb, 1.0, 0.0).astype(BF16)
    neg_exp_alog = -jnp.exp(alog_ref[...])
    dtb = dtb_ref[...]
    n_chunks = tc // C
    heads = range(H)
    hsl = [slice(h * GDN_DK, (h + 1) * GDN_DK) for h in heads]

    chunks = range(n_chunks)
    ch = [(c, h) for c in chunks for h in heads]
    rows = [slice(c * C, (c + 1) * C) for c in chunks]
    ab = [ab_ref[rows[c], :] for c in chunks]
    gv = [neg_exp_alog * _softplus(ab[c] + dtb) for c in chunks]
    bv = [_sigmoid(ab[c]) for c in chunks]
    gc_all = [_dot_exact_lhs(a_low, gv[c]) for c in chunks]
    gc_t = [gc_all[c].T for c in chunks]
    yield
    q = {(c, h): q_s[rows[c], hsl[h]] for c, h in ch}
    k = {(c, h): k_s[rows[c], hsl[h]] for c, h in ch}
    v = {(c, h): v_s[rows[c], hsl[h]] for c, h in ch}
    gc = {(c, h): jnp.broadcast_to(gc_all[c][:, h:h + 1], (C, GDN_DK)) for c, h in ch}
    beta = {(c, h): jnp.broadcast_to(bv[c][:, H + h:H + h + 1], (C, GDN_DK)) for c, h in ch}
    decay = {(c, h): jnp.exp(jnp.where(incl, gc[c, h][:, :C] - gc_t[c][h:h + 1, :], -jnp.inf)) for c, h in ch}
    egc = {x: jnp.exp(gc[x]) for x in ch}
    gl = {x: gc[x][C - 1:C, :] for x in ch}
    kb = {x: k[x] * beta[x] for x in ch}
    kbf = {x: k[x].astype(BF16) for x in ch}
    kk = {x: _dot_nt(kb[x].astype(BF16), kbf[x]) for x in ch}
    low_cat = [jnp.concatenate([jnp.where(strict, kk[c, h] * decay[c, h], 0.0) for h in heads], axis=1)
               for c in chunks]
    yield
    t_cat = [t.astype(BF16) for t in (yield from _unit_lower_inverse(low_cat, eye_cat, diag_mask))]
    uw = {(c, h): _dot(t_cat[c][:, h * C:(h + 1) * C],
                       jnp.concatenate([v[c, h] * beta[c, h], kb[c, h] * egc[c, h]], axis=1).astype(BF16))
          for c, h in ch}
    yield
    attn = {x: (_dot_nt(q[x].astype(BF16), kbf[x]) * decay[x]).astype(BF16) for x in ch}
    q_dec = {x: (q[x] * egc[x]).astype(BF16) for x in ch}
    k_dec_t = {x: (k[x] * jnp.exp(gl[x] - gc[x])).T.astype(BF16) for x in ch}
    egl = {x: jnp.exp(gl[x]) for x in ch}
    return dict(uw=uw, attn=attn, q_dec=q_dec, k_dec_t=k_dec_t, egl=egl)


def _gdn(proj, ab, conv_w, alog_v, dtb_v, norm_w, batch, seq, tc):
    T = proj.shape[0]
    n_s = seq // tc
    HD = GDN_HEADS * GDN_DK
    n_pairs = (tc // GDN_CHUNK) * GDN_HEADS
    cur = lambda b, s: b * n_s + jnp.minimum(s, n_s - 1)
    lag = lambda b, s: b * n_s + jnp.maximum(s - 1, 0)
    return pl.pallas_call(
        _gdn_kernel,
        grid=(batch, n_s + 1),
        in_specs=[pl.BlockSpec((tc, 3 * HD), lambda b, s: (cur(b, s), PM_A_QKV // (3 * HD))),
                  pl.BlockSpec((tc, HD), lambda b, s: (lag(b, s), PM_A_Z // HD)),
                  pl.BlockSpec((tc, LANES), lambda b, s: (cur(b, s), 0)),
                  pl.BlockSpec((8, 3 * HD), lambda b, s: (0, 0)),
                  pl.BlockSpec((1, LANES), lambda b, s: (0, 0)),
                  pl.BlockSpec((1, LANES), lambda b, s: (0, 0)),
                  pl.BlockSpec((1, GDN_DV), lambda b, s: (0, 0))],
        out_specs=pl.BlockSpec((tc, HD), lambda b, s: (lag(b, s), 0)),
        out_shape=jax.ShapeDtypeStruct((T, HD), BF16),
        scratch_shapes=[pltpu.VMEM((tc + 8, 3 * HD), F32),
                        pltpu.VMEM((tc, HD), F32), pltpu.VMEM((tc, HD), F32), pltpu.VMEM((tc, HD), F32),
                        pltpu.VMEM((GDN_HEADS, GDN_DK, GDN_DV), F32),
                        pltpu.VMEM((n_pairs, GDN_CHUNK, 2 * GDN_DK), F32),
                        pltpu.VMEM((n_pairs, GDN_CHUNK, GDN_CHUNK), BF16),
                        pltpu.VMEM((n_pairs, GDN_CHUNK, GDN_DK), BF16),
                        pltpu.VMEM((n_pairs, GDN_DK, GDN_CHUNK), BF16),
                        pltpu.VMEM((n_pairs, 8, GDN_DK), F32)],
        compiler_params=_cparams("parallel", "arbitrary"),
    )(proj, proj, ab, conv_w, alog_v, dtb_v, norm_w)


def _dswa_kernel(cur_ref, prev_ref, o_ref, lse_ref, *, dil):
    BL = DSWA_BLOCK
    n_blk = cur_ref.shape[1] // (BL * dil)
    n = pl.program_id(1)

    def class_rows(ref, r, blk0, n_blocks):
        start, size = r + blk0 * BL * dil, n_blocks * BL
        rows = pl.ds(start, size, stride=dil) if dil > 1 else slice(start, start + size)
        return rows, jnp.concatenate([ref[t, rows, :] for t in range(ref.shape[0])], axis=1)

    ii = lax.broadcasted_iota(jnp.int32, (BL, 2 * BL), 0)
    jj = lax.broadcasted_iota(jnp.int32, (BL, 2 * BL), 1)
    band = (jj >= ii) & (jj <= ii + BL)
    first_valid = jnp.where(n > 0, 0, BL)
    band_first = band & (jj >= first_valid)
    kw = DSWA_HPG * DSWA_DH
    pad = jnp.zeros((BL, DSWA_OW - kw), F32)
    hsl = [slice(h * DSWA_DH, (h + 1) * DSWA_DH) for h in range(DSWA_HPG)]

    mc = min(n_blk, DSWA_CHAIN_GROUP // DSWA_HPG)
    units = [(r, m0) for r in range(dil) for m0 in range(0, n_blk, mc)]
    per_group = max(1, DSWA_CHAIN_GROUP // (mc * DSWA_HPG))
    for u0 in range(0, len(units), per_group):
        group = units[u0:u0 + per_group]
        rows, cur, k_all, v_all = {}, {}, {}, {}
        for u in group:
            r, m0 = u
            rows[u], cur[u] = class_rows(cur_ref, r, m0, mc)
            before = class_rows(prev_ref, r, 0, 1)[1] if m0 == 0 else class_rows(cur_ref, r, m0 - 1, 1)[1]
            k_all[u] = jnp.concatenate([before[:, kw:2 * kw], cur[u][:, kw:2 * kw]], axis=0).astype(BF16)
            v_all[u] = jnp.concatenate([before[:, 2 * kw:3 * kw], cur[u][:, 2 * kw:3 * kw]], axis=0).astype(BF16)
        chains = [(u, m, h) for u in group for m in range(mc) for h in range(DSWA_HPG)]
        keys = {m: slice(m * BL, (m + 2) * BL) for m in range(mc)}
        q = {(u, m, h): cur[u][m * BL:(m + 1) * BL, hsl[h]].astype(BF16) for u, m, h in chains}
        s = {(u, m, h): jnp.where(band_first if (u[1] == 0 and m == 0) else band,
                                  _dot_nt(q[u, m, h], k_all[u][keys[m], hsl[h]]), -jnp.inf) for u, m, h in chains}
        mx = {x: jnp.max(s[x], axis=-1, keepdims=True) for x in chains}
        p = {x: jnp.exp(s[x] - mx[x]) for x in chains}
        l = {x: jnp.sum(p[x], axis=-1, keepdims=True) for x in chains}
        o = {(u, m, h): _dot(p[u, m, h].astype(BF16), v_all[u][keys[m], hsl[h]]) * (1.0 / l[u, m, h])
             for u, m, h in chains}
        lse = {x: jnp.broadcast_to(mx[x] + jnp.log(l[x]), (BL, DSWA_DH)) for x in chains}
        for u in group:
            o_u = jnp.concatenate([jnp.concatenate([o[u, m, h] for h in range(DSWA_HPG)] + [pad], axis=1)
                                   for m in range(mc)], axis=0)
            lse_u = jnp.concatenate([jnp.concatenate([lse[u, m, h] for h in range(DSWA_HPG)] + [pad], axis=1)
                                     for m in range(mc)], axis=0)
            for t in range(DSWA_OW // LANES):
                o_ref[t, rows[u], :] = o_u[:, t * LANES:(t + 1) * LANES]
                lse_ref[t, rows[u], :] = lse_u[:, t * LANES:(t + 1) * LANES]


def _dswa(grp, batch, seq, dil):
    n_in, T, _ = grp.shape
    n_out = DSWA_OW // LANES
    span = DSWA_BLOCK * dil
    n_blk = max(1, min(DSWA_TOKENS_PER_STEP, seq) // span)
    step = n_blk * span
    n_steps = seq // step
    out = jax.ShapeDtypeStruct((n_out, T, LANES), F32)
    return pl.pallas_call(
        functools.partial(_dswa_kernel, dil=dil),
        grid=(batch, n_steps),
        in_specs=[pl.BlockSpec((n_in, step, LANES), lambda b, n: (0, b * n_steps + n, 0)),
                  pl.BlockSpec((n_in, span, LANES),
                               lambda b, n: (0, b * (seq // span) + jnp.maximum(n * n_blk - 1, 0), 0))],
        out_specs=[pl.BlockSpec((n_out, step, LANES), lambda b, n: (0, b * n_steps + n, 0))] * 2,
        out_shape=[out, out],
        compiler_params=_cparams("parallel", "arbitrary"),
    )(grp, grp)


def _split2(x):
    hi = x.astype(BF16)
    lo = (x - hi.astype(F32)).astype(BF16)
    return hi, lo


def _sb_kernel(q_ref, k_ref, v_ref, o_ref, vt_s, acc_s):
    BLK, DH = SB_BLOCK, SB_DH
    W = 2 * BLK
    G = q_ref.shape[0] // BLK
    n_kblocks = k_ref.shape[0] // BLK
    step_id = pl.program_id(2)
    i0 = step_id * G

    @pl.when(step_id == 0)
    def _():
        for j in range(n_kblocks):
            vt_s[j] = v_ref[j * BLK:(j + 1) * BLK, :].astype(F32).T.astype(BF16)

    ri = lax.broadcasted_iota(jnp.int32, (BLK, W), 0)
    ci = lax.broadcasted_iota(jnp.int32, (BLK, W), 1)
    causal = ri < (ci & (BLK - 1))
    r2 = lax.broadcasted_iota(jnp.int32, (BLK, BLK), 0)
    c2 = lax.broadcasted_iota(jnp.int32, (BLK, BLK), 1)
    after = jnp.where(c2 > r2, 1.0, 0.0).astype(BF16)
    after2 = jnp.concatenate([after, after], axis=1)

    head0_feat = lax.broadcasted_iota(jnp.int32, (BLK, LANES), 1) < DH
    q_bd = []
    for g in range(G):
        qg = q_ref[g * BLK:(g + 1) * BLK, :]
        zero = jnp.zeros_like(qg)
        q_bd.append(jnp.concatenate([jnp.where(head0_feat, qg, zero), jnp.where(head0_feat, zero, qg)], axis=0))

    def visit(js, c_rows, mask):
        ks = [k_ref[pl.ds(pl.multiple_of(js[g] * BLK, BLK), BLK), :].astype(BF16) for g in range(G)]
        vts = [vt_s[js[g]] for g in range(G)]
        zs = [_dot_nt(ks[g], q_bd[g]) for g in range(G)]
        sps = [jnp.maximum(z, 0.0) + jnp.log(1.0 + jnp.exp(-jnp.abs(z))) for z in zs]
        sp_ms = sps if mask is None else [jnp.where(mask, sp, 0.0) for sp in sps]
        splits = [_split2(sp_m) for sp_m in sp_ms]
        sufs = [_dot(after2, jnp.concatenate([hi, lo], axis=0)) for hi, lo in splits]
        a_s = [jnp.exp(zs[g] - sps[g] - sufs[g] + c_rows[g]) for g in range(G)]
        if mask is not None:
            a_s = [jnp.where(mask, a, 0.0) for a in a_s]
        pvs = [_dot(vts[g], a_s[g].astype(BF16)) for g in range(G)]
        return pvs, [c_rows[g] - (sufs[g][0:1, :] + sp_ms[g][0:1, :]) for g in range(G)]

    pvs, cs = visit([i0 + g for g in range(G)], [jnp.zeros((1, W), F32)] * G, causal)
    for g in range(G):
        acc_s[g] = pvs[g]

    def cond(state):
        d, live, _ = state
        return (d < i0 + G) & live

    def body(state):
        d, _, cs = state
        js = [i0 + g - d for g in range(G)]
        c_in = [jnp.where(js[g] < 0, SB_LOG_DEAD, cs[g]) for g in range(G)]
        accs = [acc_s[g] for g in range(G)]
        pvs, new = visit([jnp.maximum(j, 0) for j in js], c_in, None)
        for g in range(G):
            acc_s[g] = accs[g] + pvs[g]
        c_max = jnp.max(functools.reduce(jnp.maximum, new))
        return d + 1, c_max > SB_LOG_ZERO, tuple(new)

    lax.while_loop(cond, body, (jnp.int32(1), jnp.bool_(True), tuple(cs)))
    for g in range(G):
        acc = acc_s[g]
        o_ref[g * BLK:(g + 1) * BLK, :] = jnp.concatenate([acc[:DH, :BLK], acc[DH:, BLK:]],
                                                          axis=0).T.astype(o_ref.dtype)


def _sb(proj, batch, seq):
    T = proj.shape[0]
    nq = seq // SB_BLOCK
    G = min(SB_QBLOCKS_PER_STEP, nq)
    n_steps = nq // G
    pairs = SB_HEADS // 2
    qo, ko, vo = PM_C // LANES, (PM_C + SB_HEADS * SB_DH) // LANES, (PM_C + 2 * SB_HEADS * SB_DH) // LANES
    return pl.pallas_call(
        _sb_kernel,
        grid=(batch, pairs, n_steps),
        in_specs=[pl.BlockSpec((G * SB_BLOCK, LANES), lambda b, p, i: (b * n_steps + i, qo + p)),
                  pl.BlockSpec((seq, LANES), lambda b, p, i: (b, ko + p)),
                  pl.BlockSpec((seq, LANES), lambda b, p, i: (b, vo + p))],
        out_specs=pl.BlockSpec((G * SB_BLOCK, LANES), lambda b, p, i: (b * n_steps + i, p)),
        out_shape=jax.ShapeDtypeStruct((T, SB_HEADS * SB_DH), BF16),
        scratch_shapes=[pltpu.VMEM((nq, LANES, SB_BLOCK), BF16),
                        pltpu.VMEM((G, LANES, 2 * SB_BLOCK), F32)],
        compiler_params=_cparams("parallel", "parallel", "arbitrary"),
    )(proj, proj, proj)


def _ret_kernel(qk_ref, v_ref, g_ref, cos_ref, sin_ref, nw_ref, o_ref, r_ref):
    tc = qk_ref.shape[0]
    C = RET_CHUNK
    half = RET_DK // 2

    @pl.when(pl.program_id(1) == 0)
    def _():
        r_ref[...] = jnp.zeros_like(r_ref)

    lane = lax.broadcasted_iota(jnp.int32, (C, LANES), 1)
    first = (lane & (RET_DK - 1)) < half
    ri = lax.broadcasted_iota(jnp.int32, (C, C), 0)
    ci = lax.broadcasted_iota(jnp.int32, (C, C), 1)
    diff = (ri - ci).astype(F32)
    causal = ri >= ci
    rowf = lax.broadcasted_iota(jnp.int32, (C, LANES), 0).astype(F32)
    nw = nw_ref[...]

    heads = range(RET_HEADS)
    chunks = range(tc // C)
    ch = [(c, h) for c in chunks for h in heads]
    rows = [slice(c * C, (c + 1) * C) for c in chunks]
    hsl = [slice(h * RET_DV, (h + 1) * RET_DV) for h in heads]
    dmat = [jnp.where(causal, jnp.exp(diff * _RET_LOG_GAMMA[h]), 0.0) for h in heads]
    xi = [jnp.exp((rowf + 1.0) * _RET_LOG_GAMMA[h]) for h in heads]
    zeta = [jnp.exp((C - 1.0 - rowf[:, :RET_DK]) * _RET_LOG_GAMMA[h]) for h in heads]

    def rope(x, c):
        sw = jnp.where(first, pltpu.roll(x, LANES - half, 1), pltpu.roll(x, half, 1))
        return x * cos_ref[rows[c], :] + sw * sin_ref[rows[c], :]

    kw = RET_HEADS * RET_DK
    qp = {(c, p): rope(qk_ref[rows[c], p * LANES:(p + 1) * LANES].astype(F32), c)
          for c in chunks for p in range(RET_HEADS // 2)}
    kp = {(c, p): rope(qk_ref[rows[c], kw + p * LANES:kw + (p + 1) * LANES].astype(F32), c)
          for c in chunks for p in range(RET_HEADS // 2)}
    q = {(c, h): qp[c, h // 2][:, (h % 2) * RET_DK:(h % 2 + 1) * RET_DK].astype(BF16) for c, h in ch}
    k = {(c, h): kp[c, h // 2][:, (h % 2) * RET_DK:(h % 2 + 1) * RET_DK] for c, h in ch}
    vb = {(c, h): v_ref[rows[c], hsl[h]].astype(BF16) for c, h in ch}
    gate = {(c, h): g_ref[rows[c], hsl[h]].astype(F32) for c, h in ch}
    intra = {(c, h): (_dot_nt(q[c, h], k[c, h].astype(BF16)) * dmat[h]).astype(BF16) for c, h in ch}
    delta = {(c, h): _dot((k[c, h] * zeta[h]).T.astype(BF16), vb[c, h]) for c, h in ch}
    r_in = {}
    for h in heads:
        r = r_ref[h]
        for c in chunks:
            r_in[c, h] = r.astype(BF16)
            r = r * math.exp(C * _RET_LOG_GAMMA[h]) + delta[c, h]
        r_ref[h] = r
    o = {(c, h): _dot(intra[c, h], vb[c, h]) + _dot(q[c, h], r_in[c, h]) * xi[h] for c, h in ch}
    for c, h in ch:
        gg = gate[c, h]
        o_ref[rows[c], hsl[h]] = (_rms(o[c, h], nw) * (gg * _sigmoid(gg))).astype(o_ref.dtype)


def _ret(proj, cos_t, sin_t, norm_w, batch, seq, tc):
    T = proj.shape[0]
    n_s = seq // tc
    W = RET_HEADS * RET_DV
    return pl.pallas_call(
        _ret_kernel,
        grid=(batch, n_s),
        in_specs=[pl.BlockSpec((tc, W), lambda b, s: (b * n_s + s, PM_D_QK // W)),
                  pl.BlockSpec((tc, W), lambda b, s: (b * n_s + s, PM_D_V // W)),
                  pl.BlockSpec((tc, W), lambda b, s: (b * n_s + s, PM_D_G // W)),
                  pl.BlockSpec((tc, LANES), lambda b, s: (s, 0)),
                  pl.BlockSpec((tc, LANES), lambda b, s: (s, 0)),
                  pl.BlockSpec((1, RET_DV), lambda b, s: (0, 0))],
        out_specs=pl.BlockSpec((tc, W), lambda b, s: (b * n_s + s, 0)),
        out_shape=jax.ShapeDtypeStruct((T, W), BF16),
        scratch_shapes=[pltpu.VMEM((RET_HEADS, RET_DK, RET_DV), F32)],
        compiler_params=_cparams("parallel", "arbitrary"),
    )(proj, proj, proj, cos_t, sin_t, norm_w)


def _merge_kernel(x_ref, oa_ref, ob0_ref, ob1_ref, ob2_ref, l0_ref, l1_ref, l2_ref, oc_ref, od_ref,
                  ga_ref, gb_ref, gc_ref, gd_ref, wa_ref, wb_ref, wc_ref, wd_ref, wo_ref, nw_ref, out_ref):
    def slabs(ref):
        return jnp.concatenate([ref[t] for t in range(ref.shape[0])], axis=1)

    l0, l1, l2 = slabs(l0_ref), slabs(l1_ref), slabs(l2_ref)
    m = jnp.maximum(jnp.maximum(l0, l1), l2)
    e0, e1, e2 = jnp.exp(l0 - m), jnp.exp(l1 - m), jnp.exp(l2 - m)
    inv = 1.0 / (e0 + e1 + e2)
    ob = jnp.concatenate([slabs(ob0_ref) * (e0 * inv), slabs(ob1_ref) * (e1 * inv), slabs(ob2_ref) * (e2 * inv)],
                         axis=1).astype(BF16)
    y = _sigmoid(ga_ref[...].astype(F32)) * _dot(oa_ref[...], wa_ref[...])
    y = y + _sigmoid(gb_ref[...].astype(F32)) * _dot(ob, wb_ref[...])
    y = y + _sigmoid(gc_ref[...].astype(F32)) * _dot(oc_ref[...], wc_ref[...])
    y = y + _sigmoid(gd_ref[...].astype(F32)) * _dot(od_ref[...], wd_ref[...])
    mixed = _dot(y.astype(BF16), wo_ref[...])
    out_ref[...] = x_ref[...] + _rms(mixed, nw_ref[...])


def _merge(x, oa, obs, lses, oc, od, proj, wa, wb, wc, wd, wo, nw, tm):
    T, D = x.shape
    row = lambda w: pl.BlockSpec((tm, w), lambda i: (i, 0))
    gate = lambda br: pl.BlockSpec((tm, D), lambda i: (i, PM_GATES // D + br))
    full = lambda a: pl.BlockSpec(a.shape, lambda i: (0, 0))
    return pl.pallas_call(
        _merge_kernel,
        grid=(T // tm,),
        in_specs=[row(D), row(oa.shape[1])]
                 + [pl.BlockSpec((DSWA_OW // LANES, tm, LANES), lambda i: (0, i, 0))] * 6
                 + [row(oc.shape[1]), row(od.shape[1])]
                 + [gate(0), gate(1), gate(2), gate(3)] + [full(wa), full(wb), full(wc), full(wd), full(wo), full(nw)],
        out_specs=row(D),
        out_shape=jax.ShapeDtypeStruct((T, D), F32),
        compiler_params=_cparams("parallel"),
    )(x, oa, *obs, *lses, oc, od, proj, proj, proj, proj, wa, wb, wc, wd, wo, nw)


def _mlp_kernel(x_ref, n1_ref, w1_ref, w2_ref, n2_ref, out_ref, h_ref, acc_ref):
    f = pl.program_id(1)

    @pl.when(f == 0)
    def _():
        h_ref[...] = _rms(x_ref[...], n1_ref[...]).astype(BF16)
        acc_ref[...] = jnp.zeros_like(acc_ref)

    hid = jnp.maximum(_dot(h_ref[...], w1_ref[...]), 0.0)
    acc_ref[...] += _dot((hid * hid).astype(BF16), w2_ref[...])

    @pl.when(f == pl.num_programs(1) - 1)
    def _():
        out_ref[...] = x_ref[...] + _rms(acc_ref[...], n2_ref[...])


def _mlp(x, n1, w1, w2, n2, tm, tf):
    T, D = x.shape
    F = w1.shape[1]
    return pl.pallas_call(
        _mlp_kernel,
        grid=(T // tm, F // tf),
        in_specs=[pl.BlockSpec((tm, D), lambda i, f: (i, 0)),
                  pl.BlockSpec((1, D), lambda i, f: (0, 0)),
                  pl.BlockSpec((D, tf), lambda i, f: (0, f)),
                  pl.BlockSpec((tf, D), lambda i, f: (f, 0)),
                  pl.BlockSpec((1, D), lambda i, f: (0, 0))],
        out_specs=pl.BlockSpec((tm, D), lambda i, f: (i, 0)),
        out_shape=jax.ShapeDtypeStruct((T, D), F32),
        scratch_shapes=[pltpu.VMEM((tm, D), BF16), pltpu.VMEM((tm, D), F32)],
        compiler_params=_cparams("parallel", "arbitrary"),
    )(x, n1, w1, w2, n2)


def _prep_w_in(w_in):
    sec = [w_in[:, _COL_OFF[i]:_COL_OFF[i + 1]] for i in range(len(_COL_SIZES))]
    a_qkv, a_z, a_a, a_b, b_qkv, c_qkv, d_qk, d_v, d_g, gates = sec
    sbw = SB_HEADS * SB_DH
    c_qkv = jnp.concatenate([c_qkv[:, :sbw] * SB_DH ** -0.5, c_qkv[:, sbw:]], axis=1)
    rw = RET_HEADS * RET_DK
    d_qk = jnp.concatenate([d_qk[:, :rw], d_qk[:, rw:] * RET_DK ** -0.5], axis=1)
    w_main = jnp.concatenate([a_qkv, a_z, c_qkv, d_qk, d_v, d_g, gates], axis=1).astype(BF16)
    bw = DSWA_HEADS * DSWA_DH
    gw = DSWA_HPG * DSWA_DH
    K = w_in.shape[0]
    groups = []
    for g in range(len(DSWA_GROUPS)):
        q = b_qkv[:, g * gw:(g + 1) * gw] * DSWA_DH ** -0.5
        k = b_qkv[:, bw + g * gw: bw + (g + 1) * gw]
        v = b_qkv[:, 2 * bw + g * gw: 2 * bw + (g + 1) * gw]
        groups += [q, k, v, jnp.zeros((K, DSWA_GW - 3 * gw), F32)]
    w_b = jnp.concatenate(groups + [a_a, a_b, jnp.zeros((K, LANES - 2 * GDN_HEADS), F32)], axis=1).astype(BF16)
    return w_main, w_b


def _rope_tables(seq):
    pos = jnp.arange(seq, dtype=jnp.int32).astype(F32)[:, None]
    inv_b = DSWA_ROPE_THETA ** (-jnp.arange(0, DSWA_ROPE_DIM, 2, dtype=F32) / DSWA_ROPE_DIM)
    ang = pos * inv_b[None, :]
    cb, sb = jnp.cos(ang), jnp.sin(ang)
    rest = DSWA_DH - DSWA_ROPE_DIM
    cos_b = jnp.tile(jnp.concatenate([cb, cb, jnp.ones((seq, rest), F32)], axis=1), (1, LANES // DSWA_DH))
    sin_b = jnp.tile(jnp.concatenate([-sb, sb, jnp.zeros((seq, rest), F32)], axis=1), (1, LANES // DSWA_DH))
    inv_r = RET_THETA ** (-jnp.linspace(0.0, 1.0, RET_DK // 2, dtype=F32))
    ang = pos * inv_r[None, :]
    cr, sr = jnp.cos(ang), jnp.sin(ang)
    cos_r = jnp.tile(jnp.concatenate([cr, cr], axis=1), (1, LANES // RET_DK))
    sin_r = jnp.tile(jnp.concatenate([-sr, sr], axis=1), (1, LANES // RET_DK))
    return cos_b, sin_b, cos_r, sin_r


def _lane_vec(v):
    return jnp.zeros((1, LANES), F32).at[0, :v.shape[0]].set(v.astype(F32))


def _layer(x, batch, seq, tabs, n_pre_mix, n_post_mix, n_pre_mlp, n_post_mlp, w_in, conv_w, a_log, dt_bias,
           gdn_norm, ret_norm, w_br_a, w_br_b, w_br_c, w_br_d, w_o, w_mlp_in, w_mlp_out):
    T = x.shape[0]
    cos_b, sin_b, cos_r, sin_r = tabs
    w_main, w_b = _prep_w_in(w_in)
    gain = n_pre_mix.reshape(1, -1)
    proj = _norm_matmul(x, gain, w_main, tm=min(1024, T), tn=3072)
    g0, g1, g2, ab = _dswa_proj(x, gain, w_b, cos_b, sin_b, seq, tm=min(1024, seq))

    cw = jnp.zeros((8, conv_w.shape[1]), F32).at[:GDN_CONV].set(conv_w)
    oa = _gdn(proj, ab, cw, _lane_vec(a_log), _lane_vec(dt_bias), gdn_norm.reshape(1, -1), batch, seq,
              tc=min(256, seq))
    obs, lses = [], []
    for grp, (_, dil) in zip((g0, g1, g2), DSWA_GROUPS):
        o, lse = _dswa(grp, batch, seq, dil)
        obs.append(o)
        lses.append(lse)
    oc = _sb(proj, batch, seq)
    od = _ret(proj, cos_r, sin_r, ret_norm.reshape(1, -1), batch, seq, tc=min(1024, seq))

    gw = DSWA_HPG * DSWA_DH
    wb = jnp.concatenate(
        [jnp.concatenate([w_br_b[g * gw:(g + 1) * gw], jnp.zeros((DSWA_OW - gw, D_MODEL), F32)], axis=0)
         for g in range(len(DSWA_GROUPS))], axis=0).astype(BF16)
    x1 = _merge(x, oa, obs, lses, oc, od, proj, w_br_a.astype(BF16), wb, w_br_c.astype(BF16),
                w_br_d.astype(BF16), w_o.astype(BF16), n_post_mix.reshape(1, -1), tm=min(512, T))
    return _mlp(x1, n_pre_mlp.reshape(1, -1), w_mlp_in.astype(BF16), w_mlp_out.astype(BF16),
                n_post_mlp.reshape(1, -1), tm=min(1024, T), tf=1024)


def kernel(x, norm_pre_mix, norm_post_mix, norm_pre_mlp, norm_post_mlp, w_in, conv_w, a_log, dt_bias, gdn_norm,
           ret_norm, w_br_a, w_br_b, w_br_c, w_br_d, w_o, w_mlp_in, w_mlp_out):
    batch, seq, d = x.shape
    tabs = _rope_tables(seq)
    h = x.reshape(batch * seq, d)
    for l in range(norm_pre_mix.shape[0]):
        h = _layer(h, batch, seq, tabs, norm_pre_mix[l], norm_post_mix[l], norm_pre_mlp[l], norm_post_mlp[l],
                   w_in[l], conv_w[l], a_log[l], dt_bias[l], gdn_norm[l], ret_norm[l], w_br_a[l], w_br_b[l],
                   w_br_c[l], w_br_d[l], w_o[l], w_mlp_in[l], w_mlp_out[l])
    return h.reshape(batch, seq, d)
```

```python
import functools
import math

import numpy as np
import jax
import jax.numpy as jnp
from jax import lax
from jax.experimental import pallas as pl
from jax.experimental.pallas import tpu as pltpu

F32 = jnp.float32
BF16 = jnp.bfloat16

D_MODEL = 1024
GDN_HEADS, GDN_DK, GDN_DV, GDN_CONV, GDN_CHUNK = 4, 128, 128, 4, 64
DSWA_GROUPS = ((128, 1), (512, 4), (2048, 16))
DSWA_HPG, DSWA_DH, DSWA_BLOCK = 3, 64, 128
DSWA_HEADS = DSWA_HPG * len(DSWA_GROUPS)
DSWA_ROPE_THETA, DSWA_ROPE_DIM = 500000.0, DSWA_DH // 4
SB_HEADS, SB_DH, SB_BLOCK = 8, 64, 128
SB_LOG_ZERO = -110.0
SB_LOG_DEAD = -1e30
SB_QBLOCKS_PER_STEP = 16
DSWA_TOKENS_PER_STEP = 2048
DSWA_CHAIN_GROUP = 12
RET_HEADS, RET_DK, RET_DV, RET_THETA = 4, 64, 128, 10000.0
RET_CHUNK = 128
D_FF = 4 * D_MODEL
NORM_EPS = 1e-6
L2_EPS = 1e-6

_COL_SIZES = (1536, 512, 4, 4, 1728, 1536, 512, 512, 512, 4096)
_COL_OFF = np.concatenate([[0], np.cumsum(_COL_SIZES)])

PM_A_QKV, PM_A_Z, PM_C, PM_D_QK, PM_D_V, PM_D_G, PM_GATES, PM_WIDTH = 0, 1536, 2048, 3584, 4096, 4608, 5120, 9216
DSWA_GW = 640
DSWA_OW = 256

V7X_VMEM_BYTES = 64 * 1024 * 1024
VMEM_LIMIT = V7X_VMEM_BYTES * 3 // 4
LANES = 128

_RET_LOG_GAMMA = [float(np.log1p(-np.exp2(np.float32(-5.0 - h))).astype(np.float32)) for h in range(RET_HEADS)]


def _cparams(*sem):
    return pltpu.CompilerParams(dimension_semantics=sem, vmem_limit_bytes=VMEM_LIMIT)


def _sigmoid(x):
    return 1.0 / (1.0 + jnp.exp(-x))


def _softplus(x):
    return jnp.maximum(x, 0.0) + jnp.log(1.0 + jnp.exp(-jnp.abs(x)))


def _dot(a, b):
    return jnp.dot(a, b, preferred_element_type=F32)


def _dot_nt(a, b):
    return lax.dot_general(a, b, (((1,), (1,)), ((), ())), preferred_element_type=F32)


def _split3(x):
    hi = x.astype(BF16)
    r = x - hi.astype(F32)
    mid = r.astype(BF16)
    lo = (r - mid.astype(F32)).astype(BF16)
    return hi, mid, lo


def _dot_exact_lhs(a_bf16, x):
    hi, mid, lo = _split3(x)
    return _dot(a_bf16, hi) + (_dot(a_bf16, mid) + _dot(a_bf16, lo))


def _split2(x):
    hi = x.astype(BF16)
    lo = (x - hi.astype(F32)).astype(BF16)
    return hi, lo


def _rms(x, w):
    ms = jnp.mean(x * x, axis=-1, keepdims=True)
    return x * lax.rsqrt(ms + NORM_EPS) * w


def _norm_matmul_kernel(x_ref, g_ref, w_ref, o_ref, h_ref):
    @pl.when(pl.program_id(1) == 0)
    def _():
        h_ref[...] = _rms(x_ref[...], g_ref[...]).astype(BF16)

    o_ref[...] = _dot(h_ref[...], w_ref[...]).astype(o_ref.dtype)


def _norm_matmul(x, gain, w, tm, tn):
    T, K = x.shape
    N = w.shape[1]
    return pl.pallas_call(
        _norm_matmul_kernel,
        grid=(T // tm, N // tn),
        in_specs=[pl.BlockSpec((tm, K), lambda i, j: (i, 0)),
                  pl.BlockSpec((1, K), lambda i, j: (0, 0)),
                  pl.BlockSpec((K, tn), lambda i, j: (0, j))],
        out_specs=pl.BlockSpec((tm, tn), lambda i, j: (i, j)),
        out_shape=jax.ShapeDtypeStruct((T, N), BF16),
        scratch_shapes=[pltpu.VMEM((tm, K), BF16)],
        compiler_params=_cparams("parallel", "arbitrary"),
    )(x, gain, w)


def _dswa_proj_kernel(x_ref, g_ref, w_ref, cos_ref, sin_ref, o0_ref, o1_ref, o2_ref, oab_ref):
    h = _rms(x_ref[...], g_ref[...]).astype(BF16)
    p = _dot(h, w_ref[...])
    cos = cos_ref[...]
    sin = sin_ref[...]
    lane = lax.broadcasted_iota(jnp.int32, cos.shape, 1)
    first = (lane & (DSWA_DH - 1)) < (DSWA_ROPE_DIM // 2)
    for g, o_ref in enumerate((o0_ref, o1_ref, o2_ref)):
        base = g * DSWA_GW
        for t in range(3):
            xt = p[:, base + t * LANES: base + (t + 1) * LANES]
            sw = jnp.where(first, pltpu.roll(xt, LANES - DSWA_ROPE_DIM // 2, 1),
                           pltpu.roll(xt, DSWA_ROPE_DIM // 2, 1))
            o_ref[t] = xt * cos + sw * sin
        for t in range(3, DSWA_GW // LANES):
            o_ref[t] = p[:, base + t * LANES: base + (t + 1) * LANES]
    oab_ref[...] = p[:, 3 * DSWA_GW:]


def _dswa_proj(x, gain, w, cos_t, sin_t, seq, tm):
    T, K = x.shape
    N = w.shape[1]
    n_s = seq // tm
    grp = jax.ShapeDtypeStruct((DSWA_GW // LANES, T, LANES), F32)
    return pl.pallas_call(
        _dswa_proj_kernel,
        grid=(T // tm,),
        in_specs=[pl.BlockSpec((tm, K), lambda i: (i, 0)),
                  pl.BlockSpec((1, K), lambda i: (0, 0)),
                  pl.BlockSpec((K, N), lambda i: (0, 0)),
                  pl.BlockSpec((tm, LANES), lambda i: (i % n_s, 0)),
                  pl.BlockSpec((tm, LANES), lambda i: (i % n_s, 0))],
        out_specs=[pl.BlockSpec((DSWA_GW // LANES, tm, LANES), lambda i: (0, i, 0))] * 3
                  + [pl.BlockSpec((tm, LANES), lambda i: (i, 0))],
        out_shape=[grp, grp, grp, jax.ShapeDtypeStruct((T, LANES), F32)],
        compiler_params=_cparams("parallel"),
    )(x, gain, w, cos_t, sin_t)


def _per_head_matmul(xs, p_cat, diag_mask):
    C = GDN_CHUNK
    ph, pl_ = [jnp.concatenate([t] * GDN_HEADS, axis=0) * diag_mask for t in _split2(p_cat)]
    parts = [_split2(x) for x in xs]
    his = [hi for hi, _ in parts]
    by_hi = _dot(jnp.concatenate(his + [lo for _, lo in parts], axis=0), ph)
    by_lo = _dot(jnp.concatenate(his, axis=0), pl_)
    n = len(xs)
    return [by_hi[i * C:(i + 1) * C] + (by_hi[(n + i) * C:(n + i + 1) * C] + by_lo[i * C:(i + 1) * C])
            for i in range(n)]


def _unit_lower_inverse(lows, eye_cat, diag_mask):
    xs = [eye_cat - low for low in lows]
    ps = [_per_head_matmul([low], low, diag_mask)[0] for low in lows]
    yield
    n = 4
    while n < GDN_CHUNK:
        both = [_per_head_matmul([x, p], p, diag_mask) for x, p in zip(xs, ps)]
        xs = [x + xp for x, (xp, _) in zip(xs, both)]
        ps = [pp for _, pp in both]
        n *= 2
        yield
    return [x + _per_head_matmul([x], p, diag_mask)[0] for x, p in zip(xs, ps)]


def _interleave(*gens):
    results = [None] * len(gens)
    live = list(range(len(gens)))
    while live:
        for i in list(live):
            try:
                next(gens[i])
            except StopIteration as stop:
                results[i] = stop.value
                live.remove(i)
    return results


def _gdn_kernel(qkv_ref, z_ref, ab_ref, cw_ref, alog_ref, dtb_ref, nw_ref, o_ref,
                carry_ref, q_s, k_s, v_s, state_ref, uw_s, attn_s, qd_s, kdt_s, egl_s):
    tc = qkv_ref.shape[0]
    C = GDN_CHUNK
    H = GDN_HEADS
    HD = GDN_HEADS * GDN_DK
    step = pl.program_id(1)
    n_tiles = pl.num_programs(1) - 1
    n_chunks = tc // C
    heads = range(H)
    chunks = range(n_chunks)
    ch = [(c, h) for c in chunks for h in heads]
    hsl = [slice(h * GDN_DK, (h + 1) * GDN_DK) for h in heads]
    nw = nw_ref[...]

    @pl.when(step == 0)
    def _():
        carry_ref[0:8, :] = jnp.zeros((8, 3 * HD), F32)
        for ref in (state_ref, uw_s, attn_s, qd_s, kdt_s, egl_s):
            ref[...] = jnp.zeros_like(ref)

    def recurrence():
        st = [state_ref[h] for h in heads]
        uw = {(c, h): uw_s[c * H + h] for c, h in ch}
        attn = {(c, h): attn_s[c * H + h] for c, h in ch}
        q_dec = {(c, h): qd_s[c * H + h] for c, h in ch}
        k_dec_t = {(c, h): kdt_s[c * H + h] for c, h in ch}
        egl = {(c, h): egl_s[c * H + h][0:1, :] for c, h in ch}
        gate = {(c, h): z_ref[c * C:(c + 1) * C, hsl[h]].astype(F32) for c, h in ch}
        yield
        outs = {}
        for c in chunks:
            stb = [st[h].astype(BF16) for h in heads]
            w_st = [_dot(uw[c, h][:, GDN_DV:].astype(BF16), stb[h]) for h in heads]
            q_st = [_dot(q_dec[c, h], stb[h]) for h in heads]
            yield
            vnb = [(uw[c, h][:, :GDN_DV] - w_st[h]).astype(BF16) for h in heads]
            o = [q_st[h] + _dot(attn[c, h], vnb[h]) for h in heads]
            upd = [_dot(k_dec_t[c, h], vnb[h]) for h in heads]
            yield
            for h in heads:
                st[h] = st[h] * egl[c, h] + upd[h]
                zz = gate[c, h]
                outs[c, h] = _rms(o[h], nw) * (zz * _sigmoid(zz))
            yield
        return outs, st

    def store_recurrence(outs, st):
        for c, h in ch:
            o_ref[c * C:(c + 1) * C, hsl[h]] = outs[c, h].astype(o_ref.dtype)
        for h in heads:
            state_ref[h] = st[h]

    @pl.when(step == n_tiles)
    def _():
        store_recurrence(*_interleave(recurrence())[0])

    @pl.when(step < n_tiles)
    def _():
        (outs, st), pre = _interleave(
            recurrence(), _gdn_prepare(qkv_ref, ab_ref, cw_ref, alog_ref, dtb_ref, carry_ref, q_s, k_s, v_s))
        store_recurrence(outs, st)
        for c, h in ch:
            uw_s[c * H + h] = pre["uw"][c, h]
            attn_s[c * H + h] = pre["attn"][c, h]
            qd_s[c * H + h] = pre["q_dec"][c, h]
            kdt_s[c * H + h] = pre["k_dec_t"][c, h]
            egl_s[c * H + h] = jnp.broadcast_to(pre["egl"][c, h], (8, GDN_DK))


def _gdn_prepare(qkv_ref, ab_ref, cw_ref, alog_ref, dtb_ref, carry_ref, q_s, k_s, v_s):
    tc = qkv_ref.shape[0]
    C = GDN_CHUNK
    H = GDN_HEADS
    HD = GDN_HEADS * GDN_DK
    carry_ref[8:, :] = qkv_ref[...].astype(F32)
    for grp, dst in enumerate((q_s, k_s, v_s)):
        cols = slice(grp * HD, (grp + 1) * HD)
        y = carry_ref[8:, cols] * cw_ref[GDN_CONV - 1:GDN_CONV, cols]
        for j in range(1, GDN_CONV):
            y = y + carry_ref[8 - j:8 - j + tc, cols] * cw_ref[GDN_CONV - 1 - j:GDN_CONV - j, cols]
        y = y * _sigmoid(y)
        if grp == 2:
            dst[...] = y
        else:
            for h in range(GDN_HEADS):
                yh = y[:, h * GDN_DK:(h + 1) * GDN_DK]
                yh = yh * lax.rsqrt(jnp.sum(yh * yh, axis=-1, keepdims=True) + L2_EPS)
                if grp == 0:
                    yh = yh * (GDN_DK ** -0.5)
                dst[:, h * GDN_DK:(h + 1) * GDN_DK] = yh
        yield
    carry_ref[0:8, :] = carry_ref[tc:tc + 8, :]

    ri = lax.broadcasted_iota(jnp.int32, (C, C), 0)
    ci = lax.broadcasted_iota(jnp.int32, (C, C), 1)
    incl = ri >= ci
    strict = ri > ci
    a_low = jnp.where(incl, 1.0, 0.0).astype(BF16)
    eye_cat = jnp.concatenate([jnp.where(ri == ci, 1.0, 0.0).astype(F32)] * H, axis=1)
    rb = lax.broadcasted_iota(jnp.int32, (H * C, H * C), 0) // C
    cb = lax.broadcasted_iota(jnp.int32, (H * C, H * C), 1) // C
    diag_mask = jnp.where(rb == cb, 1.0, 0.0).astype(BF16)
    neg_exp_alog = -jnp.exp(alog_ref[...])
    dtb = dtb_ref[...]
    n_chunks = tc // C
    heads = range(H)
    hsl = [slice(h * GDN_DK, (h + 1) * GDN_DK) for h in heads]

    chunks = range(n_chunks)
    ch = [(c, h) for c in chunks for h in heads]
    rows = [slice(c * C, (c + 1) * C) for c in chunks]
    ab = [ab_ref[rows[c], :] for c in chunks]
    gv = [neg_exp_alog * _softplus(ab[c] + dtb) for c in chunks]
    bv = [_sigmoid(ab[c]) for c in chunks]
    gc_all = [_dot_exact_lhs(a_low, gv[c]) for c in chunks]
    gc_t = [gc_all[c].T for c in chunks]
    yield
    q = {(c, h): q_s[rows[c], hsl[h]] for c, h in ch}
    k = {(c, h): k_s[rows[c], hsl[h]] for c, h in ch}
    v = {(c, h): v_s[rows[c], hsl[h]] for c, h in ch}
    gc = {(c, h): jnp.broadcast_to(gc_all[c][:, h:h + 1], (C, GDN_DK)) for c, h in ch}
    beta = {(c, h): jnp.broadcast_to(bv[c][:, H + h:H + h + 1], (C, GDN_DK)) for c, h in ch}
    decay = {(c, h): jnp.exp(jnp.where(incl, gc[c, h][:, :C] - gc_t[c][h:h + 1, :], -jnp.inf)) for c, h in ch}
    egc = {x: jnp.exp(gc[x]) for x in ch}
    gl = {x: gc[x][C - 1:C, :] for x in ch}
    kb = {x: k[x] * beta[x] for x in ch}
    kbf = {x: k[x].astype(BF16) for x in ch}
    kk = {x: _dot_nt(kb[x].astype(BF16), kbf[x]) for x in ch}
    low_cat = [jnp.concatenate([jnp.where(strict, kk[c, h] * decay[c, h], 0.0) for h in heads], axis=1)
               for c in chunks]
    yield
    t_cat = [t.astype(BF16) for t in (yield from _unit_lower_inverse(low_cat, eye_cat, diag_mask))]
    uw = {(c, h): _dot(t_cat[c][:, h * C:(h + 1) * C],
                       jnp.concatenate([v[c, h] * beta[c, h], kb[c, h] * egc[c, h]], axis=1).astype(BF16))
          for c, h in ch}
    yield
    attn = {x: (_dot_nt(q[x].astype(BF16), kbf[x]) * decay[x]).astype(BF16) for x in ch}
    q_dec = {x: (q[x] * egc[x]).astype(BF16) for x in ch}
    k_dec_t = {x: (k[x] * jnp.exp(gl[x] - gc[x])).T.astype(BF16) for x in ch}
    egl = {x: jnp.exp(gl[x]) for x in ch}
    return dict(uw=uw, attn=attn, q_dec=q_dec, k_dec_t=k_dec_t, egl=egl)


def _gdn(proj, ab, conv_w, alog_v, dtb_v, norm_w, batch, seq, tc):
    T = proj.shape[0]
    n_s = seq // tc
    HD = GDN_HEADS * GDN_DK
    n_pairs = (tc // GDN_CHUNK) * GDN_HEADS
    cur = lambda b, s: b * n_s + jnp.minimum(s, n_s - 1)
    lag = lambda b, s: b * n_s + jnp.maximum(s - 1, 0)
    return pl.pallas_call(
        _gdn_kernel,
        grid=(batch, n_s + 1),
        in_specs=[pl.BlockSpec((tc, 3 * HD), lambda b, s: (cur(b, s), PM_A_QKV // (3 * HD))),
                  pl.BlockSpec((tc, HD), lambda b, s: (lag(b, s), PM_A_Z // HD)),
                  pl.BlockSpec((tc, LANES), lambda b, s: (cur(b, s), 0)),
                  pl.BlockSpec((8, 3 * HD), lambda b, s: (0, 0)),
                  pl.BlockSpec((1, LANES), lambda b, s: (0, 0)),
                  pl.BlockSpec((1, LANES), lambda b, s: (0, 0)),
                  pl.BlockSpec((1, GDN_DV), lambda b, s: (0, 0))],
        out_specs=pl.BlockSpec((tc, HD), lambda b, s: (lag(b, s), 0)),
        out_shape=jax.ShapeDtypeStruct((T, HD), BF16),
        scratch_shapes=[pltpu.VMEM((tc + 8, 3 * HD), F32),
                        pltpu.VMEM((tc, HD), F32), pltpu.VMEM((tc, HD), F32), pltpu.VMEM((tc, HD), F32),
                        pltpu.VMEM((GDN_HEADS, GDN_DK, GDN_DV), F32),
                        pltpu.VMEM((n_pairs, GDN_CHUNK, 2 * GDN_DK), F32),
                        pltpu.VMEM((n_pairs, GDN_CHUNK, GDN_CHUNK), BF16),
                        pltpu.VMEM((n_pairs, GDN_CHUNK, GDN_DK), BF16),
                        pltpu.VMEM((n_pairs, GDN_DK, GDN_CHUNK), BF16),
                        pltpu.VMEM((n_pairs, 8, GDN_DK), F32)],
        compiler_params=_cparams("parallel", "arbitrary"),
    )(proj, proj, ab, conv_w, alog_v, dtb_v, norm_w)


def _dswa_kernel(cur_ref, prev_ref, o_ref, lse_ref, *, dil):
    BL = DSWA_BLOCK
    n_blk = cur_ref.shape[1] // (BL * dil)
    n = pl.program_id(1)

    def class_rows(ref, r, blk0, n_blocks):
        start, size = r + blk0 * BL * dil, n_blocks * BL
        rows = pl.ds(start, size, stride=dil) if dil > 1 else slice(start, start + size)
        return rows, jnp.concatenate([ref[t, rows, :] for t in range(ref.shape[0])], axis=1)

    ii = lax.broadcasted_iota(jnp.int32, (BL, 2 * BL), 0)
    jj = lax.broadcasted_iota(jnp.int32, (BL, 2 * BL), 1)
    band = (jj >= ii) & (jj <= ii + BL)
    first_valid = jnp.where(n > 0, 0, BL)
    band_first = band & (jj >= first_valid)
    kw = DSWA_HPG * DSWA_DH
    pad = jnp.zeros((BL, DSWA_OW - kw), F32)
    ones_keys = jnp.ones((2 * BL, DSWA_DH), BF16)
    hsl = [slice(h * DSWA_DH, (h + 1) * DSWA_DH) for h in range(DSWA_HPG)]

    mc = min(n_blk, DSWA_CHAIN_GROUP // DSWA_HPG)
    units = [(r, m0) for r in range(dil) for m0 in range(0, n_blk, mc)]
    per_group = max(1, DSWA_CHAIN_GROUP // (mc * DSWA_HPG))
    for u0 in range(0, len(units), per_group):
        group = units[u0:u0 + per_group]
        rows, cur, k_all, v_all = {}, {}, {}, {}
        for u in group:
            r, m0 = u
            rows[u], cur[u] = class_rows(cur_ref, r, m0, mc)
            before = class_rows(prev_ref, r, 0, 1)[1] if m0 == 0 else class_rows(cur_ref, r, m0 - 1, 1)[1]
            k_all[u] = jnp.concatenate([before[:, kw:2 * kw], cur[u][:, kw:2 * kw]], axis=0).astype(BF16)
            v_all[u] = jnp.concatenate([before[:, 2 * kw:3 * kw], cur[u][:, 2 * kw:3 * kw]], axis=0).astype(BF16)
        chains = [(u, m, h) for u in group for m in range(mc) for h in range(DSWA_HPG)]
        keys = {m: slice(m * BL, (m + 2) * BL) for m in range(mc)}
        q = {(u, m, h): cur[u][m * BL:(m + 1) * BL, hsl[h]].astype(BF16) for u, m, h in chains}
        s = {(u, m, h): jnp.where(band_first if (u[1] == 0 and m == 0) else band,
                                  _dot_nt(q[u, m, h], k_all[u][keys[m], hsl[h]]), -jnp.inf) for u, m, h in chains}
        mx = {x: jnp.max(s[x], axis=-1, keepdims=True) for x in chains}
        p = {x: jnp.exp(s[x] - mx[x]) for x in chains}
        pb = {x: p[x].astype(BF16) for x in chains}
        l = {x: _dot(pb[x], ones_keys) for x in chains}
        o = {(u, m, h): _dot(pb[u, m, h], v_all[u][keys[m], hsl[h]]) * (1.0 / l[u, m, h]) for u, m, h in chains}
        lse = {x: mx[x] + jnp.log(l[x]) for x in chains}
        for u in group:
            o_u = jnp.concatenate([jnp.concatenate([o[u, m, h] for h in range(DSWA_HPG)] + [pad], axis=1)
                                   for m in range(mc)], axis=0)
            lse_u = jnp.concatenate([jnp.concatenate([lse[u, m, h] for h in range(DSWA_HPG)] + [pad], axis=1)
                                     for m in range(mc)], axis=0)
            for t in range(DSWA_OW // LANES):
                o_ref[t, rows[u], :] = o_u[:, t * LANES:(t + 1) * LANES]
                lse_ref[t, rows[u], :] = lse_u[:, t * LANES:(t + 1) * LANES]


def _dswa(grp, batch, seq, dil):
    n_in, T, _ = grp.shape
    n_out = DSWA_OW // LANES
    span = DSWA_BLOCK * dil
    n_blk = max(1, min(DSWA_TOKENS_PER_STEP, seq) // span)
    step = n_blk * span
    n_steps = seq // step
    out = jax.ShapeDtypeStruct((n_out, T, LANES), F32)
    return pl.pallas_call(
        functools.partial(_dswa_kernel, dil=dil),
        grid=(batch, n_steps),
        in_specs=[pl.BlockSpec((n_in, step, LANES), lambda b, n: (0, b * n_steps + n, 0)),
                  pl.BlockSpec((n_in, span, LANES),
                               lambda b, n: (0, b * (seq // span) + jnp.maximum(n * n_blk - 1, 0), 0))],
        out_specs=[pl.BlockSpec((n_out, step, LANES), lambda b, n: (0, b * n_steps + n, 0))] * 2,
        out_shape=[out, out],
        compiler_params=_cparams("parallel", "arbitrary"),
    )(grp, grp)


def _sb_kernel(q_ref, k_ref, v_ref, o_ref, vt_s, acc_s):
    BLK, DH = SB_BLOCK, SB_DH
    W = 2 * BLK
    G = q_ref.shape[0] // BLK
    n_kblocks = k_ref.shape[0] // BLK
    step_id = pl.program_id(2)
    i0 = step_id * G

    @pl.when(step_id == 0)
    def _():
        for j in range(n_kblocks):
            vt_s[j] = v_ref[j * BLK:(j + 1) * BLK, :].astype(F32).T.astype(BF16)

    ri = lax.broadcasted_iota(jnp.int32, (BLK, W), 0)
    ci = lax.broadcasted_iota(jnp.int32, (BLK, W), 1)
    causal = ri < (ci & (BLK - 1))
    r2 = lax.broadcasted_iota(jnp.int32, (BLK, BLK), 0)
    c2 = lax.broadcasted_iota(jnp.int32, (BLK, BLK), 1)
    after = jnp.where(c2 > r2, 1.0, 0.0).astype(BF16)
    after2 = jnp.concatenate([after, after], axis=1)

    head0_feat = lax.broadcasted_iota(jnp.int32, (BLK, LANES), 1) < DH
    q_bd = []
    for g in range(G):
        qg = q_ref[g * BLK:(g + 1) * BLK, :]
        zero = jnp.zeros_like(qg)
        q_bd.append(jnp.concatenate([jnp.where(head0_feat, qg, zero), jnp.where(head0_feat, zero, qg)], axis=0))

    def visit(js, c_rows, mask):
        ks = [k_ref[pl.ds(pl.multiple_of(js[g] * BLK, BLK), BLK), :].astype(BF16) for g in range(G)]
        vts = [vt_s[js[g]] for g in range(G)]
        zs = [_dot_nt(ks[g], q_bd[g]) for g in range(G)]
        sps = [jnp.maximum(z, 0.0) + jnp.log(1.0 + jnp.exp(-jnp.abs(z))) for z in zs]
        sp_ms = sps if mask is None else [jnp.where(mask, sp, 0.0) for sp in sps]
        splits = [_split2(sp_m) for sp_m in sp_ms]
        sufs = [_dot(after2, jnp.concatenate([hi, lo], axis=0)) for hi, lo in splits]
        a_s = [jnp.exp(zs[g] - sps[g] - sufs[g] + c_rows[g]) for g in range(G)]
        if mask is not None:
            a_s = [jnp.where(mask, a, 0.0) for a in a_s]
        pvs = [_dot(vts[g], a_s[g].astype(BF16)) for g in range(G)]
        return pvs, [c_rows[g] - (sufs[g][0:1, :] + sp_ms[g][0:1, :]) for g in range(G)]

    pvs, cs = visit([i0 + g for g in range(G)], [jnp.zeros((1, W), F32)] * G, causal)
    for g in range(G):
        acc_s[g] = pvs[g]

    def cond(state):
        d, live, _ = state
        return (d < i0 + G) & live

    def body(state):
        d, _, cs = state
        js = [i0 + g - d for g in range(G)]
        c_in = [jnp.where(js[g] < 0, SB_LOG_DEAD, cs[g]) for g in range(G)]
        accs = [acc_s[g] for g in range(G)]
        pvs, new = visit([jnp.maximum(j, 0) for j in js], c_in, None)
        for g in range(G):
            acc_s[g] = accs[g] + pvs[g]
        c_max = jnp.max(functools.reduce(jnp.maximum, new))
        return d + 1, c_max > SB_LOG_ZERO, tuple(new)

    lax.while_loop(cond, body, (jnp.int32(1), jnp.bool_(True), tuple(cs)))
    for g in range(G):
        acc = acc_s[g]
        o_ref[g * BLK:(g + 1) * BLK, :] = jnp.concatenate([acc[:DH, :BLK], acc[DH:, BLK:]],
                                                          axis=0).T.astype(o_ref.dtype)


def _sb(proj, batch, seq):
    T = proj.shape[0]
    nq = seq // SB_BLOCK
    G = min(SB_QBLOCKS_PER_STEP, nq)
    n_steps = nq // G
    pairs = SB_HEADS // 2
    qo, ko, vo = PM_C // LANES, (PM_C + SB_HEADS * SB_DH) // LANES, (PM_C + 2 * SB_HEADS * SB_DH) // LANES
    return pl.pallas_call(
        _sb_kernel,
        grid=(batch, pairs, n_steps),
        in_specs=[pl.BlockSpec((G * SB_BLOCK, LANES), lambda b, p, i: (b * n_steps + i, qo + p)),
                  pl.BlockSpec((seq, LANES), lambda b, p, i: (b, ko + p)),
                  pl.BlockSpec((seq, LANES), lambda b, p, i: (b, vo + p))],
        out_specs=pl.BlockSpec((G * SB_BLOCK, LANES), lambda b, p, i: (b * n_steps + i, p)),
        out_shape=jax.ShapeDtypeStruct((T, SB_HEADS * SB_DH), BF16),
        scratch_shapes=[pltpu.VMEM((nq, LANES, SB_BLOCK), BF16),
                        pltpu.VMEM((G, LANES, 2 * SB_BLOCK), F32)],
        compiler_params=_cparams("parallel", "parallel", "arbitrary"),
    )(proj, proj, proj)


def _ret_kernel(qk_ref, v_ref, g_ref, cos_ref, sin_ref, nw_ref, o_ref, r_ref):
    tc = qk_ref.shape[0]
    C = RET_CHUNK
    half = RET_DK // 2

    @pl.when(pl.program_id(1) == 0)
    def _():
        r_ref[...] = jnp.zeros_like(r_ref)

    lane = lax.broadcasted_iota(jnp.int32, (C, LANES), 1)
    first = (lane & (RET_DK - 1)) < half
    ri = lax.broadcasted_iota(jnp.int32, (C, C), 0)
    ci = lax.broadcasted_iota(jnp.int32, (C, C), 1)
    diff = (ri - ci).astype(F32)
    causal = ri >= ci
    rowf = lax.broadcasted_iota(jnp.int32, (C, LANES), 0).astype(F32)
    nw = nw_ref[...]

    heads = range(RET_HEADS)
    chunks = range(tc // C)
    ch = [(c, h) for c in chunks for h in heads]
    rows = [slice(c * C, (c + 1) * C) for c in chunks]
    hsl = [slice(h * RET_DV, (h + 1) * RET_DV) for h in heads]
    dmat = [jnp.where(causal, jnp.exp(diff * _RET_LOG_GAMMA[h]), 0.0) for h in heads]
    xi = [jnp.exp((rowf + 1.0) * _RET_LOG_GAMMA[h]) for h in heads]
    zeta = [jnp.exp((C - 1.0 - rowf[:, :RET_DK]) * _RET_LOG_GAMMA[h]) for h in heads]

    def rope(x, c):
        sw = jnp.where(first, pltpu.roll(x, LANES - half, 1), pltpu.roll(x, half, 1))
        return x * cos_ref[rows[c], :] + sw * sin_ref[rows[c], :]

    kw = RET_HEADS * RET_DK
    qp = {(c, p): rope(qk_ref[rows[c], p * LANES:(p + 1) * LANES].astype(F32), c)
          for c in chunks for p in range(RET_HEADS // 2)}
    kp = {(c, p): rope(qk_ref[rows[c], kw + p * LANES:kw + (p + 1) * LANES].astype(F32), c)
          for c in chunks for p in range(RET_HEADS // 2)}
    q = {(c, h): qp[c, h // 2][:, (h % 2) * RET_DK:(h % 2 + 1) * RET_DK].astype(BF16) for c, h in ch}
    k = {(c, h): kp[c, h // 2][:, (h % 2) * RET_DK:(h % 2 + 1) * RET_DK] for c, h in ch}
    vb = {(c, h): v_ref[rows[c], hsl[h]].astype(BF16) for c, h in ch}
    gate = {(c, h): g_ref[rows[c], hsl[h]].astype(F32) for c, h in ch}
    intra = {(c, h): (_dot_nt(q[c, h], k[c, h].astype(BF16)) * dmat[h]).astype(BF16) for c, h in ch}
    delta = {(c, h): _dot((k[c, h] * zeta[h]).T.astype(BF16), vb[c, h]) for c, h in ch}
    r_in = {}
    for h in heads:
        r = r_ref[h]
        for c in chunks:
            r_in[c, h] = r.astype(BF16)
            r = r * math.exp(C * _RET_LOG_GAMMA[h]) + delta[c, h]
        r_ref[h] = r
    o = {(c, h): _dot(intra[c, h], vb[c, h]) + _dot(q[c, h], r_in[c, h]) * xi[h] for c, h in ch}
    for c, h in ch:
        gg = gate[c, h]
        o_ref[rows[c], hsl[h]] = (_rms(o[c, h], nw) * (gg * _sigmoid(gg))).astype(o_ref.dtype)


def _ret(proj, cos_t, sin_t, norm_w, batch, seq, tc):
    T = proj.shape[0]
    n_s = seq // tc
    W = RET_HEADS * RET_DV
    return pl.pallas_call(
        _ret_kernel,
        grid=(batch, n_s),
        in_specs=[pl.BlockSpec((tc, W), lambda b, s: (b * n_s + s, PM_D_QK // W)),
                  pl.BlockSpec((tc, W), lambda b, s: (b * n_s + s, PM_D_V // W)),
                  pl.BlockSpec((tc, W), lambda b, s: (b * n_s + s, PM_D_G // W)),
                  pl.BlockSpec((tc, LANES), lambda b, s: (s, 0)),
                  pl.BlockSpec((tc, LANES), lambda b, s: (s, 0)),
                  pl.BlockSpec((1, RET_DV), lambda b, s: (0, 0))],
        out_specs=pl.BlockSpec((tc, W), lambda b, s: (b * n_s + s, 0)),
        out_shape=jax.ShapeDtypeStruct((T, W), BF16),
        scratch_shapes=[pltpu.VMEM((RET_HEADS, RET_DK, RET_DV), F32)],
        compiler_params=_cparams("parallel", "arbitrary"),
    )(proj, proj, proj, cos_t, sin_t, norm_w)


def _merge_kernel(x_ref, oa_ref, ob0_ref, ob1_ref, ob2_ref, l0_ref, l1_ref, l2_ref, oc_ref, od_ref,
                  ga_ref, gb_ref, gc_ref, gd_ref, wa_ref, wb_ref, wc_ref, wd_ref, wo_ref, nw_ref, out_ref):
    def slabs(ref):
        return jnp.concatenate([ref[t] for t in range(ref.shape[0])], axis=1)

    l0, l1, l2 = slabs(l0_ref), slabs(l1_ref), slabs(l2_ref)
    m = jnp.maximum(jnp.maximum(l0, l1), l2)
    e0, e1, e2 = jnp.exp(l0 - m), jnp.exp(l1 - m), jnp.exp(l2 - m)
    inv = 1.0 / (e0 + e1 + e2)
    ob = jnp.concatenate([slabs(ob0_ref) * (e0 * inv), slabs(ob1_ref) * (e1 * inv), slabs(ob2_ref) * (e2 * inv)],
                         axis=1).astype(BF16)
    y = _sigmoid(ga_ref[...].astype(F32)) * _dot(oa_ref[...], wa_ref[...])
    y = y + _sigmoid(gb_ref[...].astype(F32)) * _dot(ob, wb_ref[...])
    y = y + _sigmoid(gc_ref[...].astype(F32)) * _dot(oc_ref[...], wc_ref[...])
    y = y + _sigmoid(gd_ref[...].astype(F32)) * _dot(od_ref[...], wd_ref[...])
    mixed = _dot(y.astype(BF16), wo_ref[...])
    out_ref[...] = x_ref[...] + _rms(mixed, nw_ref[...])


def _merge(x, oa, obs, lses, oc, od, proj, wa, wb, wc, wd, wo, nw, tm):
    T, D = x.shape
    row = lambda w: pl.BlockSpec((tm, w), lambda i: (i, 0))
    gate = lambda br: pl.BlockSpec((tm, D), lambda i: (i, PM_GATES // D + br))
    full = lambda a: pl.BlockSpec(a.shape, lambda i: (0, 0))
    return pl.pallas_call(
        _merge_kernel,
        grid=(T // tm,),
        in_specs=[row(D), row(oa.shape[1])]
                 + [pl.BlockSpec((DSWA_OW // LANES, tm, LANES), lambda i: (0, i, 0))] * 6
                 + [row(oc.shape[1]), row(od.shape[1])]
                 + [gate(0), gate(1), gate(2), gate(3)] + [full(wa), full(wb), full(wc), full(wd), full(wo), full(nw)],
        out_specs=row(D),
        out_shape=jax.ShapeDtypeStruct((T, D), F32),
        compiler_params=_cparams("parallel"),
    )(x, oa, *obs, *lses, oc, od, proj, proj, proj, proj, wa, wb, wc, wd, wo, nw)


def _mlp_kernel(x_ref, n1_ref, w1_ref, w2_ref, n2_ref, out_ref, h_ref, acc_ref):
    f = pl.program_id(1)

    @pl.when(f == 0)
    def _():
        h_ref[...] = _rms(x_ref[...], n1_ref[...]).astype(BF16)
        acc_ref[...] = jnp.zeros_like(acc_ref)

    hid = jnp.maximum(_dot(h_ref[...], w1_ref[...]), 0.0)
    acc_ref[...] += _dot((hid * hid).astype(BF16), w2_ref[...])

    @pl.when(f == pl.num_programs(1) - 1)
    def _():
        out_ref[...] = x_ref[...] + _rms(acc_ref[...], n2_ref[...])


def _mlp(x, n1, w1, w2, n2, tm, tf):
    T, D = x.shape
    F = w1.shape[1]
    return pl.pallas_call(
        _mlp_kernel,
        grid=(T // tm, F // tf),
        in_specs=[pl.BlockSpec((tm, D), lambda i, f: (i, 0)),
                  pl.BlockSpec((1, D), lambda i, f: (0, 0)),
                  pl.BlockSpec((D, tf), lambda i, f: (0, f)),
                  pl.BlockSpec((tf, D), lambda i, f: (f, 0)),
                  pl.BlockSpec((1, D), lambda i, f: (0, 0))],
        out_specs=pl.BlockSpec((tm, D), lambda i, f: (i, 0)),
        out_shape=jax.ShapeDtypeStruct((T, D), F32),
        scratch_shapes=[pltpu.VMEM((tm, D), BF16), pltpu.VMEM((tm, D), F32)],
        compiler_params=_cparams("parallel", "arbitrary"),
    )(x, n1, w1, w2, n2)


def _prep_w_in(w_in):
    sec = [w_in[:, _COL_OFF[i]:_COL_OFF[i + 1]] for i in range(len(_COL_SIZES))]
    a_qkv, a_z, a_a, a_b, b_qkv, c_qkv, d_qk, d_v, d_g, gates = sec
    sbw = SB_HEADS * SB_DH
    c_qkv = jnp.concatenate([c_qkv[:, :sbw] * SB_DH ** -0.5, c_qkv[:, sbw:]], axis=1)
    rw = RET_HEADS * RET_DK
    d_qk = jnp.concatenate([d_qk[:, :rw], d_qk[:, rw:] * RET_DK ** -0.5], axis=1)
    w_main = jnp.concatenate([a_qkv, a_z, c_qkv, d_qk, d_v, d_g, gates], axis=1).astype(BF16)
    bw = DSWA_HEADS * DSWA_DH
    gw = DSWA_HPG * DSWA_DH
    K = w_in.shape[0]
    groups = []
    for g in range(len(DSWA_GROUPS)):
        q = b_qkv[:, g * gw:(g + 1) * gw] * DSWA_DH ** -0.5
        k = b_qkv[:, bw + g * gw: bw + (g + 1) * gw]
        v = b_qkv[:, 2 * bw + g * gw: 2 * bw + (g + 1) * gw]
        groups += [q, k, v, jnp.zeros((K, DSWA_GW - 3 * gw), F32)]
    w_b = jnp.concatenate(groups + [a_a, a_b, jnp.zeros((K, LANES - 2 * GDN_HEADS), F32)], axis=1).astype(BF16)
    return w_main, w_b


def _rope_tables(seq):
    pos = jnp.arange(seq, dtype=jnp.int32).astype(F32)[:, None]
    inv_b = DSWA_ROPE_THETA ** (-jnp.arange(0, DSWA_ROPE_DIM, 2, dtype=F32) / DSWA_ROPE_DIM)
    ang = pos * inv_b[None, :]
    cb, sb = jnp.cos(ang), jnp.sin(ang)
    rest = DSWA_DH - DSWA_ROPE_DIM
    cos_b = jnp.tile(jnp.concatenate([cb, cb, jnp.ones((seq, rest), F32)], axis=1), (1, LANES // DSWA_DH))
    sin_b = jnp.tile(jnp.concatenate([-sb, sb, jnp.zeros((seq, rest), F32)], axis=1), (1, LANES // DSWA_DH))
    inv_r = RET_THETA ** (-jnp.linspace(0.0, 1.0, RET_DK // 2, dtype=F32))
    ang = pos * inv_r[None, :]
    cr, sr = jnp.cos(ang), jnp.sin(ang)
    cos_r = jnp.tile(jnp.concatenate([cr, cr], axis=1), (1, LANES // RET_DK))
    sin_r = jnp.tile(jnp.concatenate([-sr, sr], axis=1), (1, LANES // RET_DK))
    return cos_b, sin_b, cos_r, sin_r


def _lane_vec(v):
    return jnp.zeros((1, LANES), F32).at[0, :v.shape[0]].set(v.astype(F32))


def _layer(x, batch, seq, tabs, n_pre_mix, n_post_mix, n_pre_mlp, n_post_mlp, w_in, conv_w, a_log, dt_bias,
           gdn_norm, ret_norm, w_br_a, w_br_b, w_br_c, w_br_d, w_o, w_mlp_in, w_mlp_out):
    T = x.shape[0]
    tiles = _tile_sizes(T, seq)
    cos_b, sin_b, cos_r, sin_r = tabs
    w_main, w_b = _prep_w_in(w_in)
    gain = n_pre_mix.reshape(1, -1)
    proj = _norm_matmul(x, gain, w_main, tm=tiles["proj_rows"], tn=tiles["proj_cols"])
    g0, g1, g2, ab = _dswa_proj(x, gain, w_b, cos_b, sin_b, seq, tm=tiles["dswa_proj_rows"])

    cw = jnp.zeros((8, conv_w.shape[1]), F32).at[:GDN_CONV].set(conv_w)
    oa = _gdn(proj, ab, cw, _lane_vec(a_log), _lane_vec(dt_bias), gdn_norm.reshape(1, -1), batch, seq,
              tc=tiles["gdn_rows"])
    obs, lses = [], []
    for grp, (_, dil) in zip((g0, g1, g2), DSWA_GROUPS):
        o, lse = _dswa(grp, batch, seq, dil)
        obs.append(o)
        lses.append(lse)
    oc = _sb(proj, batch, seq)
    od = _ret(proj, cos_r, sin_r, ret_norm.reshape(1, -1), batch, seq, tc=tiles["ret_rows"])

    gw = DSWA_HPG * DSWA_DH
    wb = jnp.concatenate(
        [jnp.concatenate([w_br_b[g * gw:(g + 1) * gw], jnp.zeros((DSWA_OW - gw, D_MODEL), F32)], axis=0)
         for g in range(len(DSWA_GROUPS))], axis=0).astype(BF16)
    x1 = _merge(x, oa, obs, lses, oc, od, proj, w_br_a.astype(BF16), wb, w_br_c.astype(BF16),
                w_br_d.astype(BF16), w_o.astype(BF16), n_post_mix.reshape(1, -1), tm=tiles["merge_rows"])
    return _mlp(x1, n_pre_mlp.reshape(1, -1), w_mlp_in.astype(BF16), w_mlp_out.astype(BF16),
                n_post_mlp.reshape(1, -1), tm=tiles["mlp_rows"], tf=tiles["mlp_hidden"])


def _tile_sizes(n_tokens, seq):
    return dict(
        proj_rows=min(1024, n_tokens), proj_cols=3072,
        dswa_proj_rows=min(1024, seq),
        gdn_rows=min(4 * GDN_CHUNK, seq),
        ret_rows=min(8 * RET_CHUNK, seq),
        merge_rows=min(512, n_tokens),
        mlp_rows=min(1024, n_tokens), mlp_hidden=1024,
    )


def kernel(x, norm_pre_mix, norm_post_mix, norm_pre_mlp, norm_post_mlp, w_in, conv_w, a_log, dt_bias, gdn_norm,
           ret_norm, w_br_a, w_br_b, w_br_c, w_br_d, w_o, w_mlp_in, w_mlp_out):
    batch, seq, d = x.shape
    tabs = _rope_tables(seq)
    h = x.reshape(batch * seq, d)
    for l in range(norm_pre_mix.shape[0]):
        h = _layer(h, batch, seq, tabs, norm_pre_mix[l], norm_post_mix[l], norm_pre_mlp[l], norm_post_mlp[l],
                   w_in[l], conv_w[l], a_log[l], dt_bias[l], gdn_norm[l], ret_norm[l], w_br_a[l], w_br_b[l],
                   w_br_c[l], w_br_d[l], w_o[l], w_mlp_in[l], w_mlp_out[l])
    return h.reshape(batch, seq, d)
```

```python
import functools
import math

import numpy as np
import jax
import jax.numpy as jnp
from jax import lax
from jax.experimental import pallas as pl
from jax.experimental.pallas import tpu as pltpu

F32 = jnp.float32
BF16 = jnp.bfloat16

D_MODEL = 1024
GDN_HEADS, GDN_DK, GDN_DV, GDN_CONV, GDN_CHUNK = 4, 128, 128, 4, 64
DSWA_GROUPS = ((128, 1), (512, 4), (2048, 16))
DSWA_HPG, DSWA_DH, DSWA_BLOCK = 3, 64, 128
DSWA_HEADS = DSWA_HPG * len(DSWA_GROUPS)
DSWA_ROPE_THETA, DSWA_ROPE_DIM = 500000.0, DSWA_DH // 4
SB_HEADS, SB_DH, SB_BLOCK = 8, 64, 128
SB_LOG_ZERO = -110.0
SB_LOG_DEAD = -1e30
SB_QBLOCKS_PER_STEP = 16
DSWA_TOKENS_PER_STEP = 2048
DSWA_CHAIN_GROUP = 12
RET_HEADS, RET_DK, RET_DV, RET_THETA = 4, 64, 128, 10000.0
RET_CHUNK = 128
D_FF = 4 * D_MODEL
NORM_EPS = 1e-6
L2_EPS = 1e-6

_COL_SIZES = (1536, 512, 4, 4, 1728, 1536, 512, 512, 512, 4096)
_COL_OFF = np.concatenate([[0], np.cumsum(_COL_SIZES)])

PM_A_QKV, PM_A_Z, PM_C, PM_D_QK, PM_D_V, PM_D_G, PM_GATES, PM_WIDTH = 0, 1536, 2048, 3584, 4096, 4608, 5120, 9216
DSWA_GW = 640
DSWA_OW = 256

V7X_VMEM_BYTES = 64 * 1024 * 1024
VMEM_LIMIT = V7X_VMEM_BYTES * 3 // 4
LANES = 128

_RET_LOG_GAMMA = [float(np.log1p(-np.exp2(np.float32(-5.0 - h))).astype(np.float32)) for h in range(RET_HEADS)]


def _cparams(*sem):
    return pltpu.CompilerParams(dimension_semantics=sem, vmem_limit_bytes=VMEM_LIMIT)


def _sigmoid(x):
    return 1.0 / (1.0 + jnp.exp(-x))


def _softplus(x):
    return jnp.maximum(x, 0.0) + jnp.log(1.0 + jnp.exp(-jnp.abs(x)))


def _dot(a, b):
    return jnp.dot(a, b, preferred_element_type=F32)


def _dot_nt(a, b):
    return lax.dot_general(a, b, (((1,), (1,)), ((), ())), preferred_element_type=F32)


def _split3(x):
    hi = x.astype(BF16)
    r = x - hi.astype(F32)
    mid = r.astype(BF16)
    lo = (r - mid.astype(F32)).astype(BF16)
    return hi, mid, lo


def _dot_exact_lhs(a_bf16, x):
    hi, mid, lo = _split3(x)
    return _dot(a_bf16, hi) + (_dot(a_bf16, mid) + _dot(a_bf16, lo))


def _split2(x):
    hi = x.astype(BF16)
    lo = (x - hi.astype(F32)).astype(BF16)
    return hi, lo


def _rms(x, w):
    ms = jnp.mean(x * x, axis=-1, keepdims=True)
    return x * lax.rsqrt(ms + NORM_EPS) * w


def _norm_matmul_kernel(x_ref, g_ref, w_ref, o_ref, h_ref):
    @pl.when(pl.program_id(1) == 0)
    def _():
        h_ref[...] = _rms(x_ref[...], g_ref[...]).astype(BF16)

    o_ref[...] = _dot(h_ref[...], w_ref[...]).astype(o_ref.dtype)


def _norm_matmul(x, gain, w, tm, tn):
    T, K = x.shape
    N = w.shape[1]
    return pl.pallas_call(
        _norm_matmul_kernel,
        grid=(T // tm, N // tn),
        in_specs=[pl.BlockSpec((tm, K), lambda i, j: (i, 0)),
                  pl.BlockSpec((1, K), lambda i, j: (0, 0)),
                  pl.BlockSpec((K, tn), lambda i, j: (0, j))],
        out_specs=pl.BlockSpec((tm, tn), lambda i, j: (i, j)),
        out_shape=jax.ShapeDtypeStruct((T, N), BF16),
        scratch_shapes=[pltpu.VMEM((tm, K), BF16)],
        compiler_params=_cparams("parallel", "arbitrary"),
    )(x, gain, w)


def _dswa_proj_kernel(x_ref, g_ref, w_ref, cos_ref, sin_ref, o0_ref, o1_ref, o2_ref, oab_ref):
    h = _rms(x_ref[...], g_ref[...]).astype(BF16)
    p = _dot(h, w_ref[...])
    cos = cos_ref[...]
    sin = sin_ref[...]
    lane = lax.broadcasted_iota(jnp.int32, cos.shape, 1)
    first = (lane & (DSWA_DH - 1)) < (DSWA_ROPE_DIM // 2)
    for g, o_ref in enumerate((o0_ref, o1_ref, o2_ref)):
        base = g * DSWA_GW
        for t in range(3):
            xt = p[:, base + t * LANES: base + (t + 1) * LANES]
            sw = jnp.where(first, pltpu.roll(xt, LANES - DSWA_ROPE_DIM // 2, 1),
                           pltpu.roll(xt, DSWA_ROPE_DIM // 2, 1))
            o_ref[t] = xt * cos + sw * sin
        for t in range(3, DSWA_GW // LANES):
            o_ref[t] = p[:, base + t * LANES: base + (t + 1) * LANES]
    oab_ref[...] = p[:, 3 * DSWA_GW:]


def _dswa_proj(x, gain, w, cos_t, sin_t, seq, tm):
    T, K = x.shape
    N = w.shape[1]
    n_s = seq // tm
    grp = jax.ShapeDtypeStruct((DSWA_GW // LANES, T, LANES), F32)
    return pl.pallas_call(
        _dswa_proj_kernel,
        grid=(T // tm,),
        in_specs=[pl.BlockSpec((tm, K), lambda i: (i, 0)),
                  pl.BlockSpec((1, K), lambda i: (0, 0)),
                  pl.BlockSpec((K, N), lambda i: (0, 0)),
                  pl.BlockSpec((tm, LANES), lambda i: (i % n_s, 0)),
                  pl.BlockSpec((tm, LANES), lambda i: (i % n_s, 0))],
        out_specs=[pl.BlockSpec((DSWA_GW // LANES, tm, LANES), lambda i: (0, i, 0))] * 3
                  + [pl.BlockSpec((tm, LANES), lambda i: (i, 0))],
        out_shape=[grp, grp, grp, jax.ShapeDtypeStruct((T, LANES), F32)],
        compiler_params=_cparams("parallel"),
    )(x, gain, w, cos_t, sin_t)


def _per_head_matmul(xs, p_cat, diag_mask):
    C = GDN_CHUNK
    ph, pl_ = [jnp.concatenate([t] * GDN_HEADS, axis=0) * diag_mask for t in _split2(p_cat)]
    parts = [_split2(x) for x in xs]
    his = [hi for hi, _ in parts]
    by_hi = _dot(jnp.concatenate(his + [lo for _, lo in parts], axis=0), ph)
    by_lo = _dot(jnp.concatenate(his, axis=0), pl_)
    n = len(xs)
    return [by_hi[i * C:(i + 1) * C] + (by_hi[(n + i) * C:(n + i + 1) * C] + by_lo[i * C:(i + 1) * C])
            for i in range(n)]


def _unit_lower_inverse(lows, eye_cat, diag_mask):
    xs = [eye_cat - low for low in lows]
    ps = [_per_head_matmul([low], low, diag_mask)[0] for low in lows]
    yield
    n = 4
    while n < GDN_CHUNK:
        both = [_per_head_matmul([x, p], p, diag_mask) for x, p in zip(xs, ps)]
        xs = [x + xp for x, (xp, _) in zip(xs, both)]
        ps = [pp for _, pp in both]
        n *= 2
        yield
    return [x + _per_head_matmul([x], p, diag_mask)[0] for x, p in zip(xs, ps)]


def _interleave(*gens):
    results = [None] * len(gens)
    live = list(range(len(gens)))
    while live:
        for i in list(live):
            try:
                next(gens[i])
            except StopIteration as stop:
                results[i] = stop.value
                live.remove(i)
    return results


def _gdn_kernel(qkv_ref, z_ref, ab_ref, cw_ref, alog_ref, dtb_ref, nw_ref, o_ref,
                carry_ref, q_s, k_s, v_s, state_ref, uw_s, attn_s, qd_s, kdt_s, egl_s):
    tc = qkv_ref.shape[0]
    C = GDN_CHUNK
    H = GDN_HEADS
    HD = GDN_HEADS * GDN_DK
    step = pl.program_id(1)
    n_tiles = pl.num_programs(1) - 1
    n_chunks = tc // C
    heads = range(H)
    chunks = range(n_chunks)
    ch = [(c, h) for c in chunks for h in heads]
    hsl = [slice(h * GDN_DK, (h + 1) * GDN_DK) for h in heads]
    nw = nw_ref[...]

    @pl.when(step == 0)
    def _():
        carry_ref[0:8, :] = jnp.zeros((8, 3 * HD), F32)
        for ref in (state_ref, uw_s, attn_s, qd_s, kdt_s, egl_s):
            ref[...] = jnp.zeros_like(ref)

    def recurrence():
        st = [state_ref[h] for h in heads]
        uw = {(c, h): uw_s[c * H + h] for c, h in ch}
        attn = {(c, h): attn_s[c * H + h] for c, h in ch}
        q_dec = {(c, h): qd_s[c * H + h] for c, h in ch}
        k_dec_t = {(c, h): kdt_s[c * H + h] for c, h in ch}
        egl = {(c, h): egl_s[c * H + h][0:1, :] for c, h in ch}
        gate = {(c, h): z_ref[c * C:(c + 1) * C, hsl[h]].astype(F32) for c, h in ch}
        yield
        outs = {}
        for c in chunks:
            stb = [st[h].astype(BF16) for h in heads]
            w_st = [_dot(uw[c, h][:, GDN_DV:].astype(BF16), stb[h]) for h in heads]
            q_st = [_dot(q_dec[c, h], stb[h]) for h in heads]
            yield
            vnb = [(uw[c, h][:, :GDN_DV] - w_st[h]).astype(BF16) for h in heads]
            o = [q_st[h] + _dot(attn[c, h], vnb[h]) for h in heads]
            upd = [_dot(k_dec_t[c, h], vnb[h]) for h in heads]
            yield
            for h in heads:
                st[h] = st[h] * egl[c, h] + upd[h]
                zz = gate[c, h]
                outs[c, h] = _rms(o[h], nw) * (zz * _sigmoid(zz))
            yield
        return outs, st

    def store_recurrence(outs, st):
        for c, h in ch:
            o_ref[c * C:(c + 1) * C, hsl[h]] = outs[c, h].astype(o_ref.dtype)
        for h in heads:
            state_ref[h] = st[h]

    @pl.when(step == n_tiles)
    def _():
        store_recurrence(*_interleave(recurrence())[0])

    @pl.when(step < n_tiles)
    def _():
        (outs, st), pre = _interleave(
            recurrence(), _gdn_prepare(qkv_ref, ab_ref, cw_ref, alog_ref, dtb_ref, carry_ref, q_s, k_s, v_s))
        store_recurrence(outs, st)
        for c, h in ch:
            uw_s[c * H + h] = pre["uw"][c, h]
            attn_s[c * H + h] = pre["attn"][c, h]
            qd_s[c * H + h] = pre["q_dec"][c, h]
            kdt_s[c * H + h] = pre["k_dec_t"][c, h]
            egl_s[c * H + h] = jnp.broadcast_to(pre["egl"][c, h], (8, GDN_DK))


def _gdn_prepare(qkv_ref, ab_ref, cw_ref, alog_ref, dtb_ref, carry_ref, q_s, k_s, v_s):
    tc = qkv_ref.shape[0]
    C = GDN_CHUNK
    H = GDN_HEADS
    HD = GDN_HEADS * GDN_DK
    carry_ref[8:, :] = qkv_ref[...].astype(F32)
    for grp, dst in enumerate((q_s, k_s, v_s)):
        cols = slice(grp * HD, (grp + 1) * HD)
        y = carry_ref[8:, cols] * cw_ref[GDN_CONV - 1:GDN_CONV, cols]
        for j in range(1, GDN_CONV):
            y = y + carry_ref[8 - j:8 - j + tc, cols] * cw_ref[GDN_CONV - 1 - j:GDN_CONV - j, cols]
        y = y * _sigmoid(y)
        if grp == 2:
            dst[...] = y
        else:
            for h in range(GDN_HEADS):
                yh = y[:, h * GDN_DK:(h + 1) * GDN_DK]
                yh = yh * lax.rsqrt(jnp.sum(yh * yh, axis=-1, keepdims=True) + L2_EPS)
                if grp == 0:
                    yh = yh * (GDN_DK ** -0.5)
                dst[:, h * GDN_DK:(h + 1) * GDN_DK] = yh
        yield
    carry_ref[0:8, :] = carry_ref[tc:tc + 8, :]

    ri = lax.broadcasted_iota(jnp.int32, (C, C), 0)
    ci = lax.broadcasted_iota(jnp.int32, (C, C), 1)
    incl = ri >= ci
    strict = ri > ci
    a_low = jnp.where(incl, 1.0, 0.0).astype(BF16)
    eye_cat = jnp.concatenate([jnp.where(ri == ci, 1.0, 0.0).astype(F32)] * H, axis=1)
    rb = lax.broadcasted_iota(jnp.int32, (H * C, H * C), 0) // C
    cb = lax.broadcasted_iota(jnp.int32, (H * C, H * C), 1) // C
    diag_mask = jnp.where(rb == cb, 1.0, 0.0).astype(BF16)
    neg_exp_alog = -jnp.exp(alog_ref[...])
    dtb = dtb_ref[...]
    n_chunks = tc // C
    heads = range(H)
    hsl = [slice(h * GDN_DK, (h + 1) * GDN_DK) for h in heads]

    chunks = range(n_chunks)
    ch = [(c, h) for c in chunks for h in heads]
    rows = [slice(c * C, (c + 1) * C) for c in chunks]
    ab = [ab_ref[rows[c], :] for c in chunks]
    gv = [neg_exp_alog * _softplus(ab[c] + dtb) for c in chunks]
    bv = [_sigmoid(ab[c]) for c in chunks]
    gc_all = [_dot_exact_lhs(a_low, gv[c]) for c in chunks]
    gc_t = [gc_all[c].T for c in chunks]
    yield
    q = {(c, h): q_s[rows[c], hsl[h]] for c, h in ch}
    k = {(c, h): k_s[rows[c], hsl[h]] for c, h in ch}
    v = {(c, h): v_s[rows[c], hsl[h]] for c, h in ch}
    gc = {(c, h): jnp.broadcast_to(gc_all[c][:, h:h + 1], (C, GDN_DK)) for c, h in ch}
    beta = {(c, h): jnp.broadcast_to(bv[c][:, H + h:H + h + 1], (C, GDN_DK)) for c, h in ch}
    decay = {(c, h): jnp.exp(jnp.where(incl, gc[c, h][:, :C] - gc_t[c][h:h + 1, :], -jnp.inf)) for c, h in ch}
    egc = {x: jnp.exp(gc[x]) for x in ch}
    gl = {x: gc[x][C - 1:C, :] for x in ch}
    kb = {x: k[x] * beta[x] for x in ch}
    kbf = {x: k[x].astype(BF16) for x in ch}
    kk = {x: _dot_nt(kb[x].astype(BF16), kbf[x]) for x in ch}
    low_cat = [jnp.concatenate([jnp.where(strict, kk[c, h] * decay[c, h], 0.0) for h in heads], axis=1)
               for c in chunks]
    yield
    t_cat = [t.astype(BF16) for t in (yield from _unit_lower_inverse(low_cat, eye_cat, diag_mask))]
    uw = {(c, h): _dot(t_cat[c][:, h * C:(h + 1) * C],
                       jnp.concatenate([v[c, h] * beta[c, h], kb[c, h] * egc[c, h]], axis=1).astype(BF16))
          for c, h in ch}
    yield
    attn = {x: (_dot_nt(q[x].astype(BF16), kbf[x]) * decay[x]).astype(BF16) for x in ch}
    q_dec = {x: (q[x] * egc[x]).astype(BF16) for x in ch}
    k_dec_t = {x: (k[x] * jnp.exp(gl[x] - gc[x])).T.astype(BF16) for x in ch}
    egl = {x: jnp.exp(gl[x]) for x in ch}
    return dict(uw=uw, attn=attn, q_dec=q_dec, k_dec_t=k_dec_t, egl=egl)


def _gdn(proj, ab, conv_w, alog_v, dtb_v, norm_w, batch, seq, tc):
    T = proj.shape[0]
    n_s = seq // tc
    HD = GDN_HEADS * GDN_DK
    n_pairs = (tc // GDN_CHUNK) * GDN_HEADS
    cur = lambda b, s: b * n_s + jnp.minimum(s, n_s - 1)
    lag = lambda b, s: b * n_s + jnp.maximum(s - 1, 0)
    return pl.pallas_call(
        _gdn_kernel,
        grid=(batch, n_s + 1),
        in_specs=[pl.BlockSpec((tc, 3 * HD), lambda b, s: (cur(b, s), PM_A_QKV // (3 * HD))),
                  pl.BlockSpec((tc, HD), lambda b, s: (lag(b, s), PM_A_Z // HD)),
                  pl.BlockSpec((tc, LANES), lambda b, s: (cur(b, s), 0)),
                  pl.BlockSpec((8, 3 * HD), lambda b, s: (0, 0)),
                  pl.BlockSpec((1, LANES), lambda b, s: (0, 0)),
                  pl.BlockSpec((1, LANES), lambda b, s: (0, 0)),
                  pl.BlockSpec((1, GDN_DV), lambda b, s: (0, 0))],
        out_specs=pl.BlockSpec((tc, HD), lambda b, s: (lag(b, s), 0)),
        out_shape=jax.ShapeDtypeStruct((T, HD), BF16),
        scratch_shapes=[pltpu.VMEM((tc + 8, 3 * HD), F32),
                        pltpu.VMEM((tc, HD), F32), pltpu.VMEM((tc, HD), F32), pltpu.VMEM((tc, HD), F32),
                        pltpu.VMEM((GDN_HEADS, GDN_DK, GDN_DV), F32),
                        pltpu.VMEM((n_pairs, GDN_CHUNK, 2 * GDN_DK), F32),
                        pltpu.VMEM((n_pairs, GDN_CHUNK, GDN_CHUNK), BF16),
                        pltpu.VMEM((n_pairs, GDN_CHUNK, GDN_DK), BF16),
                        pltpu.VMEM((n_pairs, GDN_DK, GDN_CHUNK), BF16),
                        pltpu.VMEM((n_pairs, 8, GDN_DK), F32)],
        compiler_params=_cparams("parallel", "arbitrary"),
    )(proj, proj, ab, conv_w, alog_v, dtb_v, norm_w)


def _dswa_kernel(cur_ref, prev_ref, o_ref, lse_ref, *, dil):
    BL = DSWA_BLOCK
    n_blk = cur_ref.shape[1] // (BL * dil)
    n = pl.program_id(1)

    def class_rows(ref, r, blk0, n_blocks):
        start, size = r + blk0 * BL * dil, n_blocks * BL
        rows = pl.ds(start, size, stride=dil) if dil > 1 else slice(start, start + size)
        return rows, jnp.concatenate([ref[t, rows, :] for t in range(ref.shape[0])], axis=1)

    ii = lax.broadcasted_iota(jnp.int32, (BL, 2 * BL), 0)
    jj = lax.broadcasted_iota(jnp.int32, (BL, 2 * BL), 1)
    band = (jj >= ii) & (jj <= ii + BL)
    first_valid = jnp.where(n > 0, 0, BL)
    band_first = band & (jj >= first_valid)
    kw = DSWA_HPG * DSWA_DH
    pad = jnp.zeros((BL, DSWA_OW - kw), F32)
    ones_keys = jnp.ones((2 * BL, DSWA_DH), BF16)
    hsl = [slice(h * DSWA_DH, (h + 1) * DSWA_DH) for h in range(DSWA_HPG)]

    mc = min(n_blk, DSWA_CHAIN_GROUP // DSWA_HPG)
    units = [(r, m0) for r in range(dil) for m0 in range(0, n_blk, mc)]
    per_group = max(1, DSWA_CHAIN_GROUP // (mc * DSWA_HPG))
    for u0 in range(0, len(units), per_group):
        group = units[u0:u0 + per_group]
        rows, cur, k_all, v_all = {}, {}, {}, {}
        for u in group:
            r, m0 = u
            rows[u], cur[u] = class_rows(cur_ref, r, m0, mc)
            before = class_rows(prev_ref, r, 0, 1)[1] if m0 == 0 else class_rows(cur_ref, r, m0 - 1, 1)[1]
            k_all[u] = jnp.concatenate([before[:, kw:2 * kw], cur[u][:, kw:2 * kw]], axis=0).astype(BF16)
            v_all[u] = jnp.concatenate([before[:, 2 * kw:3 * kw], cur[u][:, 2 * kw:3 * kw]], axis=0).astype(BF16)
        chains = [(u, m, h) for u in group for m in range(mc) for h in range(DSWA_HPG)]
        keys = {m: slice(m * BL, (m + 2) * BL) for m in range(mc)}
        q = {(u, m, h): cur[u][m * BL:(m + 1) * BL, hsl[h]].astype(BF16) for u, m, h in chains}
        s = {(u, m, h): jnp.where(band_first if (u[1] == 0 and m == 0) else band,
                                  _dot_nt(q[u, m, h], k_all[u][keys[m], hsl[h]]), -jnp.inf) for u, m, h in chains}
        mx = {x: jnp.max(s[x], axis=-1, keepdims=True) for x in chains}
        p = {x: jnp.exp(s[x] - mx[x]) for x in chains}
        pb = {x: p[x].astype(BF16) for x in chains}
        l = {x: _dot(pb[x], ones_keys) for x in chains}
        o = {(u, m, h): _dot(pb[u, m, h], v_all[u][keys[m], hsl[h]]) * (1.0 / l[u, m, h]) for u, m, h in chains}
        lse = {x: mx[x] + jnp.log(l[x]) for x in chains}
        for u in group:
            o_u = jnp.concatenate([jnp.concatenate([o[u, m, h] for h in range(DSWA_HPG)] + [pad], axis=1)
                                   for m in range(mc)], axis=0)
            lse_u = jnp.concatenate([jnp.concatenate([lse[u, m, h] for h in range(DSWA_HPG)] + [pad], axis=1)
                                     for m in range(mc)], axis=0)
            for t in range(DSWA_OW // LANES):
                o_ref[t, rows[u], :] = o_u[:, t * LANES:(t + 1) * LANES]
                lse_ref[t, rows[u], :] = lse_u[:, t * LANES:(t + 1) * LANES]


def _dswa(grp, batch, seq, dil):
    n_in, T, _ = grp.shape
    n_out = DSWA_OW // LANES
    span = DSWA_BLOCK * dil
    n_blk = max(1, min(DSWA_TOKENS_PER_STEP, seq) // span)
    step = n_blk * span
    n_steps = seq // step
    out = jax.ShapeDtypeStruct((n_out, T, LANES), F32)
    return pl.pallas_call(
        functools.partial(_dswa_kernel, dil=dil),
        grid=(batch, n_steps),
        in_specs=[pl.BlockSpec((n_in, step, LANES), lambda b, n: (0, b * n_steps + n, 0)),
                  pl.BlockSpec((n_in, span, LANES),
                               lambda b, n: (0, b * (seq // span) + jnp.maximum(n * n_blk - 1, 0), 0))],
        out_specs=[pl.BlockSpec((n_out, step, LANES), lambda b, n: (0, b * n_steps + n, 0))] * 2,
        out_shape=[out, out],
        compiler_params=_cparams("parallel", "arbitrary"),
    )(grp, grp)


def _sb_kernel(q_ref, k_ref, v_ref, o_ref, vt_s, acc_s):
    BLK, DH = SB_BLOCK, SB_DH
    W = 2 * BLK
    G = q_ref.shape[0] // BLK
    n_kblocks = k_ref.shape[0] // BLK
    step_id = pl.program_id(2)
    i0 = step_id * G

    @pl.when(step_id == 0)
    def _():
        for j in range(n_kblocks):
            vt_s[j] = v_ref[j * BLK:(j + 1) * BLK, :].astype(F32).T.astype(BF16)

    ri = lax.broadcasted_iota(jnp.int32, (BLK, W), 0)
    ci = lax.broadcasted_iota(jnp.int32, (BLK, W), 1)
    causal = ri < (ci & (BLK - 1))
    r2 = lax.broadcasted_iota(jnp.int32, (BLK, BLK), 0)
    c2 = lax.broadcasted_iota(jnp.int32, (BLK, BLK), 1)
    after = jnp.where(c2 > r2, 1.0, 0.0).astype(BF16)
    after2 = jnp.concatenate([after, after], axis=1)

    head0_feat = lax.broadcasted_iota(jnp.int32, (BLK, LANES), 1) < DH
    q_bd = []
    for g in range(G):
        qg = q_ref[g * BLK:(g + 1) * BLK, :]
        zero = jnp.zeros_like(qg)
        q_bd.append(jnp.concatenate([jnp.where(head0_feat, qg, zero), jnp.where(head0_feat, zero, qg)], axis=0))

    def visit(js, c_rows, mask):
        ks = [k_ref[pl.ds(pl.multiple_of(js[g] * BLK, BLK), BLK), :].astype(BF16) for g in range(G)]
        vts = [vt_s[js[g]] for g in range(G)]
        zs = [_dot_nt(ks[g], q_bd[g]) for g in range(G)]
        sps = [jnp.maximum(z, 0.0) + jnp.log(1.0 + jnp.exp(-jnp.abs(z))) for z in zs]
        sp_ms = sps if mask is None else [jnp.where(mask, sp, 0.0) for sp in sps]
        splits = [_split2(sp_m) for sp_m in sp_ms]
        sufs = [_dot(after2, jnp.concatenate([hi, lo], axis=0)) for hi, lo in splits]
        a_s = [jnp.exp(zs[g] - sps[g] - sufs[g] + c_rows[g]) for g in range(G)]
        if mask is not None:
            a_s = [jnp.where(mask, a, 0.0) for a in a_s]
        pvs = [_dot(vts[g], a_s[g].astype(BF16)) for g in range(G)]
        return pvs, [c_rows[g] - (sufs[g][0:1, :] + sp_ms[g][0:1, :]) for g in range(G)]

    pvs, cs = visit([i0 + g for g in range(G)], [jnp.zeros((1, W), F32)] * G, causal)
    for g in range(G):
        acc_s[g] = pvs[g]

    def cond(state):
        d, live, _ = state
        return (d < i0 + G) & live

    def body(state):
        d, _, cs = state
        js = [i0 + g - d for g in range(G)]
        c_in = [jnp.where(js[g] < 0, SB_LOG_DEAD, cs[g]) for g in range(G)]
        accs = [acc_s[g] for g in range(G)]
        pvs, new = visit([jnp.maximum(j, 0) for j in js], c_in, None)
        for g in range(G):
            acc_s[g] = accs[g] + pvs[g]
        c_max = jnp.max(functools.reduce(jnp.maximum, new))
        return d + 1, c_max > SB_LOG_ZERO, tuple(new)

    lax.while_loop(cond, body, (jnp.int32(1), jnp.bool_(True), tuple(cs)))
    for g in range(G):
        acc = acc_s[g]
        o_ref[g * BLK:(g + 1) * BLK, :] = jnp.concatenate([acc[:DH, :BLK], acc[DH:, BLK:]],
                                                          axis=0).T.astype(o_ref.dtype)


def _sb(proj, batch, seq):
    T = proj.shape[0]
    nq = seq // SB_BLOCK
    G = min(SB_QBLOCKS_PER_STEP, nq)
    n_steps = nq // G
    pairs = SB_HEADS // 2
    qo, ko, vo = PM_C // LANES, (PM_C + SB_HEADS * SB_DH) // LANES, (PM_C + 2 * SB_HEADS * SB_DH) // LANES
    return pl.pallas_call(
        _sb_kernel,
        grid=(batch, pairs, n_steps),
        in_specs=[pl.BlockSpec((G * SB_BLOCK, LANES), lambda b, p, i: (b * n_steps + i, qo + p)),
                  pl.BlockSpec((seq, LANES), lambda b, p, i: (b, ko + p)),
                  pl.BlockSpec((seq, LANES), lambda b, p, i: (b, vo + p))],
        out_specs=pl.BlockSpec((G * SB_BLOCK, LANES), lambda b, p, i: (b * n_steps + i, p)),
        out_shape=jax.ShapeDtypeStruct((T, SB_HEADS * SB_DH), BF16),
        scratch_shapes=[pltpu.VMEM((nq, LANES, SB_BLOCK), BF16),
                        pltpu.VMEM((G, LANES, 2 * SB_BLOCK), F32)],
        compiler_params=_cparams("parallel", "parallel", "arbitrary"),
    )(proj, proj, proj)


def _ret_kernel(qk_ref, v_ref, g_ref, cos_ref, sin_ref, nw_ref, o_ref, r_ref):
    tc = qk_ref.shape[0]
    C = RET_CHUNK
    half = RET_DK // 2

    @pl.when(pl.program_id(1) == 0)
    def _():
        r_ref[...] = jnp.zeros_like(r_ref)

    lane = lax.broadcasted_iota(jnp.int32, (C, LANES), 1)
    first = (lane & (RET_DK - 1)) < half
    ri = lax.broadcasted_iota(jnp.int32, (C, C), 0)
    ci = lax.broadcasted_iota(jnp.int32, (C, C), 1)
    diff = (ri - ci).astype(F32)
    causal = ri >= ci
    rowf = lax.broadcasted_iota(jnp.int32, (C, LANES), 0).astype(F32)
    nw = nw_ref[...]

    heads = range(RET_HEADS)
    chunks = range(tc // C)
    ch = [(c, h) for c in chunks for h in heads]
    rows = [slice(c * C, (c + 1) * C) for c in chunks]
    hsl = [slice(h * RET_DV, (h + 1) * RET_DV) for h in heads]
    dmat = [jnp.where(causal, jnp.exp(diff * _RET_LOG_GAMMA[h]), 0.0) for h in heads]
    xi = [jnp.exp((rowf + 1.0) * _RET_LOG_GAMMA[h]) for h in heads]
    zeta = [jnp.exp((C - 1.0 - rowf[:, :RET_DK]) * _RET_LOG_GAMMA[h]) for h in heads]

    def rope(x, c):
        sw = jnp.where(first, pltpu.roll(x, LANES - half, 1), pltpu.roll(x, half, 1))
        return x * cos_ref[rows[c], :] + sw * sin_ref[rows[c], :]

    kw = RET_HEADS * RET_DK
    qp = {(c, p): rope(qk_ref[rows[c], p * LANES:(p + 1) * LANES].astype(F32), c)
          for c in chunks for p in range(RET_HEADS // 2)}
    kp = {(c, p): rope(qk_ref[rows[c], kw + p * LANES:kw + (p + 1) * LANES].astype(F32), c)
          for c in chunks for p in range(RET_HEADS // 2)}
    q = {(c, h): qp[c, h // 2][:, (h % 2) * RET_DK:(h % 2 + 1) * RET_DK].astype(BF16) for c, h in ch}
    k = {(c, h): kp[c, h // 2][:, (h % 2) * RET_DK:(h % 2 + 1) * RET_DK] for c, h in ch}
    vb = {(c, h): v_ref[rows[c], hsl[h]].astype(BF16) for c, h in ch}
    gate = {(c, h): g_ref[rows[c], hsl[h]].astype(F32) for c, h in ch}
    intra = {(c, h): (_dot_nt(q[c, h], k[c, h].astype(BF16)) * dmat[h]).astype(BF16) for c, h in ch}
    delta = {(c, h): _dot((k[c, h] * zeta[h]).T.astype(BF16), vb[c, h]) for c, h in ch}
    r_in = {}
    for h in heads:
        r = r_ref[h]
        for c in chunks:
            r_in[c, h] = r.astype(BF16)
            r = r * math.exp(C * _RET_LOG_GAMMA[h]) + delta[c, h]
        r_ref[h] = r
    o = {(c, h): _dot(intra[c, h], vb[c, h]) + _dot(q[c, h], r_in[c, h]) * xi[h] for c, h in ch}
    for c, h in ch:
        gg = gate[c, h]
        o_ref[rows[c], hsl[h]] = (_rms(o[c, h], nw) * (gg * _sigmoid(gg))).astype(o_ref.dtype)


def _ret(proj, cos_t, sin_t, norm_w, batch, seq, tc):
    T = proj.shape[0]
    n_s = seq // tc
    W = RET_HEADS * RET_DV
    return pl.pallas_call(
        _ret_kernel,
        grid=(batch, n_s),
        in_specs=[pl.BlockSpec((tc, W), lambda b, s: (b * n_s + s, PM_D_QK // W)),
                  pl.BlockSpec((tc, W), lambda b, s: (b * n_s + s, PM_D_V // W)),
                  pl.BlockSpec((tc, W), lambda b, s: (b * n_s + s, PM_D_G // W)),
                  pl.BlockSpec((tc, LANES), lambda b, s: (s, 0)),
                  pl.BlockSpec((tc, LANES), lambda b, s: (s, 0)),
                  pl.BlockSpec((1, RET_DV), lambda b, s: (0, 0))],
        out_specs=pl.BlockSpec((tc, W), lambda b, s: (b * n_s + s, 0)),
        out_shape=jax.ShapeDtypeStruct((T, W), BF16),
        scratch_shapes=[pltpu.VMEM((RET_HEADS, RET_DK, RET_DV), F32)],
        compiler_params=_cparams("parallel", "arbitrary"),
    )(proj, proj, proj, cos_t, sin_t, norm_w)


def _merge_kernel(x_ref, oa_ref, ob0_ref, ob1_ref, ob2_ref, l0_ref, l1_ref, l2_ref, oc_ref, od_ref,
                  ga_ref, gb_ref, gc_ref, gd_ref, wa_ref, wb_ref, wc_ref, wd_ref, wo_ref, nw_ref, out_ref):
    def slabs(ref):
        return jnp.concatenate([ref[t] for t in range(ref.shape[0])], axis=1)

    l0, l1, l2 = slabs(l0_ref), slabs(l1_ref), slabs(l2_ref)
    m = jnp.maximum(jnp.maximum(l0, l1), l2)
    e0, e1, e2 = jnp.exp(l0 - m), jnp.exp(l1 - m), jnp.exp(l2 - m)
    inv = 1.0 / (e0 + e1 + e2)
    ob = jnp.concatenate([slabs(ob0_ref) * (e0 * inv), slabs(ob1_ref) * (e1 * inv), slabs(ob2_ref) * (e2 * inv)],
                         axis=1).astype(BF16)
    y = _sigmoid(ga_ref[...].astype(F32)) * _dot(oa_ref[...], wa_ref[...])
    y = y + _sigmoid(gb_ref[...].astype(F32)) * _dot(ob, wb_ref[...])
    y = y + _sigmoid(gc_ref[...].astype(F32)) * _dot(oc_ref[...], wc_ref[...])
    y = y + _sigmoid(gd_ref[...].astype(F32)) * _dot(od_ref[...], wd_ref[...])
    mixed = _dot(y.astype(BF16), wo_ref[...])
    out_ref[...] = x_ref[...] + _rms(mixed, nw_ref[...])


def _merge(x, oa, obs, lses, oc, od, proj, wa, wb, wc, wd, wo, nw, tm):
    T, D = x.shape
    row = lambda w: pl.BlockSpec((tm, w), lambda i: (i, 0))
    gate = lambda br: pl.BlockSpec((tm, D), lambda i: (i, PM_GATES // D + br))
    full = lambda a: pl.BlockSpec(a.shape, lambda i: (0, 0))
    return pl.pallas_call(
        _merge_kernel,
        grid=(T // tm,),
        in_specs=[row(D), row(oa.shape[1])]
                 + [pl.BlockSpec((DSWA_OW // LANES, tm, LANES), lambda i: (0, i, 0))] * 6
                 + [row(oc.shape[1]), row(od.shape[1])]
                 + [gate(0), gate(1), gate(2), gate(3)] + [full(wa), full(wb), full(wc), full(wd), full(wo), full(nw)],
        out_specs=row(D),
        out_shape=jax.ShapeDtypeStruct((T, D), F32),
        compiler_params=_cparams("parallel"),
    )(x, oa, *obs, *lses, oc, od, proj, proj, proj, proj, wa, wb, wc, wd, wo, nw)


def _mlp_kernel(x_ref, n1_ref, w1_ref, w2_ref, n2_ref, out_ref, h_ref, acc_ref):
    f = pl.program_id(1)

    @pl.when(f == 0)
    def _():
        h_ref[...] = _rms(x_ref[...], n1_ref[...]).astype(BF16)
        acc_ref[...] = jnp.zeros_like(acc_ref)

    hid = jnp.maximum(_dot(h_ref[...], w1_ref[...]), 0.0)
    acc_ref[...] += _dot((hid * hid).astype(BF16), w2_ref[...])

    @pl.when(f == pl.num_programs(1) - 1)
    def _():
        out_ref[...] = x_ref[...] + _rms(acc_ref[...], n2_ref[...])


def _mlp(x, n1, w1, w2, n2, tm, tf):
    T, D = x.shape
    F = w1.shape[1]
    return pl.pallas_call(
        _mlp_kernel,
        grid=(T // tm, F // tf),
        in_specs=[pl.BlockSpec((tm, D), lambda i, f: (i, 0)),
                  pl.BlockSpec((1, D), lambda i, f: (0, 0)),
                  pl.BlockSpec((D, tf), lambda i, f: (0, f)),
                  pl.BlockSpec((tf, D), lambda i, f: (f, 0)),
                  pl.BlockSpec((1, D), lambda i, f: (0, 0))],
        out_specs=pl.BlockSpec((tm, D), lambda i, f: (i, 0)),
        out_shape=jax.ShapeDtypeStruct((T, D), F32),
        scratch_shapes=[pltpu.VMEM((tm, D), BF16), pltpu.VMEM((tm, D), F32)],
        compiler_params=_cparams("parallel", "arbitrary"),
    )(x, n1, w1, w2, n2)


def _prep_w_in(w_in):
    sec = [w_in[..., _COL_OFF[i]:_COL_OFF[i + 1]] for i in range(len(_COL_SIZES))]
    a_qkv, a_z, a_a, a_b, b_qkv, c_qkv, d_qk, d_v, d_g, gates = sec
    sbw = SB_HEADS * SB_DH
    c_qkv = jnp.concatenate([c_qkv[..., :sbw] * SB_DH ** -0.5, c_qkv[..., sbw:]], axis=-1)
    rw = RET_HEADS * RET_DK
    d_qk = jnp.concatenate([d_qk[..., :rw], d_qk[..., rw:] * RET_DK ** -0.5], axis=-1)
    w_main = jnp.concatenate([a_qkv, a_z, c_qkv, d_qk, d_v, d_g, gates], axis=-1).astype(BF16)
    bw = DSWA_HEADS * DSWA_DH
    gw = DSWA_HPG * DSWA_DH
    lead = w_in.shape[:-1]
    groups = []
    for g in range(len(DSWA_GROUPS)):
        q = b_qkv[..., g * gw:(g + 1) * gw] * DSWA_DH ** -0.5
        k = b_qkv[..., bw + g * gw: bw + (g + 1) * gw]
        v = b_qkv[..., 2 * bw + g * gw: 2 * bw + (g + 1) * gw]
        groups += [q, k, v, jnp.zeros(lead + (DSWA_GW - 3 * gw,), F32)]
    w_b = jnp.concatenate(groups + [a_a, a_b, jnp.zeros(lead + (LANES - 2 * GDN_HEADS,), F32)],
                          axis=-1).astype(BF16)
    return w_main, w_b


def _prep_params(w_in, conv_w, a_log, dt_bias, w_br_a, w_br_b, w_br_c, w_br_d, w_o, w_mlp_in, w_mlp_out):
    n_layers = w_in.shape[0]
    w_main, w_b = _prep_w_in(w_in)
    gw = DSWA_HPG * DSWA_DH
    wb = jnp.concatenate(
        [jnp.concatenate([w_br_b[:, g * gw:(g + 1) * gw], jnp.zeros((n_layers, DSWA_OW - gw, D_MODEL), F32)], axis=1)
         for g in range(len(DSWA_GROUPS))], axis=1).astype(BF16)
    lane_vec = lambda v: jnp.zeros((n_layers, 1, LANES), F32).at[:, 0, :v.shape[1]].set(v.astype(F32))
    return dict(
        w_main=w_main, w_b=w_b,
        cw=jnp.zeros((n_layers, 8, conv_w.shape[2]), F32).at[:, :GDN_CONV].set(conv_w),
        alog=lane_vec(a_log), dtb=lane_vec(dt_bias),
        wa=w_br_a.astype(BF16), wb=wb, wc=w_br_c.astype(BF16), wd=w_br_d.astype(BF16), wo=w_o.astype(BF16),
        w1=w_mlp_in.astype(BF16), w2=w_mlp_out.astype(BF16))


def _rope_tables(seq):
    pos = jnp.arange(seq, dtype=jnp.int32).astype(F32)[:, None]
    inv_b = DSWA_ROPE_THETA ** (-jnp.arange(0, DSWA_ROPE_DIM, 2, dtype=F32) / DSWA_ROPE_DIM)
    ang = pos * inv_b[None, :]
    cb, sb = jnp.cos(ang), jnp.sin(ang)
    rest = DSWA_DH - DSWA_ROPE_DIM
    cos_b = jnp.tile(jnp.concatenate([cb, cb, jnp.ones((seq, rest), F32)], axis=1), (1, LANES // DSWA_DH))
    sin_b = jnp.tile(jnp.concatenate([-sb, sb, jnp.zeros((seq, rest), F32)], axis=1), (1, LANES // DSWA_DH))
    inv_r = RET_THETA ** (-jnp.linspace(0.0, 1.0, RET_DK // 2, dtype=F32))
    ang = pos * inv_r[None, :]
    cr, sr = jnp.cos(ang), jnp.sin(ang)
    cos_r = jnp.tile(jnp.concatenate([cr, cr], axis=1), (1, LANES // RET_DK))
    sin_r = jnp.tile(jnp.concatenate([-sr, sr], axis=1), (1, LANES // RET_DK))
    return cos_b, sin_b, cos_r, sin_r


def _layer(x, batch, seq, tabs, n_pre_mix, n_post_mix, n_pre_mlp, n_post_mlp, gdn_norm, ret_norm, p):
    T = x.shape[0]
    tiles = _tile_sizes(T, seq)
    cos_b, sin_b, cos_r, sin_r = tabs
    gain = n_pre_mix.reshape(1, -1)
    proj = _norm_matmul(x, gain, p["w_main"], tm=tiles["proj_rows"], tn=tiles["proj_cols"])
    g0, g1, g2, ab = _dswa_proj(x, gain, p["w_b"], cos_b, sin_b, seq, tm=tiles["dswa_proj_rows"])

    oa = _gdn(proj, ab, p["cw"], p["alog"], p["dtb"], gdn_norm.reshape(1, -1), batch, seq, tc=tiles["gdn_rows"])
    obs, lses = [], []
    for grp, (_, dil) in zip((g0, g1, g2), DSWA_GROUPS):
        o, lse = _dswa(grp, batch, seq, dil)
        obs.append(o)
        lses.append(lse)
    oc = _sb(proj, batch, seq)
    od = _ret(proj, cos_r, sin_r, ret_norm.reshape(1, -1), batch, seq, tc=tiles["ret_rows"])

    x1 = _merge(x, oa, obs, lses, oc, od, proj, p["wa"], p["wb"], p["wc"], p["wd"], p["wo"],
                n_post_mix.reshape(1, -1), tm=tiles["merge_rows"])
    return _mlp(x1, n_pre_mlp.reshape(1, -1), p["w1"], p["w2"], n_post_mlp.reshape(1, -1),
                tm=tiles["mlp_rows"], tf=tiles["mlp_hidden"])


def _tile_sizes(n_tokens, seq):
    return dict(
        proj_rows=min(1024, n_tokens), proj_cols=3072,
        dswa_proj_rows=min(1024, seq),
        gdn_rows=min(4 * GDN_CHUNK, seq),
        ret_rows=min(8 * RET_CHUNK, seq),
        merge_rows=min(512, n_tokens),
        mlp_rows=min(1024, n_tokens), mlp_hidden=1024,
    )


def kernel(x, norm_pre_mix, norm_post_mix, norm_pre_mlp, norm_post_mlp, w_in, conv_w, a_log, dt_bias, gdn_norm,
           ret_norm, w_br_a, w_br_b, w_br_c, w_br_d, w_o, w_mlp_in, w_mlp_out):
    batch, seq, d = x.shape
    tabs = _rope_tables(seq)
    params = _prep_params(w_in, conv_w, a_log, dt_bias, w_br_a, w_br_b, w_br_c, w_br_d, w_o, w_mlp_in, w_mlp_out)
    h = x.reshape(batch * seq, d)
    for l in range(norm_pre_mix.shape[0]):
        h = _layer(h, batch, seq, tabs, norm_pre_mix[l], norm_post_mix[l], norm_pre_mlp[l], norm_post_mlp[l],
                   gdn_norm[l], ret_norm[l], {name: value[l] for name, value in params.items()})
    return h.reshape(batch, seq, d)
```

```python
import functools
import math

import numpy as np
import jax
import jax.numpy as jnp
from jax import lax
from jax.experimental import pallas as pl
from jax.experimental.pallas import tpu as pltpu

F32 = jnp.float32
BF16 = jnp.bfloat16

D_MODEL = 1024
GDN_HEADS, GDN_DK, GDN_DV, GDN_CONV, GDN_CHUNK = 4, 128, 128, 4, 64
DSWA_GROUPS = ((128, 1), (512, 4), (2048, 16))
DSWA_HPG, DSWA_DH, DSWA_BLOCK = 3, 64, 128
DSWA_HEADS = DSWA_HPG * len(DSWA_GROUPS)
DSWA_ROPE_THETA, DSWA_ROPE_DIM = 500000.0, DSWA_DH // 4
SB_HEADS, SB_DH, SB_BLOCK = 8, 64, 128
SB_LOG_ZERO = -110.0
SB_LOG_DEAD = -1e30
SB_QBLOCKS_PER_STEP = 16
DSWA_TOKENS_PER_STEP = 2048
DSWA_CHAIN_GROUP = 12
RET_HEADS, RET_DK, RET_DV, RET_THETA = 4, 64, 128, 10000.0
RET_CHUNK = 128
D_FF = 4 * D_MODEL
NORM_EPS = 1e-6
L2_EPS = 1e-6

_COL_SIZES = (1536, 512, 4, 4, 1728, 1536, 512, 512, 512, 4096)
_COL_OFF = np.concatenate([[0], np.cumsum(_COL_SIZES)])

PM_A_QKV, PM_A_Z, PM_C, PM_D_QK, PM_D_V, PM_D_G, PM_GATES, PM_WIDTH = 0, 1536, 2048, 3584, 4096, 4608, 5120, 9216
DSWA_GW = 640
DSWA_OW = 256

V7X_VMEM_BYTES = 64 * 1024 * 1024
VMEM_LIMIT = V7X_VMEM_BYTES * 3 // 4
LANES = 128

_RET_LOG_GAMMA = [float(np.log1p(-np.exp2(np.float32(-5.0 - h))).astype(np.float32)) for h in range(RET_HEADS)]


def _cparams(*sem):
    return pltpu.CompilerParams(dimension_semantics=sem, vmem_limit_bytes=VMEM_LIMIT)


def _sigmoid(x):
    return 1.0 / (1.0 + jnp.exp(-x))


def _softplus(x):
    return jnp.maximum(x, 0.0) + jnp.log(1.0 + jnp.exp(-jnp.abs(x)))


def _dot(a, b):
    return jnp.dot(a, b, preferred_element_type=F32)


def _dot_nt(a, b):
    return lax.dot_general(a, b, (((1,), (1,)), ((), ())), preferred_element_type=F32)


def _split3(x):
    hi = x.astype(BF16)
    r = x - hi.astype(F32)
    mid = r.astype(BF16)
    lo = (r - mid.astype(F32)).astype(BF16)
    return hi, mid, lo


def _dot_exact_lhs(a_bf16, x):
    hi, mid, lo = _split3(x)
    return _dot(a_bf16, hi) + (_dot(a_bf16, mid) + _dot(a_bf16, lo))


def _split2(x):
    hi = x.astype(BF16)
    lo = (x - hi.astype(F32)).astype(BF16)
    return hi, lo


def _rms(x, w):
    ms = jnp.mean(x * x, axis=-1, keepdims=True)
    return x * lax.rsqrt(ms + NORM_EPS) * w


def _norm_matmul_kernel(x_ref, g_ref, w_ref, o_ref, h_ref):
    @pl.when(pl.program_id(1) == 0)
    def _():
        h_ref[...] = _rms(x_ref[...], g_ref[...]).astype(BF16)

    o_ref[...] = _dot(h_ref[...], w_ref[...]).astype(o_ref.dtype)


def _norm_matmul(x, gain, w, layer, tm, tn):
    T, K = x.shape
    N = w.shape[2]
    return pl.pallas_call(
        _norm_matmul_kernel,
        grid=(T // tm, N // tn),
        in_specs=[pl.BlockSpec((tm, K), lambda i, j: (i, 0)),
                  pl.BlockSpec((1, K), lambda i, j: (0, 0)),
                  pl.BlockSpec((None, K, tn), lambda i, j: (layer, 0, j))],
        out_specs=pl.BlockSpec((tm, tn), lambda i, j: (i, j)),
        out_shape=jax.ShapeDtypeStruct((T, N), BF16),
        scratch_shapes=[pltpu.VMEM((tm, K), BF16)],
        compiler_params=_cparams("parallel", "arbitrary"),
    )(x, gain, w)


def _dswa_proj_kernel(x_ref, g_ref, w_ref, cos_ref, sin_ref, o0_ref, o1_ref, o2_ref, oab_ref):
    h = _rms(x_ref[...], g_ref[...]).astype(BF16)
    p = _dot(h, w_ref[...])
    cos = cos_ref[...]
    sin = sin_ref[...]
    lane = lax.broadcasted_iota(jnp.int32, cos.shape, 1)
    first = (lane & (DSWA_DH - 1)) < (DSWA_ROPE_DIM // 2)
    for g, o_ref in enumerate((o0_ref, o1_ref, o2_ref)):
        base = g * DSWA_GW
        for t in range(3):
            xt = p[:, base + t * LANES: base + (t + 1) * LANES]
            sw = jnp.where(first, pltpu.roll(xt, LANES - DSWA_ROPE_DIM // 2, 1),
                           pltpu.roll(xt, DSWA_ROPE_DIM // 2, 1))
            o_ref[t] = xt * cos + sw * sin
        for t in range(3, DSWA_GW // LANES):
            o_ref[t] = p[:, base + t * LANES: base + (t + 1) * LANES]
    oab_ref[...] = p[:, 3 * DSWA_GW:]


def _dswa_proj(x, gain, w, layer, cos_t, sin_t, seq, tm):
    T, K = x.shape
    N = w.shape[2]
    n_s = seq // tm
    grp = jax.ShapeDtypeStruct((DSWA_GW // LANES, T, LANES), F32)
    return pl.pallas_call(
        _dswa_proj_kernel,
        grid=(T // tm,),
        in_specs=[pl.BlockSpec((tm, K), lambda i: (i, 0)),
                  pl.BlockSpec((1, K), lambda i: (0, 0)),
                  pl.BlockSpec((None, K, N), lambda i: (layer, 0, 0)),
                  pl.BlockSpec((tm, LANES), lambda i: (i % n_s, 0)),
                  pl.BlockSpec((tm, LANES), lambda i: (i % n_s, 0))],
        out_specs=[pl.BlockSpec((DSWA_GW // LANES, tm, LANES), lambda i: (0, i, 0))] * 3
                  + [pl.BlockSpec((tm, LANES), lambda i: (i, 0))],
        out_shape=[grp, grp, grp, jax.ShapeDtypeStruct((T, LANES), F32)],
        compiler_params=_cparams("parallel"),
    )(x, gain, w, cos_t, sin_t)


def _per_head_matmul(xs, p_cat, diag_mask):
    C = GDN_CHUNK
    ph, pl_ = [jnp.concatenate([t] * GDN_HEADS, axis=0) * diag_mask for t in _split2(p_cat)]
    parts = [_split2(x) for x in xs]
    his = [hi for hi, _ in parts]
    by_hi = _dot(jnp.concatenate(his + [lo for _, lo in parts], axis=0), ph)
    by_lo = _dot(jnp.concatenate(his, axis=0), pl_)
    n = len(xs)
    return [by_hi[i * C:(i + 1) * C] + (by_hi[(n + i) * C:(n + i + 1) * C] + by_lo[i * C:(i + 1) * C])
            for i in range(n)]


def _unit_lower_inverse(lows, eye_cat, diag_mask):
    xs = [eye_cat - low for low in lows]
    ps = [_per_head_matmul([low], low, diag_mask)[0] for low in lows]
    yield
    n = 4
    while n < GDN_CHUNK:
        both = [_per_head_matmul([x, p], p, diag_mask) for x, p in zip(xs, ps)]
        xs = [x + xp for x, (xp, _) in zip(xs, both)]
        ps = [pp for _, pp in both]
        n *= 2
        yield
    return [x + _per_head_matmul([x], p, diag_mask)[0] for x, p in zip(xs, ps)]


def _interleave(*gens):
    results = [None] * len(gens)
    live = list(range(len(gens)))
    while live:
        for i in list(live):
            try:
                next(gens[i])
            except StopIteration as stop:
                results[i] = stop.value
                live.remove(i)
    return results


def _gdn_kernel(qkv_ref, z_ref, ab_ref, cw_ref, alog_ref, dtb_ref, nw_ref, o_ref,
                carry_ref, q_s, k_s, v_s, state_ref, uw_s, attn_s, qd_s, kdt_s, egl_s):
    tc = qkv_ref.shape[0]
    C = GDN_CHUNK
    H = GDN_HEADS
    HD = GDN_HEADS * GDN_DK
    step = pl.program_id(1)
    n_tiles = pl.num_programs(1) - 1
    n_chunks = tc // C
    heads = range(H)
    chunks = range(n_chunks)
    ch = [(c, h) for c in chunks for h in heads]
    hsl = [slice(h * GDN_DK, (h + 1) * GDN_DK) for h in heads]
    nw = nw_ref[...]

    @pl.when(step == 0)
    def _():
        carry_ref[0:8, :] = jnp.zeros((8, 3 * HD), F32)
        for ref in (state_ref, uw_s, attn_s, qd_s, kdt_s, egl_s):
            ref[...] = jnp.zeros_like(ref)

    def recurrence():
        st = [state_ref[h] for h in heads]
        uw = {(c, h): uw_s[c * H + h] for c, h in ch}
        attn = {(c, h): attn_s[c * H + h] for c, h in ch}
        q_dec = {(c, h): qd_s[c * H + h] for c, h in ch}
        k_dec_t = {(c, h): kdt_s[c * H + h] for c, h in ch}
        egl = {(c, h): egl_s[c * H + h][0:1, :] for c, h in ch}
        gate = {(c, h): z_ref[c * C:(c + 1) * C, hsl[h]].astype(F32) for c, h in ch}
        yield
        outs = {}
        for c in chunks:
            stb = [st[h].astype(BF16) for h in heads]
            w_st = [_dot(uw[c, h][:, GDN_DV:].astype(BF16), stb[h]) for h in heads]
            q_st = [_dot(q_dec[c, h], stb[h]) for h in heads]
            yield
            vnb = [(uw[c, h][:, :GDN_DV] - w_st[h]).astype(BF16) for h in heads]
            o = [q_st[h] + _dot(attn[c, h], vnb[h]) for h in heads]
            upd = [_dot(k_dec_t[c, h], vnb[h]) for h in heads]
            yield
            for h in heads:
                st[h] = st[h] * egl[c, h] + upd[h]
                zz = gate[c, h]
                outs[c, h] = _rms(o[h], nw) * (zz * _sigmoid(zz))
            yield
        return outs, st

    def store_recurrence(outs, st):
        for c, h in ch:
            o_ref[c * C:(c + 1) * C, hsl[h]] = outs[c, h].astype(o_ref.dtype)
        for h in heads:
            state_ref[h] = st[h]

    @pl.when(step == n_tiles)
    def _():
        store_recurrence(*_interleave(recurrence())[0])

    @pl.when(step < n_tiles)
    def _():
        (outs, st), pre = _interleave(
            recurrence(), _gdn_prepare(qkv_ref, ab_ref, cw_ref, alog_ref, dtb_ref, carry_ref, q_s, k_s, v_s))
        store_recurrence(outs, st)
        for c, h in ch:
            uw_s[c * H + h] = pre["uw"][c, h]
            attn_s[c * H + h] = pre["attn"][c, h]
            qd_s[c * H + h] = pre["q_dec"][c, h]
            kdt_s[c * H + h] = pre["k_dec_t"][c, h]
            egl_s[c * H + h] = jnp.broadcast_to(pre["egl"][c, h], (8, GDN_DK))


def _gdn_prepare(qkv_ref, ab_ref, cw_ref, alog_ref, dtb_ref, carry_ref, q_s, k_s, v_s):
    tc = qkv_ref.shape[0]
    C = GDN_CHUNK
    H = GDN_HEADS
    HD = GDN_HEADS * GDN_DK
    carry_ref[8:, :] = qkv_ref[...].astype(F32)
    for grp, dst in enumerate((q_s, k_s, v_s)):
        cols = slice(grp * HD, (grp + 1) * HD)
        y = carry_ref[8:, cols] * cw_ref[GDN_CONV - 1:GDN_CONV, cols]
        for j in range(1, GDN_CONV):
            y = y + carry_ref[8 - j:8 - j + tc, cols] * cw_ref[GDN_CONV - 1 - j:GDN_CONV - j, cols]
        y = y * _sigmoid(y)
        if grp == 2:
            dst[...] = y
        else:
            for h in range(GDN_HEADS):
                yh = y[:, h * GDN_DK:(h + 1) * GDN_DK]
                yh = yh * lax.rsqrt(jnp.sum(yh * yh, axis=-1, keepdims=True) + L2_EPS)
                if grp == 0:
                    yh = yh * (GDN_DK ** -0.5)
                dst[:, h * GDN_DK:(h + 1) * GDN_DK] = yh
        yield
    carry_ref[0:8, :] = carry_ref[tc:tc + 8, :]

    ri = lax.broadcasted_iota(jnp.int32, (C, C), 0)
    ci = lax.broadcasted_iota(jnp.int32, (C, C), 1)
    incl = ri >= ci
    strict = ri > ci
    a_low = jnp.where(incl, 1.0, 0.0).astype(BF16)
    eye_cat = jnp.concatenate([jnp.where(ri == ci, 1.0, 0.0).astype(F32)] * H, axis=1)
    rb = lax.broadcasted_iota(jnp.int32, (H * C, H * C), 0) // C
    cb = lax.broadcasted_iota(jnp.int32, (H * C, H * C), 1) // C
    diag_mask = jnp.where(rb == cb, 1.0, 0.0).astype(BF16)
    neg_exp_alog = -jnp.exp(alog_ref[...])
    dtb = dtb_ref[...]
    n_chunks = tc // C
    heads = range(H)
    hsl = [slice(h * GDN_DK, (h + 1) * GDN_DK) for h in heads]

    chunks = range(n_chunks)
    ch = [(c, h) for c in chunks for h in heads]
    rows = [slice(c * C, (c + 1) * C) for c in chunks]
    ab = [ab_ref[rows[c], :] for c in chunks]
    gv = [neg_exp_alog * _softplus(ab[c] + dtb) for c in chunks]
    bv = [_sigmoid(ab[c]) for c in chunks]
    gc_all = [_dot_exact_lhs(a_low, gv[c]) for c in chunks]
    gc_t = [gc_all[c].T for c in chunks]
    yield
    q = {(c, h): q_s[rows[c], hsl[h]] for c, h in ch}
    k = {(c, h): k_s[rows[c], hsl[h]] for c, h in ch}
    v = {(c, h): v_s[rows[c], hsl[h]] for c, h in ch}
    gc = {(c, h): jnp.broadcast_to(gc_all[c][:, h:h + 1], (C, GDN_DK)) for c, h in ch}
    beta = {(c, h): jnp.broadcast_to(bv[c][:, H + h:H + h + 1], (C, GDN_DK)) for c, h in ch}
    decay = {(c, h): jnp.exp(jnp.where(incl, gc[c, h][:, :C] - gc_t[c][h:h + 1, :], -jnp.inf)) for c, h in ch}
    egc = {x: jnp.exp(gc[x]) for x in ch}
    gl = {x: gc[x][C - 1:C, :] for x in ch}
    kb = {x: k[x] * beta[x] for x in ch}
    kbf = {x: k[x].astype(BF16) for x in ch}
    kk = {x: _dot_nt(kb[x].astype(BF16), kbf[x]) for x in ch}
    low_cat = [jnp.concatenate([jnp.where(strict, kk[c, h] * decay[c, h], 0.0) for h in heads], axis=1)
               for c in chunks]
    yield
    t_cat = [t.astype(BF16) for t in (yield from _unit_lower_inverse(low_cat, eye_cat, diag_mask))]
    uw = {(c, h): _dot(t_cat[c][:, h * C:(h + 1) * C],
                       jnp.concatenate([v[c, h] * beta[c, h], kb[c, h] * egc[c, h]], axis=1).astype(BF16))
          for c, h in ch}
    yield
    attn = {x: (_dot_nt(q[x].astype(BF16), kbf[x]) * decay[x]).astype(BF16) for x in ch}
    q_dec = {x: (q[x] * egc[x]).astype(BF16) for x in ch}
    k_dec_t = {x: (k[x] * jnp.exp(gl[x] - gc[x])).T.astype(BF16) for x in ch}
    egl = {x: jnp.exp(gl[x]) for x in ch}
    return dict(uw=uw, attn=attn, q_dec=q_dec, k_dec_t=k_dec_t, egl=egl)


def _gdn(proj, ab, conv_w, alog_v, dtb_v, layer, norm_w, batch, seq, tc):
    T = proj.shape[0]
    n_s = seq // tc
    HD = GDN_HEADS * GDN_DK
    n_pairs = (tc // GDN_CHUNK) * GDN_HEADS
    cur = lambda b, s: b * n_s + jnp.minimum(s, n_s - 1)
    lag = lambda b, s: b * n_s + jnp.maximum(s - 1, 0)
    return pl.pallas_call(
        _gdn_kernel,
        grid=(batch, n_s + 1),
        in_specs=[pl.BlockSpec((tc, 3 * HD), lambda b, s: (cur(b, s), PM_A_QKV // (3 * HD))),
                  pl.BlockSpec((tc, HD), lambda b, s: (lag(b, s), PM_A_Z // HD)),
                  pl.BlockSpec((tc, LANES), lambda b, s: (cur(b, s), 0)),
                  pl.BlockSpec((None, 8, 3 * HD), lambda b, s: (layer, 0, 0)),
                  pl.BlockSpec((None, 1, LANES), lambda b, s: (layer, 0, 0)),
                  pl.BlockSpec((None, 1, LANES), lambda b, s: (layer, 0, 0)),
                  pl.BlockSpec((1, GDN_DV), lambda b, s: (0, 0))],
        out_specs=pl.BlockSpec((tc, HD), lambda b, s: (lag(b, s), 0)),
        out_shape=jax.ShapeDtypeStruct((T, HD), BF16),
        scratch_shapes=[pltpu.VMEM((tc + 8, 3 * HD), F32),
                        pltpu.VMEM((tc, HD), F32), pltpu.VMEM((tc, HD), F32), pltpu.VMEM((tc, HD), F32),
                        pltpu.VMEM((GDN_HEADS, GDN_DK, GDN_DV), F32),
                        pltpu.VMEM((n_pairs, GDN_CHUNK, 2 * GDN_DK), F32),
                        pltpu.VMEM((n_pairs, GDN_CHUNK, GDN_CHUNK), BF16),
                        pltpu.VMEM((n_pairs, GDN_CHUNK, GDN_DK), BF16),
                        pltpu.VMEM((n_pairs, GDN_DK, GDN_CHUNK), BF16),
                        pltpu.VMEM((n_pairs, 8, GDN_DK), F32)],
        compiler_params=_cparams("parallel", "arbitrary"),
    )(proj, proj, ab, conv_w, alog_v, dtb_v, norm_w)


def _dswa_kernel(cur_ref, prev_ref, o_ref, lse_ref, *, dil):
    BL = DSWA_BLOCK
    n_blk = cur_ref.shape[1] // (BL * dil)
    n = pl.program_id(1)

    def class_rows(ref, r, blk0, n_blocks):
        start, size = r + blk0 * BL * dil, n_blocks * BL
        rows = pl.ds(start, size, stride=dil) if dil > 1 else slice(start, start + size)
        return rows, jnp.concatenate([ref[t, rows, :] for t in range(ref.shape[0])], axis=1)

    ii = lax.broadcasted_iota(jnp.int32, (BL, 2 * BL), 0)
    jj = lax.broadcasted_iota(jnp.int32, (BL, 2 * BL), 1)
    band = (jj >= ii) & (jj <= ii + BL)
    first_valid = jnp.where(n > 0, 0, BL)
    band_first = band & (jj >= first_valid)
    kw = DSWA_HPG * DSWA_DH
    pad = jnp.zeros((BL, DSWA_OW - kw), F32)
    ones_keys = jnp.ones((2 * BL, DSWA_DH), BF16)
    hsl = [slice(h * DSWA_DH, (h + 1) * DSWA_DH) for h in range(DSWA_HPG)]

    mc = min(n_blk, DSWA_CHAIN_GROUP // DSWA_HPG)
    units = [(r, m0) for r in range(dil) for m0 in range(0, n_blk, mc)]
    per_group = max(1, DSWA_CHAIN_GROUP // (mc * DSWA_HPG))
    for u0 in range(0, len(units), per_group):
        group = units[u0:u0 + per_group]
        rows, cur, k_all, v_all = {}, {}, {}, {}
        for u in group:
            r, m0 = u
            rows[u], cur[u] = class_rows(cur_ref, r, m0, mc)
            before = class_rows(prev_ref, r, 0, 1)[1] if m0 == 0 else class_rows(cur_ref, r, m0 - 1, 1)[1]
            k_all[u] = jnp.concatenate([before[:, kw:2 * kw], cur[u][:, kw:2 * kw]], axis=0).astype(BF16)
            v_all[u] = jnp.concatenate([before[:, 2 * kw:3 * kw], cur[u][:, 2 * kw:3 * kw]], axis=0).astype(BF16)
        chains = [(u, m, h) for u in group for m in range(mc) for h in range(DSWA_HPG)]
        keys = {m: slice(m * BL, (m + 2) * BL) for m in range(mc)}
        q = {(u, m, h): cur[u][m * BL:(m + 1) * BL, hsl[h]].astype(BF16) for u, m, h in chains}
        s = {(u, m, h): jnp.where(band_first if (u[1] == 0 and m == 0) else band,
                                  _dot_nt(q[u, m, h], k_all[u][keys[m], hsl[h]]), -jnp.inf) for u, m, h in chains}
        mx = {x: jnp.max(s[x], axis=-1, keepdims=True) for x in chains}
        p = {x: jnp.exp(s[x] - mx[x]) for x in chains}
        pb = {x: p[x].astype(BF16) for x in chains}
        l = {x: _dot(pb[x], ones_keys) for x in chains}
        o = {(u, m, h): _dot(pb[u, m, h], v_all[u][keys[m], hsl[h]]) * (1.0 / l[u, m, h]) for u, m, h in chains}
        lse = {x: mx[x] + jnp.log(l[x]) for x in chains}
        for u in group:
            o_u = jnp.concatenate([jnp.concatenate([o[u, m, h] for h in range(DSWA_HPG)] + [pad], axis=1)
                                   for m in range(mc)], axis=0)
            lse_u = jnp.concatenate([jnp.concatenate([lse[u, m, h] for h in range(DSWA_HPG)] + [pad], axis=1)
                                     for m in range(mc)], axis=0)
            for t in range(DSWA_OW // LANES):
                o_ref[t, rows[u], :] = o_u[:, t * LANES:(t + 1) * LANES]
                lse_ref[t, rows[u], :] = lse_u[:, t * LANES:(t + 1) * LANES]


def _dswa(grp, batch, seq, dil):
    n_in, T, _ = grp.shape
    n_out = DSWA_OW // LANES
    span = DSWA_BLOCK * dil
    n_blk = max(1, min(DSWA_TOKENS_PER_STEP, seq) // span)
    step = n_blk * span
    n_steps = seq // step
    out = jax.ShapeDtypeStruct((n_out, T, LANES), F32)
    return pl.pallas_call(
        functools.partial(_dswa_kernel, dil=dil),
        grid=(batch, n_steps),
        in_specs=[pl.BlockSpec((n_in, step, LANES), lambda b, n: (0, b * n_steps + n, 0)),
                  pl.BlockSpec((n_in, span, LANES),
                               lambda b, n: (0, b * (seq // span) + jnp.maximum(n * n_blk - 1, 0), 0))],
        out_specs=[pl.BlockSpec((n_out, step, LANES), lambda b, n: (0, b * n_steps + n, 0))] * 2,
        out_shape=[out, out],
        compiler_params=_cparams("parallel", "arbitrary"),
    )(grp, grp)


def _sb_kernel(q_ref, k_ref, v_ref, o_ref, vt_s, acc_s):
    BLK, DH = SB_BLOCK, SB_DH
    W = 2 * BLK
    G = q_ref.shape[0] // BLK
    n_kblocks = k_ref.shape[0] // BLK
    step_id = pl.program_id(2)
    i0 = step_id * G

    @pl.when(step_id == 0)
    def _():
        for j in range(n_kblocks):
            vt_s[j] = v_ref[j * BLK:(j + 1) * BLK, :].astype(F32).T.astype(BF16)

    ri = lax.broadcasted_iota(jnp.int32, (BLK, W), 0)
    ci = lax.broadcasted_iota(jnp.int32, (BLK, W), 1)
    causal = ri < (ci & (BLK - 1))
    r2 = lax.broadcasted_iota(jnp.int32, (BLK, BLK), 0)
    c2 = lax.broadcasted_iota(jnp.int32, (BLK, BLK), 1)
    after = jnp.where(c2 > r2, 1.0, 0.0).astype(BF16)
    after2 = jnp.concatenate([after, after], axis=1)

    head0_feat = lax.broadcasted_iota(jnp.int32, (BLK, LANES), 1) < DH
    q_bd = []
    for g in range(G):
        qg = q_ref[g * BLK:(g + 1) * BLK, :]
        zero = jnp.zeros_like(qg)
        q_bd.append(jnp.concatenate([jnp.where(head0_feat, qg, zero), jnp.where(head0_feat, zero, qg)], axis=0))

    def visit(js, c_rows, mask):
        ks = [k_ref[pl.ds(pl.multiple_of(js[g] * BLK, BLK), BLK), :].astype(BF16) for g in range(G)]
        vts = [vt_s[js[g]] for g in range(G)]
        zs = [_dot_nt(ks[g], q_bd[g]) for g in range(G)]
        sps = [jnp.maximum(z, 0.0) + jnp.log(1.0 + jnp.exp(-jnp.abs(z))) for z in zs]
        sp_ms = sps if mask is None else [jnp.where(mask, sp, 0.0) for sp in sps]
        splits = [_split2(sp_m) for sp_m in sp_ms]
        sufs = [_dot(after2, jnp.concatenate([hi, lo], axis=0)) for hi, lo in splits]
        a_s = [jnp.exp(zs[g] - sps[g] - sufs[g] + c_rows[g]) for g in range(G)]
        if mask is not None:
            a_s = [jnp.where(mask, a, 0.0) for a in a_s]
        pvs = [_dot(vts[g], a_s[g].astype(BF16)) for g in range(G)]
        return pvs, [c_rows[g] - (sufs[g][0:1, :] + sp_ms[g][0:1, :]) for g in range(G)]

    pvs, cs = visit([i0 + g for g in range(G)], [jnp.zeros((1, W), F32)] * G, causal)
    for g in range(G):
        acc_s[g] = pvs[g]

    def cond(state):
        d, live, _ = state
        return (d < i0 + G) & live

    def body(state):
        d, _, cs = state
        js = [i0 + g - d for g in range(G)]
        c_in = [jnp.where(js[g] < 0, SB_LOG_DEAD, cs[g]) for g in range(G)]
        accs = [acc_s[g] for g in range(G)]
        pvs, new = visit([jnp.maximum(j, 0) for j in js], c_in, None)
        for g in range(G):
            acc_s[g] = accs[g] + pvs[g]
        c_max = jnp.max(functools.reduce(jnp.maximum, new))
        return d + 1, c_max > SB_LOG_ZERO, tuple(new)

    lax.while_loop(cond, body, (jnp.int32(1), jnp.bool_(True), tuple(cs)))
    for g in range(G):
        acc = acc_s[g]
        o_ref[g * BLK:(g + 1) * BLK, :] = jnp.concatenate([acc[:DH, :BLK], acc[DH:, BLK:]],
                                                          axis=0).T.astype(o_ref.dtype)


def _sb(proj, batch, seq):
    T = proj.shape[0]
    nq = seq // SB_BLOCK
    G = min(SB_QBLOCKS_PER_STEP, nq)
    n_steps = nq // G
    pairs = SB_HEADS // 2
    qo, ko, vo = PM_C // LANES, (PM_C + SB_HEADS * SB_DH) // LANES, (PM_C + 2 * SB_HEADS * SB_DH) // LANES
    return pl.pallas_call(
        _sb_kernel,
        grid=(batch, pairs, n_steps),
        in_specs=[pl.BlockSpec((G * SB_BLOCK, LANES), lambda b, p, i: (b * n_steps + i, qo + p)),
                  pl.BlockSpec((seq, LANES), lambda b, p, i: (b, ko + p)),
                  pl.BlockSpec((seq, LANES), lambda b, p, i: (b, vo + p))],
        out_specs=pl.BlockSpec((G * SB_BLOCK, LANES), lambda b, p, i: (b * n_steps + i, p)),
        out_shape=jax.ShapeDtypeStruct((T, SB_HEADS * SB_DH), BF16),
        scratch_shapes=[pltpu.VMEM((nq, LANES, SB_BLOCK), BF16),
                        pltpu.VMEM((G, LANES, 2 * SB_BLOCK), F32)],
        compiler_params=_cparams("parallel", "parallel", "arbitrary"),
    )(proj, proj, proj)


def _ret_kernel(qk_ref, v_ref, g_ref, cos_ref, sin_ref, nw_ref, o_ref, r_ref):
    tc = qk_ref.shape[0]
    C = RET_CHUNK
    half = RET_DK // 2

    @pl.when(pl.program_id(1) == 0)
    def _():
        r_ref[...] = jnp.zeros_like(r_ref)

    lane = lax.broadcasted_iota(jnp.int32, (C, LANES), 1)
    first = (lane & (RET_DK - 1)) < half
    ri = lax.broadcasted_iota(jnp.int32, (C, C), 0)
    ci = lax.broadcasted_iota(jnp.int32, (C, C), 1)
    diff = (ri - ci).astype(F32)
    causal = ri >= ci
    rowf = lax.broadcasted_iota(jnp.int32, (C, LANES), 0).astype(F32)
    nw = nw_ref[...]

    heads = range(RET_HEADS)
    chunks = range(tc // C)
    ch = [(c, h) for c in chunks for h in heads]
    rows = [slice(c * C, (c + 1) * C) for c in chunks]
    hsl = [slice(h * RET_DV, (h + 1) * RET_DV) for h in heads]
    dmat = [jnp.where(causal, jnp.exp(diff * _RET_LOG_GAMMA[h]), 0.0) for h in heads]
    xi = [jnp.exp((rowf + 1.0) * _RET_LOG_GAMMA[h]) for h in heads]
    zeta = [jnp.exp((C - 1.0 - rowf[:, :RET_DK]) * _RET_LOG_GAMMA[h]) for h in heads]

    def rope(x, c):
        sw = jnp.where(first, pltpu.roll(x, LANES - half, 1), pltpu.roll(x, half, 1))
        return x * cos_ref[rows[c], :] + sw * sin_ref[rows[c], :]

    kw = RET_HEADS * RET_DK
    qp = {(c, p): rope(qk_ref[rows[c], p * LANES:(p + 1) * LANES].astype(F32), c)
          for c in chunks for p in range(RET_HEADS // 2)}
    kp = {(c, p): rope(qk_ref[rows[c], kw + p * LANES:kw + (p + 1) * LANES].astype(F32), c)
          for c in chunks for p in range(RET_HEADS // 2)}
    q = {(c, h): qp[c, h // 2][:, (h % 2) * RET_DK:(h % 2 + 1) * RET_DK].astype(BF16) for c, h in ch}
    k = {(c, h): kp[c, h // 2][:, (h % 2) * RET_DK:(h % 2 + 1) * RET_DK] for c, h in ch}
    vb = {(c, h): v_ref[rows[c], hsl[h]].astype(BF16) for c, h in ch}
    gate = {(c, h): g_ref[rows[c], hsl[h]].astype(F32) for c, h in ch}
    intra = {(c, h): (_dot_nt(q[c, h], k[c, h].astype(BF16)) * dmat[h]).astype(BF16) for c, h in ch}
    delta = {(c, h): _dot((k[c, h] * zeta[h]).T.astype(BF16), vb[c, h]) for c, h in ch}
    r_in = {}
    for h in heads:
        r = r_ref[h]
        for c in chunks:
            r_in[c, h] = r.astype(BF16)
            r = r * math.exp(C * _RET_LOG_GAMMA[h]) + delta[c, h]
        r_ref[h] = r
    o = {(c, h): _dot(intra[c, h], vb[c, h]) + _dot(q[c, h], r_in[c, h]) * xi[h] for c, h in ch}
    for c, h in ch:
        gg = gate[c, h]
        o_ref[rows[c], hsl[h]] = (_rms(o[c, h], nw) * (gg * _sigmoid(gg))).astype(o_ref.dtype)


def _ret(proj, cos_t, sin_t, norm_w, batch, seq, tc):
    T = proj.shape[0]
    n_s = seq // tc
    W = RET_HEADS * RET_DV
    return pl.pallas_call(
        _ret_kernel,
        grid=(batch, n_s),
        in_specs=[pl.BlockSpec((tc, W), lambda b, s: (b * n_s + s, PM_D_QK // W)),
                  pl.BlockSpec((tc, W), lambda b, s: (b * n_s + s, PM_D_V // W)),
                  pl.BlockSpec((tc, W), lambda b, s: (b * n_s + s, PM_D_G // W)),
                  pl.BlockSpec((tc, LANES), lambda b, s: (s, 0)),
                  pl.BlockSpec((tc, LANES), lambda b, s: (s, 0)),
                  pl.BlockSpec((1, RET_DV), lambda b, s: (0, 0))],
        out_specs=pl.BlockSpec((tc, W), lambda b, s: (b * n_s + s, 0)),
        out_shape=jax.ShapeDtypeStruct((T, W), BF16),
        scratch_shapes=[pltpu.VMEM((RET_HEADS, RET_DK, RET_DV), F32)],
        compiler_params=_cparams("parallel", "arbitrary"),
    )(proj, proj, proj, cos_t, sin_t, norm_w)


def _merge_kernel(x_ref, oa_ref, ob0_ref, ob1_ref, ob2_ref, l0_ref, l1_ref, l2_ref, oc_ref, od_ref,
                  ga_ref, gb_ref, gc_ref, gd_ref, wa_ref, wb_ref, wc_ref, wd_ref, wo_ref, nw_ref, out_ref):
    def slabs(ref):
        return jnp.concatenate([ref[t] for t in range(ref.shape[0])], axis=1)

    l0, l1, l2 = slabs(l0_ref), slabs(l1_ref), slabs(l2_ref)
    m = jnp.maximum(jnp.maximum(l0, l1), l2)
    e0, e1, e2 = jnp.exp(l0 - m), jnp.exp(l1 - m), jnp.exp(l2 - m)
    inv = 1.0 / (e0 + e1 + e2)
    ob = jnp.concatenate([slabs(ob0_ref) * (e0 * inv), slabs(ob1_ref) * (e1 * inv), slabs(ob2_ref) * (e2 * inv)],
                         axis=1).astype(BF16)
    y = _sigmoid(ga_ref[...].astype(F32)) * _dot(oa_ref[...], wa_ref[...])
    y = y + _sigmoid(gb_ref[...].astype(F32)) * _dot(ob, wb_ref[...])
    y = y + _sigmoid(gc_ref[...].astype(F32)) * _dot(oc_ref[...], wc_ref[...])
    y = y + _sigmoid(gd_ref[...].astype(F32)) * _dot(od_ref[...], wd_ref[...])
    mixed = _dot(y.astype(BF16), wo_ref[...])
    out_ref[...] = x_ref[...] + _rms(mixed, nw_ref[...])


def _merge(x, oa, obs, lses, oc, od, proj, wa, wb, wc, wd, wo, layer, nw, tm):
    T, D = x.shape
    row = lambda w: pl.BlockSpec((tm, w), lambda i: (i, 0))
    gate = lambda br: pl.BlockSpec((tm, D), lambda i: (i, PM_GATES // D + br))
    full = lambda a: pl.BlockSpec((None,) + a.shape[1:], lambda i: (layer, 0, 0))
    return pl.pallas_call(
        _merge_kernel,
        grid=(T // tm,),
        in_specs=[row(D), row(oa.shape[1])]
                 + [pl.BlockSpec((DSWA_OW // LANES, tm, LANES), lambda i: (0, i, 0))] * 6
                 + [row(oc.shape[1]), row(od.shape[1])]
                 + [gate(0), gate(1), gate(2), gate(3)] + [full(wa), full(wb), full(wc), full(wd), full(wo)]
                 + [pl.BlockSpec(nw.shape, lambda i: (0, 0))],
        out_specs=row(D),
        out_shape=jax.ShapeDtypeStruct((T, D), F32),
        compiler_params=_cparams("parallel"),
    )(x, oa, *obs, *lses, oc, od, proj, proj, proj, proj, wa, wb, wc, wd, wo, nw)


def _mlp_kernel(x_ref, n1_ref, w1_ref, w2_ref, n2_ref, out_ref, h_ref, acc_ref):
    f = pl.program_id(1)

    @pl.when(f == 0)
    def _():
        h_ref[...] = _rms(x_ref[...], n1_ref[...]).astype(BF16)
        acc_ref[...] = jnp.zeros_like(acc_ref)

    hid = jnp.maximum(_dot(h_ref[...], w1_ref[...]), 0.0)
    acc_ref[...] += _dot((hid * hid).astype(BF16), w2_ref[...])

    @pl.when(f == pl.num_programs(1) - 1)
    def _():
        out_ref[...] = x_ref[...] + _rms(acc_ref[...], n2_ref[...])


def _mlp(x, n1, w1, w2, layer, n2, tm, tf):
    T, D = x.shape
    F = w1.shape[2]
    return pl.pallas_call(
        _mlp_kernel,
        grid=(T // tm, F // tf),
        in_specs=[pl.BlockSpec((tm, D), lambda i, f: (i, 0)),
                  pl.BlockSpec((1, D), lambda i, f: (0, 0)),
                  pl.BlockSpec((None, D, tf), lambda i, f: (layer, 0, f)),
                  pl.BlockSpec((None, tf, D), lambda i, f: (layer, f, 0)),
                  pl.BlockSpec((1, D), lambda i, f: (0, 0))],
        out_specs=pl.BlockSpec((tm, D), lambda i, f: (i, 0)),
        out_shape=jax.ShapeDtypeStruct((T, D), F32),
        scratch_shapes=[pltpu.VMEM((tm, D), BF16), pltpu.VMEM((tm, D), F32)],
        compiler_params=_cparams("parallel", "arbitrary"),
    )(x, n1, w1, w2, n2)


def _prep_w_in(w_in):
    sec = [w_in[..., _COL_OFF[i]:_COL_OFF[i + 1]] for i in range(len(_COL_SIZES))]
    a_qkv, a_z, a_a, a_b, b_qkv, c_qkv, d_qk, d_v, d_g, gates = sec
    sbw = SB_HEADS * SB_DH
    c_qkv = jnp.concatenate([c_qkv[..., :sbw] * SB_DH ** -0.5, c_qkv[..., sbw:]], axis=-1)
    rw = RET_HEADS * RET_DK
    d_qk = jnp.concatenate([d_qk[..., :rw], d_qk[..., rw:] * RET_DK ** -0.5], axis=-1)
    w_main = jnp.concatenate([a_qkv, a_z, c_qkv, d_qk, d_v, d_g, gates], axis=-1).astype(BF16)
    bw = DSWA_HEADS * DSWA_DH
    gw = DSWA_HPG * DSWA_DH
    lead = w_in.shape[:-1]
    groups = []
    for g in range(len(DSWA_GROUPS)):
        q = b_qkv[..., g * gw:(g + 1) * gw] * DSWA_DH ** -0.5
        k = b_qkv[..., bw + g * gw: bw + (g + 1) * gw]
        v = b_qkv[..., 2 * bw + g * gw: 2 * bw + (g + 1) * gw]
        groups += [q, k, v, jnp.zeros(lead + (DSWA_GW - 3 * gw,), F32)]
    w_b = jnp.concatenate(groups + [a_a, a_b, jnp.zeros(lead + (LANES - 2 * GDN_HEADS,), F32)],
                          axis=-1).astype(BF16)
    return w_main, w_b


def _prep_params(w_in, conv_w, a_log, dt_bias, w_br_a, w_br_b, w_br_c, w_br_d, w_o, w_mlp_in, w_mlp_out):
    n_layers = w_in.shape[0]
    w_main, w_b = _prep_w_in(w_in)
    gw = DSWA_HPG * DSWA_DH
    wb = jnp.concatenate(
        [jnp.concatenate([w_br_b[:, g * gw:(g + 1) * gw], jnp.zeros((n_layers, DSWA_OW - gw, D_MODEL), F32)], axis=1)
         for g in range(len(DSWA_GROUPS))], axis=1).astype(BF16)
    lane_vec = lambda v: jnp.zeros((n_layers, 1, LANES), F32).at[:, 0, :v.shape[1]].set(v.astype(F32))
    return dict(
        w_main=w_main, w_b=w_b,
        cw=jnp.zeros((n_layers, 8, conv_w.shape[2]), F32).at[:, :GDN_CONV].set(conv_w),
        alog=lane_vec(a_log), dtb=lane_vec(dt_bias),
        wa=w_br_a.astype(BF16), wb=wb, wc=w_br_c.astype(BF16), wd=w_br_d.astype(BF16), wo=w_o.astype(BF16),
        w1=w_mlp_in.astype(BF16), w2=w_mlp_out.astype(BF16))


def _rope_tables(seq):
    pos = jnp.arange(seq, dtype=jnp.int32).astype(F32)[:, None]
    inv_b = DSWA_ROPE_THETA ** (-jnp.arange(0, DSWA_ROPE_DIM, 2, dtype=F32) / DSWA_ROPE_DIM)
    ang = pos * inv_b[None, :]
    cb, sb = jnp.cos(ang), jnp.sin(ang)
    rest = DSWA_DH - DSWA_ROPE_DIM
    cos_b = jnp.tile(jnp.concatenate([cb, cb, jnp.ones((seq, rest), F32)], axis=1), (1, LANES // DSWA_DH))
    sin_b = jnp.tile(jnp.concatenate([-sb, sb, jnp.zeros((seq, rest), F32)], axis=1), (1, LANES // DSWA_DH))
    inv_r = RET_THETA ** (-jnp.linspace(0.0, 1.0, RET_DK // 2, dtype=F32))
    ang = pos * inv_r[None, :]
    cr, sr = jnp.cos(ang), jnp.sin(ang)
    cos_r = jnp.tile(jnp.concatenate([cr, cr], axis=1), (1, LANES // RET_DK))
    sin_r = jnp.tile(jnp.concatenate([-sr, sr], axis=1), (1, LANES // RET_DK))
    return cos_b, sin_b, cos_r, sin_r


def _layer(x, batch, seq, tabs, n_pre_mix, n_post_mix, n_pre_mlp, n_post_mlp, gdn_norm, ret_norm, p, layer):
    T = x.shape[0]
    tiles = _tile_sizes(T, seq)
    cos_b, sin_b, cos_r, sin_r = tabs
    gain = n_pre_mix.reshape(1, -1)
    proj = _norm_matmul(x, gain, p["w_main"], layer, tm=tiles["proj_rows"], tn=tiles["proj_cols"])
    g0, g1, g2, ab = _dswa_proj(x, gain, p["w_b"], layer, cos_b, sin_b, seq, tm=tiles["dswa_proj_rows"])

    oa = _gdn(proj, ab, p["cw"], p["alog"], p["dtb"], layer, gdn_norm.reshape(1, -1), batch, seq,
              tc=tiles["gdn_rows"])
    obs, lses = [], []
    for grp, (_, dil) in zip((g0, g1, g2), DSWA_GROUPS):
        o, lse = _dswa(grp, batch, seq, dil)
        obs.append(o)
        lses.append(lse)
    oc = _sb(proj, batch, seq)
    od = _ret(proj, cos_r, sin_r, ret_norm.reshape(1, -1), batch, seq, tc=tiles["ret_rows"])

    x1 = _merge(x, oa, obs, lses, oc, od, proj, p["wa"], p["wb"], p["wc"], p["wd"], p["wo"], layer,
                n_post_mix.reshape(1, -1), tm=tiles["merge_rows"])
    return _mlp(x1, n_pre_mlp.reshape(1, -1), p["w1"], p["w2"], layer, n_post_mlp.reshape(1, -1),
                tm=tiles["mlp_rows"], tf=tiles["mlp_hidden"])


def _tile_sizes(n_tokens, seq):
    return dict(
        proj_rows=min(1024, n_tokens), proj_cols=3072,
        dswa_proj_rows=min(1024, seq),
        gdn_rows=min(4 * GDN_CHUNK, seq),
        ret_rows=min(8 * RET_CHUNK, seq),
        merge_rows=min(512, n_tokens),
        mlp_rows=min(1024, n_tokens), mlp_hidden=1024,
    )


def kernel(x, norm_pre_mix, norm_post_mix, norm_pre_mlp, norm_post_mlp, w_in, conv_w, a_log, dt_bias, gdn_norm,
           ret_norm, w_br_a, w_br_b, w_br_c, w_br_d, w_o, w_mlp_in, w_mlp_out):
    batch, seq, d = x.shape
    tabs = _rope_tables(seq)
    params = _prep_params(w_in, conv_w, a_log, dt_bias, w_br_a, w_br_b, w_br_c, w_br_d, w_o, w_mlp_in, w_mlp_out)
    h = x.reshape(batch * seq, d)
    for l in range(norm_pre_mix.shape[0]):
        h = _layer(h, batch, seq, tabs, norm_pre_mix[l], norm_post_mix[l], norm_pre_mlp[l], norm_post_mlp[l],
                   gdn_norm[l], ret_norm[l], params, l)
    return h.reshape(batch, seq, d)
```

```python
import functools
import math

import numpy as np
import jax
import jax.numpy as jnp
from jax import lax
from jax.experimental import pallas as pl
from jax.experimental.pallas import tpu as pltpu

F32 = jnp.float32
BF16 = jnp.bfloat16

D_MODEL = 1024
GDN_HEADS, GDN_DK, GDN_DV, GDN_CONV, GDN_CHUNK = 4, 128, 128, 4, 64
DSWA_GROUPS = ((128, 1), (512, 4), (2048, 16))
DSWA_HPG, DSWA_DH, DSWA_BLOCK = 3, 64, 128
DSWA_HEADS = DSWA_HPG * len(DSWA_GROUPS)
DSWA_ROPE_THETA, DSWA_ROPE_DIM = 500000.0, DSWA_DH // 4
SB_HEADS, SB_DH, SB_BLOCK = 8, 64, 128
SB_LOG_ZERO = -110.0
SB_LOG_DEAD = -1e30
SB_QBLOCKS_PER_STEP = 16
DSWA_TOKENS_PER_STEP = 2048
DSWA_CHAIN_GROUP = 12
RET_HEADS, RET_DK, RET_DV, RET_THETA = 4, 64, 128, 10000.0
RET_CHUNK = 128
D_FF = 4 * D_MODEL
NORM_EPS = 1e-6
L2_EPS = 1e-6

_COL_SIZES = (1536, 512, 4, 4, 1728, 1536, 512, 512, 512, 4096)
_COL_OFF = np.concatenate([[0], np.cumsum(_COL_SIZES)])

PM_A_QKV, PM_A_Z, PM_C, PM_D_QK, PM_D_V, PM_D_G, PM_GATES, PM_WIDTH = 0, 1536, 2048, 3584, 4096, 4608, 5120, 9216
DSWA_GW = 640
DSWA_OW = 256

V7X_VMEM_BYTES = 64 * 1024 * 1024
VMEM_LIMIT = V7X_VMEM_BYTES * 3 // 4
LANES = 128

_RET_LOG_GAMMA = [float(np.log1p(-np.exp2(np.float32(-5.0 - h))).astype(np.float32)) for h in range(RET_HEADS)]


def _cparams(*sem):
    return pltpu.CompilerParams(dimension_semantics=sem, vmem_limit_bytes=VMEM_LIMIT)


def _sigmoid(x):
    return 1.0 / (1.0 + jnp.exp(-x))


def _softplus(x):
    return jnp.maximum(x, 0.0) + jnp.log(1.0 + jnp.exp(-jnp.abs(x)))


def _dot(a, b):
    return jnp.dot(a, b, preferred_element_type=F32)


def _dot_nt(a, b):
    return lax.dot_general(a, b, (((1,), (1,)), ((), ())), preferred_element_type=F32)


def _split3(x):
    hi = x.astype(BF16)
    r = x - hi.astype(F32)
    mid = r.astype(BF16)
    lo = (r - mid.astype(F32)).astype(BF16)
    return hi, mid, lo


def _dot_exact_lhs(a_bf16, x):
    hi, mid, lo = _split3(x)
    return _dot(a_bf16, hi) + (_dot(a_bf16, mid) + _dot(a_bf16, lo))


def _split2(x):
    hi = x.astype(BF16)
    lo = (x - hi.astype(F32)).astype(BF16)
    return hi, lo


def _rms(x, w):
    ms = jnp.mean(x * x, axis=-1, keepdims=True)
    return x * lax.rsqrt(ms + NORM_EPS) * w


def _matmul_kernel(h_ref, w_ref, o_ref):
    o_ref[...] = _dot(h_ref[...], w_ref[...]).astype(o_ref.dtype)


def _matmul(h, w, layer, tm, tn):
    T, K = h.shape
    N = w.shape[2]
    return pl.pallas_call(
        _matmul_kernel,
        grid=(T // tm, N // tn),
        in_specs=[pl.BlockSpec((tm, K), lambda i, j: (i, 0)),
                  pl.BlockSpec((None, K, tn), lambda i, j: (layer, 0, j))],
        out_specs=pl.BlockSpec((tm, tn), lambda i, j: (i, j)),
        out_shape=jax.ShapeDtypeStruct((T, N), BF16),
        compiler_params=_cparams("parallel", "parallel"),
    )(h, w)


def _dswa_proj_kernel(x_ref, g_ref, w_ref, cos_ref, sin_ref, oh_ref, o0_ref, o1_ref, o2_ref, oab_ref):
    h = _rms(x_ref[...], g_ref[...]).astype(BF16)
    oh_ref[...] = h
    p = _dot(h, w_ref[...])
    cos = cos_ref[...]
    sin = sin_ref[...]
    lane = lax.broadcasted_iota(jnp.int32, cos.shape, 1)
    first = (lane & (DSWA_DH - 1)) < (DSWA_ROPE_DIM // 2)
    for g, o_ref in enumerate((o0_ref, o1_ref, o2_ref)):
        base = g * DSWA_GW
        for t in range(3):
            xt = p[:, base + t * LANES: base + (t + 1) * LANES]
            sw = jnp.where(first, pltpu.roll(xt, LANES - DSWA_ROPE_DIM // 2, 1),
                           pltpu.roll(xt, DSWA_ROPE_DIM // 2, 1))
            o_ref[t] = xt * cos + sw * sin
        for t in range(3, DSWA_GW // LANES):
            o_ref[t] = p[:, base + t * LANES: base + (t + 1) * LANES]
    oab_ref[...] = p[:, 3 * DSWA_GW:]


def _dswa_proj(x, gain, w, layer, cos_t, sin_t, seq, tm):
    T, K = x.shape
    N = w.shape[2]
    n_s = seq // tm
    grp = jax.ShapeDtypeStruct((DSWA_GW // LANES, T, LANES), F32)
    return pl.pallas_call(
        _dswa_proj_kernel,
        grid=(T // tm,),
        in_specs=[pl.BlockSpec((tm, K), lambda i: (i, 0)),
                  pl.BlockSpec((1, K), lambda i: (0, 0)),
                  pl.BlockSpec((None, K, N), lambda i: (layer, 0, 0)),
                  pl.BlockSpec((tm, LANES), lambda i: (i % n_s, 0)),
                  pl.BlockSpec((tm, LANES), lambda i: (i % n_s, 0))],
        out_specs=[pl.BlockSpec((tm, K), lambda i: (i, 0))]
                  + [pl.BlockSpec((DSWA_GW // LANES, tm, LANES), lambda i: (0, i, 0))] * 3
                  + [pl.BlockSpec((tm, LANES), lambda i: (i, 0))],
        out_shape=[jax.ShapeDtypeStruct((T, K), BF16), grp, grp, grp, jax.ShapeDtypeStruct((T, LANES), F32)],
        compiler_params=_cparams("parallel"),
    )(x, gain, w, cos_t, sin_t)


def _per_head_matmul(xs, p_cat, diag_mask):
    C = GDN_CHUNK
    ph, pl_ = [jnp.concatenate([t] * GDN_HEADS, axis=0) * diag_mask for t in _split2(p_cat)]
    parts = [_split2(x) for x in xs]
    his = [hi for hi, _ in parts]
    by_hi = _dot(jnp.concatenate(his + [lo for _, lo in parts], axis=0), ph)
    by_lo = _dot(jnp.concatenate(his, axis=0), pl_)
    n = len(xs)
    return [by_hi[i * C:(i + 1) * C] + (by_hi[(n + i) * C:(n + i + 1) * C] + by_lo[i * C:(i + 1) * C])
            for i in range(n)]


def _unit_lower_inverse(lows, eye_cat, diag_mask):
    xs = [eye_cat - low for low in lows]
    ps = [_per_head_matmul([low], low, diag_mask)[0] for low in lows]
    yield
    n = 4
    while n < GDN_CHUNK:
        both = [_per_head_matmul([x, p], p, diag_mask) for x, p in zip(xs, ps)]
        xs = [x + xp for x, (xp, _) in zip(xs, both)]
        ps = [pp for _, pp in both]
        n *= 2
        yield
    return [x + _per_head_matmul([x], p, diag_mask)[0] for x, p in zip(xs, ps)]


def _interleave(*gens):
    results = [None] * len(gens)
    live = list(range(len(gens)))
    while live:
        for i in list(live):
            try:
                next(gens[i])
            except StopIteration as stop:
                results[i] = stop.value
                live.remove(i)
    return results


def _gdn_kernel(qkv_ref, z_ref, ab_ref, cw_ref, alog_ref, dtb_ref, nw_ref, o_ref,
                carry_ref, q_s, k_s, v_s, state_ref, uw_s, attn_s, qd_s, kdt_s, egl_s):
    tc = qkv_ref.shape[0]
    C = GDN_CHUNK
    H = GDN_HEADS
    HD = GDN_HEADS * GDN_DK
    step = pl.program_id(1)
    n_tiles = pl.num_programs(1) - 1
    n_chunks = tc // C
    heads = range(H)
    chunks = range(n_chunks)
    ch = [(c, h) for c in chunks for h in heads]
    hsl = [slice(h * GDN_DK, (h + 1) * GDN_DK) for h in heads]
    nw = nw_ref[...]

    @pl.when(step == 0)
    def _():
        carry_ref[0:8, :] = jnp.zeros((8, 3 * HD), F32)
        for ref in (state_ref, uw_s, attn_s, qd_s, kdt_s, egl_s):
            ref[...] = jnp.zeros_like(ref)

    def recurrence():
        st = [state_ref[h] for h in heads]
        uw = {(c, h): uw_s[c * H + h] for c, h in ch}
        attn = {(c, h): attn_s[c * H + h] for c, h in ch}
        q_dec = {(c, h): qd_s[c * H + h] for c, h in ch}
        k_dec_t = {(c, h): kdt_s[c * H + h] for c, h in ch}
        egl = {(c, h): egl_s[c * H + h][0:1, :] for c, h in ch}
        gate = {(c, h): z_ref[c * C:(c + 1) * C, hsl[h]].astype(F32) for c, h in ch}
        yield
        outs = {}
        for c in chunks:
            stb = [st[h].astype(BF16) for h in heads]
            w_st = [_dot(uw[c, h][:, GDN_DV:].astype(BF16), stb[h]) for h in heads]
            q_st = [_dot(q_dec[c, h], stb[h]) for h in heads]
            yield
            vnb = [(uw[c, h][:, :GDN_DV] - w_st[h]).astype(BF16) for h in heads]
            o = [q_st[h] + _dot(attn[c, h], vnb[h]) for h in heads]
            upd = [_dot(k_dec_t[c, h], vnb[h]) for h in heads]
            yield
            for h in heads:
                st[h] = st[h] * egl[c, h] + upd[h]
                zz = gate[c, h]
                outs[c, h] = _rms(o[h], nw) * (zz * _sigmoid(zz))
            yield
        return outs, st

    def store_recurrence(outs, st):
        for c, h in ch:
            o_ref[c * C:(c + 1) * C, hsl[h]] = outs[c, h].astype(o_ref.dtype)
        for h in heads:
            state_ref[h] = st[h]

    @pl.when(step == n_tiles)
    def _():
        store_recurrence(*_interleave(recurrence())[0])

    @pl.when(step < n_tiles)
    def _():
        (outs, st), pre = _interleave(
            recurrence(), _gdn_prepare(qkv_ref, ab_ref, cw_ref, alog_ref, dtb_ref, carry_ref, q_s, k_s, v_s))
        store_recurrence(outs, st)
        for c, h in ch:
            uw_s[c * H + h] = pre["uw"][c, h]
            attn_s[c * H + h] = pre["attn"][c, h]
            qd_s[c * H + h] = pre["q_dec"][c, h]
            kdt_s[c * H + h] = pre["k_dec_t"][c, h]
            egl_s[c * H + h] = jnp.broadcast_to(pre["egl"][c, h], (8, GDN_DK))


def _gdn_prepare(qkv_ref, ab_ref, cw_ref, alog_ref, dtb_ref, carry_ref, q_s, k_s, v_s):
    tc = qkv_ref.shape[0]
    C = GDN_CHUNK
    H = GDN_HEADS
    HD = GDN_HEADS * GDN_DK
    carry_ref[8:, :] = qkv_ref[...].astype(F32)
    for grp, dst in enumerate((q_s, k_s, v_s)):
        cols = slice(grp * HD, (grp + 1) * HD)
        y = carry_ref[8:, cols] * cw_ref[GDN_CONV - 1:GDN_CONV, cols]
        for j in range(1, GDN_CONV):
            y = y + carry_ref[8 - j:8 - j + tc, cols] * cw_ref[GDN_CONV - 1 - j:GDN_CONV - j, cols]
        y = y * _sigmoid(y)
        if grp == 2:
            dst[...] = y
        else:
            for h in range(GDN_HEADS):
                yh = y[:, h * GDN_DK:(h + 1) * GDN_DK]
                yh = yh * lax.rsqrt(jnp.sum(yh * yh, axis=-1, keepdims=True) + L2_EPS)
                if grp == 0:
                    yh = yh * (GDN_DK ** -0.5)
                dst[:, h * GDN_DK:(h + 1) * GDN_DK] = yh
        yield
    carry_ref[0:8, :] = carry_ref[tc:tc + 8, :]

    ri = lax.broadcasted_iota(jnp.int32, (C, C), 0)
    ci = lax.broadcasted_iota(jnp.int32, (C, C), 1)
    incl = ri >= ci
    strict = ri > ci
    a_low = jnp.where(incl, 1.0, 0.0).astype(BF16)
    eye_cat = jnp.concatenate([jnp.where(ri == ci, 1.0, 0.0).astype(F32)] * H, axis=1)
    rb = lax.broadcasted_iota(jnp.int32, (H * C, H * C), 0) // C
    cb = lax.broadcasted_iota(jnp.int32, (H * C, H * C), 1) // C
    diag_mask = jnp.where(rb == cb, 1.0, 0.0).astype(BF16)
    neg_exp_alog = -jnp.exp(alog_ref[...])
    dtb = dtb_ref[...]
    n_chunks = tc // C
    heads = range(H)
    hsl = [slice(h * GDN_DK, (h + 1) * GDN_DK) for h in heads]

    chunks = range(n_chunks)
    ch = [(c, h) for c in chunks for h in heads]
    rows = [slice(c * C, (c + 1) * C) for c in chunks]
    ab = [ab_ref[rows[c], :] for c in chunks]
    gv = [neg_exp_alog * _softplus(ab[c] + dtb) for c in chunks]
    bv = [_sigmoid(ab[c]) for c in chunks]
    gc_all = [_dot_exact_lhs(a_low, gv[c]) for c in chunks]
    gc_t = [gc_all[c].T for c in chunks]
    yield
    q = {(c, h): q_s[rows[c], hsl[h]] for c, h in ch}
    k = {(c, h): k_s[rows[c], hsl[h]] for c, h in ch}
    v = {(c, h): v_s[rows[c], hsl[h]] for c, h in ch}
    gc = {(c, h): jnp.broadcast_to(gc_all[c][:, h:h + 1], (C, GDN_DK)) for c, h in ch}
    beta = {(c, h): jnp.broadcast_to(bv[c][:, H + h:H + h + 1], (C, GDN_DK)) for c, h in ch}
    decay = {(c, h): jnp.exp(jnp.where(incl, gc[c, h][:, :C] - gc_t[c][h:h + 1, :], -jnp.inf)) for c, h in ch}
    egc = {x: jnp.exp(gc[x]) for x in ch}
    gl = {x: gc[x][C - 1:C, :] for x in ch}
    kb = {x: k[x] * beta[x] for x in ch}
    kbf = {x: k[x].astype(BF16) for x in ch}
    kk = {x: _dot_nt(kb[x].astype(BF16), kbf[x]) for x in ch}
    low_cat = [jnp.concatenate([jnp.where(strict, kk[c, h] * decay[c, h], 0.0) for h in heads], axis=1)
               for c in chunks]
    yield
    t_cat = [t.astype(BF16) for t in (yield from _unit_lower_inverse(low_cat, eye_cat, diag_mask))]
    uw = {(c, h): _dot(t_cat[c][:, h * C:(h + 1) * C],
                       jnp.concatenate([v[c, h] * beta[c, h], kb[c, h] * egc[c, h]], axis=1).astype(BF16))
          for c, h in ch}
    yield
    attn = {x: (_dot_nt(q[x].astype(BF16), kbf[x]) * decay[x]).astype(BF16) for x in ch}
    q_dec = {x: (q[x] * egc[x]).astype(BF16) for x in ch}
    k_dec_t = {x: (k[x] * jnp.exp(gl[x] - gc[x])).T.astype(BF16) for x in ch}
    egl = {x: jnp.exp(gl[x]) for x in ch}
    return dict(uw=uw, attn=attn, q_dec=q_dec, k_dec_t=k_dec_t, egl=egl)


def _gdn(proj, ab, conv_w, alog_v, dtb_v, layer, norm_w, batch, seq, tc):
    T = proj.shape[0]
    n_s = seq // tc
    HD = GDN_HEADS * GDN_DK
    n_pairs = (tc // GDN_CHUNK) * GDN_HEADS
    cur = lambda b, s: b * n_s + jnp.minimum(s, n_s - 1)
    lag = lambda b, s: b * n_s + jnp.maximum(s - 1, 0)
    return pl.pallas_call(
        _gdn_kernel,
        grid=(batch, n_s + 1),
        in_specs=[pl.BlockSpec((tc, 3 * HD), lambda b, s: (cur(b, s), PM_A_QKV // (3 * HD))),
                  pl.BlockSpec((tc, HD), lambda b, s: (lag(b, s), PM_A_Z // HD)),
                  pl.BlockSpec((tc, LANES), lambda b, s: (cur(b, s), 0)),
                  pl.BlockSpec((None, 8, 3 * HD), lambda b, s: (layer, 0, 0)),
                  pl.BlockSpec((None, 1, LANES), lambda b, s: (layer, 0, 0)),
                  pl.BlockSpec((None, 1, LANES), lambda b, s: (layer, 0, 0)),
                  pl.BlockSpec((1, GDN_DV), lambda b, s: (0, 0))],
        out_specs=pl.BlockSpec((tc, HD), lambda b, s: (lag(b, s), 0)),
        out_shape=jax.ShapeDtypeStruct((T, HD), BF16),
        scratch_shapes=[pltpu.VMEM((tc + 8, 3 * HD), F32),
                        pltpu.VMEM((tc, HD), F32), pltpu.VMEM((tc, HD), F32), pltpu.VMEM((tc, HD), F32),
                        pltpu.VMEM((GDN_HEADS, GDN_DK, GDN_DV), F32),
                        pltpu.VMEM((n_pairs, GDN_CHUNK, 2 * GDN_DK), F32),
                        pltpu.VMEM((n_pairs, GDN_CHUNK, GDN_CHUNK), BF16),
                        pltpu.VMEM((n_pairs, GDN_CHUNK, GDN_DK), BF16),
                        pltpu.VMEM((n_pairs, GDN_DK, GDN_CHUNK), BF16),
                        pltpu.VMEM((n_pairs, 8, GDN_DK), F32)],
        compiler_params=_cparams("parallel", "arbitrary"),
    )(proj, proj, ab, conv_w, alog_v, dtb_v, norm_w)


def _dswa_kernel(cur_ref, prev_ref, o_ref, lse_ref, *, dil):
    BL = DSWA_BLOCK
    n_blk = cur_ref.shape[1] // (BL * dil)
    n = pl.program_id(1)

    def class_rows(ref, r, blk0, n_blocks):
        start, size = r + blk0 * BL * dil, n_blocks * BL
        rows = pl.ds(start, size, stride=dil) if dil > 1 else slice(start, start + size)
        return rows, jnp.concatenate([ref[t, rows, :] for t in range(ref.shape[0])], axis=1)

    ii = lax.broadcasted_iota(jnp.int32, (BL, 2 * BL), 0)
    jj = lax.broadcasted_iota(jnp.int32, (BL, 2 * BL), 1)
    band = (jj >= ii) & (jj <= ii + BL)
    first_valid = jnp.where(n > 0, 0, BL)
    band_first = band & (jj >= first_valid)
    kw = DSWA_HPG * DSWA_DH
    pad = jnp.zeros((BL, DSWA_OW - kw), F32)
    ones_keys = jnp.ones((2 * BL, DSWA_DH), BF16)
    hsl = [slice(h * DSWA_DH, (h + 1) * DSWA_DH) for h in range(DSWA_HPG)]

    mc = min(n_blk, DSWA_CHAIN_GROUP // DSWA_HPG)
    units = [(r, m0) for r in range(dil) for m0 in range(0, n_blk, mc)]
    per_group = max(1, DSWA_CHAIN_GROUP // (mc * DSWA_HPG))
    for u0 in range(0, len(units), per_group):
        group = units[u0:u0 + per_group]
        rows, cur, k_all, v_all = {}, {}, {}, {}
        for u in group:
            r, m0 = u
            rows[u], cur[u] = class_rows(cur_ref, r, m0, mc)
            before = class_rows(prev_ref, r, 0, 1)[1] if m0 == 0 else class_rows(cur_ref, r, m0 - 1, 1)[1]
            k_all[u] = jnp.concatenate([before[:, kw:2 * kw], cur[u][:, kw:2 * kw]], axis=0).astype(BF16)
            v_all[u] = jnp.concatenate([before[:, 2 * kw:3 * kw], cur[u][:, 2 * kw:3 * kw]], axis=0).astype(BF16)
        chains = [(u, m, h) for u in group for m in range(mc) for h in range(DSWA_HPG)]
        keys = {m: slice(m * BL, (m + 2) * BL) for m in range(mc)}
        q = {(u, m, h): cur[u][m * BL:(m + 1) * BL, hsl[h]].astype(BF16) for u, m, h in chains}
        s = {(u, m, h): jnp.where(band_first if (u[1] == 0 and m == 0) else band,
                                  _dot_nt(q[u, m, h], k_all[u][keys[m], hsl[h]]), -jnp.inf) for u, m, h in chains}
        mx = {x: jnp.max(s[x], axis=-1, keepdims=True) for x in chains}
        p = {x: jnp.exp(s[x] - mx[x]) for x in chains}
        pb = {x: p[x].astype(BF16) for x in chains}
        l = {x: _dot(pb[x], ones_keys) for x in chains}
        o = {(u, m, h): _dot(pb[u, m, h], v_all[u][keys[m], hsl[h]]) * (1.0 / l[u, m, h]) for u, m, h in chains}
        lse = {x: mx[x] + jnp.log(l[x]) for x in chains}
        for u in group:
            o_u = jnp.concatenate([jnp.concatenate([o[u, m, h] for h in range(DSWA_HPG)] + [pad], axis=1)
                                   for m in range(mc)], axis=0)
            lse_u = jnp.concatenate([jnp.concatenate([lse[u, m, h] for h in range(DSWA_HPG)] + [pad], axis=1)
                                     for m in range(mc)], axis=0)
            for t in range(DSWA_OW // LANES):
                o_ref[t, rows[u], :] = o_u[:, t * LANES:(t + 1) * LANES]
                lse_ref[t, rows[u], :] = lse_u[:, t * LANES:(t + 1) * LANES]


def _dswa(grp, batch, seq, dil):
    n_in, T, _ = grp.shape
    n_out = DSWA_OW // LANES
    span = DSWA_BLOCK * dil
    n_blk = max(1, min(DSWA_TOKENS_PER_STEP, seq) // span)
    step = n_blk * span
    n_steps = seq // step
    out = jax.ShapeDtypeStruct((n_out, T, LANES), F32)
    return pl.pallas_call(
        functools.partial(_dswa_kernel, dil=dil),
        grid=(batch, n_steps),
        in_specs=[pl.BlockSpec((n_in, step, LANES), lambda b, n: (0, b * n_steps + n, 0)),
                  pl.BlockSpec((n_in, span, LANES),
                               lambda b, n: (0, b * (seq // span) + jnp.maximum(n * n_blk - 1, 0), 0))],
        out_specs=[pl.BlockSpec((n_out, step, LANES), lambda b, n: (0, b * n_steps + n, 0))] * 2,
        out_shape=[out, out],
        compiler_params=_cparams("parallel", "arbitrary"),
    )(grp, grp)


def _sb_kernel(q_ref, k_ref, v_ref, o_ref, vt_s, acc_s):
    BLK, DH = SB_BLOCK, SB_DH
    W = 2 * BLK
    G = q_ref.shape[0] // BLK
    n_kblocks = k_ref.shape[0] // BLK
    step_id = pl.program_id(2)
    i0 = step_id * G

    @pl.when(step_id == 0)
    def _():
        for j in range(n_kblocks):
            vt_s[j] = v_ref[j * BLK:(j + 1) * BLK, :].astype(F32).T.astype(BF16)

    ri = lax.broadcasted_iota(jnp.int32, (BLK, W), 0)
    ci = lax.broadcasted_iota(jnp.int32, (BLK, W), 1)
    causal = ri < (ci & (BLK - 1))
    r2 = lax.broadcasted_iota(jnp.int32, (BLK, BLK), 0)
    c2 = lax.broadcasted_iota(jnp.int32, (BLK, BLK), 1)
    after = jnp.where(c2 > r2, 1.0, 0.0).astype(BF16)
    after2 = jnp.concatenate([after, after], axis=1)

    head0_feat = lax.broadcasted_iota(jnp.int32, (BLK, LANES), 1) < DH
    q_bd = []
    for g in range(G):
        qg = q_ref[g * BLK:(g + 1) * BLK, :]
        zero = jnp.zeros_like(qg)
        q_bd.append(jnp.concatenate([jnp.where(head0_feat, qg, zero), jnp.where(head0_feat, zero, qg)], axis=0))

    def visit(js, c_rows, mask):
        ks = [k_ref[pl.ds(pl.multiple_of(js[g] * BLK, BLK), BLK), :].astype(BF16) for g in range(G)]
        vts = [vt_s[js[g]] for g in range(G)]
        zs = [_dot_nt(ks[g], q_bd[g]) for g in range(G)]
        sps = [jnp.maximum(z, 0.0) + jnp.log(1.0 + jnp.exp(-jnp.abs(z))) for z in zs]
        sp_ms = sps if mask is None else [jnp.where(mask, sp, 0.0) for sp in sps]
        splits = [_split2(sp_m) for sp_m in sp_ms]
        sufs = [_dot(after2, jnp.concatenate([hi, lo], axis=0)) for hi, lo in splits]
        a_s = [jnp.exp(zs[g] - sps[g] - sufs[g] + c_rows[g]) for g in range(G)]
        if mask is not None:
            a_s = [jnp.where(mask, a, 0.0) for a in a_s]
        pvs = [_dot(vts[g], a_s[g].astype(BF16)) for g in range(G)]
        return pvs, [c_rows[g] - (sufs[g][0:1, :] + sp_ms[g][0:1, :]) for g in range(G)]

    pvs, cs = visit([i0 + g for g in range(G)], [jnp.zeros((1, W), F32)] * G, causal)
    for g in range(G):
        acc_s[g] = pvs[g]

    def cond(state):
        d, live, _ = state
        return (d < i0 + G) & live

    def body(state):
        d, _, cs = state
        js = [i0 + g - d for g in range(G)]
        c_in = [jnp.where(js[g] < 0, SB_LOG_DEAD, cs[g]) for g in range(G)]
        accs = [acc_s[g] for g in range(G)]
        pvs, new = visit([jnp.maximum(j, 0) for j in js], c_in, None)
        for g in range(G):
            acc_s[g] = accs[g] + pvs[g]
        c_max = jnp.max(functools.reduce(jnp.maximum, new))
        return d + 1, c_max > SB_LOG_ZERO, tuple(new)

    lax.while_loop(cond, body, (jnp.int32(1), jnp.bool_(True), tuple(cs)))
    for g in range(G):
        acc = acc_s[g]
        o_ref[g * BLK:(g + 1) * BLK, :] = jnp.concatenate([acc[:DH, :BLK], acc[DH:, BLK:]],
                                                          axis=0).T.astype(o_ref.dtype)


def _sb(proj, batch, seq):
    T = proj.shape[0]
    nq = seq // SB_BLOCK
    G = min(SB_QBLOCKS_PER_STEP, nq)
    n_steps = nq // G
    pairs = SB_HEADS // 2
    qo, ko, vo = PM_C // LANES, (PM_C + SB_HEADS * SB_DH) // LANES, (PM_C + 2 * SB_HEADS * SB_DH) // LANES
    return pl.pallas_call(
        _sb_kernel,
        grid=(batch, pairs, n_steps),
        in_specs=[pl.BlockSpec((G * SB_BLOCK, LANES), lambda b, p, i: (b * n_steps + i, qo + p)),
                  pl.BlockSpec((seq, LANES), lambda b, p, i: (b, ko + p)),
                  pl.BlockSpec((seq, LANES), lambda b, p, i: (b, vo + p))],
        out_specs=pl.BlockSpec((G * SB_BLOCK, LANES), lambda b, p, i: (b * n_steps + i, p)),
        out_shape=jax.ShapeDtypeStruct((T, SB_HEADS * SB_DH), BF16),
        scratch_shapes=[pltpu.VMEM((nq, LANES, SB_BLOCK), BF16),
                        pltpu.VMEM((G, LANES, 2 * SB_BLOCK), F32)],
        compiler_params=_cparams("parallel", "parallel", "arbitrary"),
    )(proj, proj, proj)


def _ret_kernel(qk_ref, v_ref, g_ref, cos_ref, sin_ref, nw_ref, o_ref, r_ref):
    tc = qk_ref.shape[0]
    C = RET_CHUNK
    half = RET_DK // 2

    @pl.when(pl.program_id(1) == 0)
    def _():
        r_ref[...] = jnp.zeros_like(r_ref)

    lane = lax.broadcasted_iota(jnp.int32, (C, LANES), 1)
    first = (lane & (RET_DK - 1)) < half
    ri = lax.broadcasted_iota(jnp.int32, (C, C), 0)
    ci = lax.broadcasted_iota(jnp.int32, (C, C), 1)
    diff = (ri - ci).astype(F32)
    causal = ri >= ci
    rowf = lax.broadcasted_iota(jnp.int32, (C, LANES), 0).astype(F32)
    nw = nw_ref[...]

    heads = range(RET_HEADS)
    chunks = range(tc // C)
    ch = [(c, h) for c in chunks for h in heads]
    rows = [slice(c * C, (c + 1) * C) for c in chunks]
    hsl = [slice(h * RET_DV, (h + 1) * RET_DV) for h in heads]
    dmat = [jnp.where(causal, jnp.exp(diff * _RET_LOG_GAMMA[h]), 0.0) for h in heads]
    xi = [jnp.exp((rowf + 1.0) * _RET_LOG_GAMMA[h]) for h in heads]
    zeta = [jnp.exp((C - 1.0 - rowf[:, :RET_DK]) * _RET_LOG_GAMMA[h]) for h in heads]

    def rope(x, c):
        sw = jnp.where(first, pltpu.roll(x, LANES - half, 1), pltpu.roll(x, half, 1))
        return x * cos_ref[rows[c], :] + sw * sin_ref[rows[c], :]

    kw = RET_HEADS * RET_DK
    qp = {(c, p): rope(qk_ref[rows[c], p * LANES:(p + 1) * LANES].astype(F32), c)
          for c in chunks for p in range(RET_HEADS // 2)}
    kp = {(c, p): rope(qk_ref[rows[c], kw + p * LANES:kw + (p + 1) * LANES].astype(F32), c)
          for c in chunks for p in range(RET_HEADS // 2)}
    q = {(c, h): qp[c, h // 2][:, (h % 2) * RET_DK:(h % 2 + 1) * RET_DK].astype(BF16) for c, h in ch}
    k = {(c, h): kp[c, h // 2][:, (h % 2) * RET_DK:(h % 2 + 1) * RET_DK] for c, h in ch}
    vb = {(c, h): v_ref[rows[c], hsl[h]].astype(BF16) for c, h in ch}
    gate = {(c, h): g_ref[rows[c], hsl[h]].astype(F32) for c, h in ch}
    intra = {(c, h): (_dot_nt(q[c, h], k[c, h].astype(BF16)) * dmat[h]).astype(BF16) for c, h in ch}
    delta = {(c, h): _dot((k[c, h] * zeta[h]).T.astype(BF16), vb[c, h]) for c, h in ch}
    r_in = {}
    for h in heads:
        r = r_ref[h]
        for c in chunks:
            r_in[c, h] = r.astype(BF16)
            r = r * math.exp(C * _RET_LOG_GAMMA[h]) + delta[c, h]
        r_ref[h] = r
    o = {(c, h): _dot(intra[c, h], vb[c, h]) + _dot(q[c, h], r_in[c, h]) * xi[h] for c, h in ch}
    for c, h in ch:
        gg = gate[c, h]
        o_ref[rows[c], hsl[h]] = (_rms(o[c, h], nw) * (gg * _sigmoid(gg))).astype(o_ref.dtype)


def _ret(proj, cos_t, sin_t, norm_w, batch, seq, tc):
    T = proj.shape[0]
    n_s = seq // tc
    W = RET_HEADS * RET_DV
    return pl.pallas_call(
        _ret_kernel,
        grid=(batch, n_s),
        in_specs=[pl.BlockSpec((tc, W), lambda b, s: (b * n_s + s, PM_D_QK // W)),
                  pl.BlockSpec((tc, W), lambda b, s: (b * n_s + s, PM_D_V // W)),
                  pl.BlockSpec((tc, W), lambda b, s: (b * n_s + s, PM_D_G // W)),
                  pl.BlockSpec((tc, LANES), lambda b, s: (s, 0)),
                  pl.BlockSpec((tc, LANES), lambda b, s: (s, 0)),
                  pl.BlockSpec((1, RET_DV), lambda b, s: (0, 0))],
        out_specs=pl.BlockSpec((tc, W), lambda b, s: (b * n_s + s, 0)),
        out_shape=jax.ShapeDtypeStruct((T, W), BF16),
        scratch_shapes=[pltpu.VMEM((RET_HEADS, RET_DK, RET_DV), F32)],
        compiler_params=_cparams("parallel", "arbitrary"),
    )(proj, proj, proj, cos_t, sin_t, norm_w)


def _merge_kernel(x_ref, oa_ref, ob0_ref, ob1_ref, ob2_ref, l0_ref, l1_ref, l2_ref, oc_ref, od_ref,
                  ga_ref, gb_ref, gc_ref, gd_ref, wa_ref, wb_ref, wc_ref, wd_ref, wo_ref, nw_ref, out_ref):
    def slabs(ref):
        return jnp.concatenate([ref[t] for t in range(ref.shape[0])], axis=1)

    l0, l1, l2 = slabs(l0_ref), slabs(l1_ref), slabs(l2_ref)
    m = jnp.maximum(jnp.maximum(l0, l1), l2)
    e0, e1, e2 = jnp.exp(l0 - m), jnp.exp(l1 - m), jnp.exp(l2 - m)
    inv = 1.0 / (e0 + e1 + e2)
    ob = jnp.concatenate([slabs(ob0_ref) * (e0 * inv), slabs(ob1_ref) * (e1 * inv), slabs(ob2_ref) * (e2 * inv)],
                         axis=1).astype(BF16)
    y = _sigmoid(ga_ref[...].astype(F32)) * _dot(oa_ref[...], wa_ref[...])
    y = y + _sigmoid(gb_ref[...].astype(F32)) * _dot(ob, wb_ref[...])
    y = y + _sigmoid(gc_ref[...].astype(F32)) * _dot(oc_ref[...], wc_ref[...])
    y = y + _sigmoid(gd_ref[...].astype(F32)) * _dot(od_ref[...], wd_ref[...])
    mixed = _dot(y.astype(BF16), wo_ref[...])
    out_ref[...] = x_ref[...] + _rms(mixed, nw_ref[...])


def _merge(x, oa, obs, lses, oc, od, proj, wa, wb, wc, wd, wo, layer, nw, tm):
    T, D = x.shape
    row = lambda w: pl.BlockSpec((tm, w), lambda i: (i, 0))
    gate = lambda br: pl.BlockSpec((tm, D), lambda i: (i, PM_GATES // D + br))
    full = lambda a: pl.BlockSpec((None,) + a.shape[1:], lambda i: (layer, 0, 0))
    return pl.pallas_call(
        _merge_kernel,
        grid=(T // tm,),
        in_specs=[row(D), row(oa.shape[1])]
                 + [pl.BlockSpec((DSWA_OW // LANES, tm, LANES), lambda i: (0, i, 0))] * 6
                 + [row(oc.shape[1]), row(od.shape[1])]
                 + [gate(0), gate(1), gate(2), gate(3)] + [full(wa), full(wb), full(wc), full(wd), full(wo)]
                 + [pl.BlockSpec(nw.shape, lambda i: (0, 0))],
        out_specs=row(D),
        out_shape=jax.ShapeDtypeStruct((T, D), F32),
        compiler_params=_cparams("parallel"),
    )(x, oa, *obs, *lses, oc, od, proj, proj, proj, proj, wa, wb, wc, wd, wo, nw)


def _mlp_kernel(x_ref, n1_ref, w1_ref, w2_ref, n2_ref, out_ref, h_ref, acc_ref):
    f = pl.program_id(1)

    @pl.when(f == 0)
    def _():
        h_ref[...] = _rms(x_ref[...], n1_ref[...]).astype(BF16)
        acc_ref[...] = jnp.zeros_like(acc_ref)

    hid = jnp.maximum(_dot(h_ref[...], w1_ref[...]), 0.0)
    acc_ref[...] += _dot((hid * hid).astype(BF16), w2_ref[...])

    @pl.when(f == pl.num_programs(1) - 1)
    def _():
        out_ref[...] = x_ref[...] + _rms(acc_ref[...], n2_ref[...])


def _mlp(x, n1, w1, w2, layer, n2, tm, tf):
    T, D = x.shape
    F = w1.shape[2]
    return pl.pallas_call(
        _mlp_kernel,
        grid=(T // tm, F // tf),
        in_specs=[pl.BlockSpec((tm, D), lambda i, f: (i, 0)),
                  pl.BlockSpec((1, D), lambda i, f: (0, 0)),
                  pl.BlockSpec((None, D, tf), lambda i, f: (layer, 0, f)),
                  pl.BlockSpec((None, tf, D), lambda i, f: (layer, f, 0)),
                  pl.BlockSpec((1, D), lambda i, f: (0, 0))],
        out_specs=pl.BlockSpec((tm, D), lambda i, f: (i, 0)),
        out_shape=jax.ShapeDtypeStruct((T, D), F32),
        scratch_shapes=[pltpu.VMEM((tm, D), BF16), pltpu.VMEM((tm, D), F32)],
        compiler_params=_cparams("parallel", "arbitrary"),
    )(x, n1, w1, w2, n2)


def _prep_w_in(w_in):
    sec = [w_in[..., _COL_OFF[i]:_COL_OFF[i + 1]] for i in range(len(_COL_SIZES))]
    a_qkv, a_z, a_a, a_b, b_qkv, c_qkv, d_qk, d_v, d_g, gates = sec
    sbw = SB_HEADS * SB_DH
    c_qkv = jnp.concatenate([c_qkv[..., :sbw] * SB_DH ** -0.5, c_qkv[..., sbw:]], axis=-1)
    rw = RET_HEADS * RET_DK
    d_qk = jnp.concatenate([d_qk[..., :rw], d_qk[..., rw:] * RET_DK ** -0.5], axis=-1)
    w_main = jnp.concatenate([a_qkv, a_z, c_qkv, d_qk, d_v, d_g, gates], axis=-1).astype(BF16)
    bw = DSWA_HEADS * DSWA_DH
    gw = DSWA_HPG * DSWA_DH
    lead = w_in.shape[:-1]
    groups = []
    for g in range(len(DSWA_GROUPS)):
        q = b_qkv[..., g * gw:(g + 1) * gw] * DSWA_DH ** -0.5
        k = b_qkv[..., bw + g * gw: bw + (g + 1) * gw]
        v = b_qkv[..., 2 * bw + g * gw: 2 * bw + (g + 1) * gw]
        groups += [q, k, v, jnp.zeros(lead + (DSWA_GW - 3 * gw,), F32)]
    w_b = jnp.concatenate(groups + [a_a, a_b, jnp.zeros(lead + (LANES - 2 * GDN_HEADS,), F32)],
                          axis=-1).astype(BF16)
    return w_main, w_b


def _prep_params(w_in, conv_w, a_log, dt_bias, w_br_a, w_br_b, w_br_c, w_br_d, w_o, w_mlp_in, w_mlp_out):
    n_layers = w_in.shape[0]
    w_main, w_b = _prep_w_in(w_in)
    gw = DSWA_HPG * DSWA_DH
    wb = jnp.concatenate(
        [jnp.concatenate([w_br_b[:, g * gw:(g + 1) * gw], jnp.zeros((n_layers, DSWA_OW - gw, D_MODEL), F32)], axis=1)
         for g in range(len(DSWA_GROUPS))], axis=1).astype(BF16)
    lane_vec = lambda v: jnp.zeros((n_layers, 1, LANES), F32).at[:, 0, :v.shape[1]].set(v.astype(F32))
    return dict(
        w_main=w_main, w_b=w_b,
        cw=jnp.zeros((n_layers, 8, conv_w.shape[2]), F32).at[:, :GDN_CONV].set(conv_w),
        alog=lane_vec(a_log), dtb=lane_vec(dt_bias),
        wa=w_br_a.astype(BF16), wb=wb, wc=w_br_c.astype(BF16), wd=w_br_d.astype(BF16), wo=w_o.astype(BF16),
        w1=w_mlp_in.astype(BF16), w2=w_mlp_out.astype(BF16))


def _rope_tables(seq):
    pos = jnp.arange(seq, dtype=jnp.int32).astype(F32)[:, None]
    inv_b = DSWA_ROPE_THETA ** (-jnp.arange(0, DSWA_ROPE_DIM, 2, dtype=F32) / DSWA_ROPE_DIM)
    ang = pos * inv_b[None, :]
    cb, sb = jnp.cos(ang), jnp.sin(ang)
    rest = DSWA_DH - DSWA_ROPE_DIM
    cos_b = jnp.tile(jnp.concatenate([cb, cb, jnp.ones((seq, rest), F32)], axis=1), (1, LANES // DSWA_DH))
    sin_b = jnp.tile(jnp.concatenate([-sb, sb, jnp.zeros((seq, rest), F32)], axis=1), (1, LANES // DSWA_DH))
    inv_r = RET_THETA ** (-jnp.linspace(0.0, 1.0, RET_DK // 2, dtype=F32))
    ang = pos * inv_r[None, :]
    cr, sr = jnp.cos(ang), jnp.sin(ang)
    cos_r = jnp.tile(jnp.concatenate([cr, cr], axis=1), (1, LANES // RET_DK))
    sin_r = jnp.tile(jnp.concatenate([-sr, sr], axis=1), (1, LANES // RET_DK))
    return cos_b, sin_b, cos_r, sin_r


def _layer(x, batch, seq, tabs, n_pre_mix, n_post_mix, n_pre_mlp, n_post_mlp, gdn_norm, ret_norm, p, layer):
    T = x.shape[0]
    tiles = _tile_sizes(T, seq)
    cos_b, sin_b, cos_r, sin_r = tabs
    gain = n_pre_mix.reshape(1, -1)
    h, g0, g1, g2, ab = _dswa_proj(x, gain, p["w_b"], layer, cos_b, sin_b, seq, tm=tiles["dswa_proj_rows"])
    proj = _matmul(h, p["w_main"], layer, tm=tiles["proj_rows"], tn=tiles["proj_cols"])

    oa = _gdn(proj, ab, p["cw"], p["alog"], p["dtb"], layer, gdn_norm.reshape(1, -1), batch, seq,
              tc=tiles["gdn_rows"])
    obs, lses = [], []
    for grp, (_, dil) in zip((g0, g1, g2), DSWA_GROUPS):
        o, lse = _dswa(grp, batch, seq, dil)
        obs.append(o)
        lses.append(lse)
    oc = _sb(proj, batch, seq)
    od = _ret(proj, cos_r, sin_r, ret_norm.reshape(1, -1), batch, seq, tc=tiles["ret_rows"])

    x1 = _merge(x, oa, obs, lses, oc, od, proj, p["wa"], p["wb"], p["wc"], p["wd"], p["wo"], layer,
                n_post_mix.reshape(1, -1), tm=tiles["merge_rows"])
    return _mlp(x1, n_pre_mlp.reshape(1, -1), p["w1"], p["w2"], layer, n_post_mlp.reshape(1, -1),
                tm=tiles["mlp_rows"], tf=tiles["mlp_hidden"])


def _tile_sizes(n_tokens, seq):
    return dict(
        proj_rows=min(1024, n_tokens), proj_cols=3072,
        dswa_proj_rows=min(1024, seq),
        gdn_rows=min(4 * GDN_CHUNK, seq),
        ret_rows=min(8 * RET_CHUNK, seq),
        merge_rows=min(512, n_tokens),
        mlp_rows=min(1024, n_tokens), mlp_hidden=1024,
    )


def kernel(x, norm_pre_mix, norm_post_mix, norm_pre_mlp, norm_post_mlp, w_in, conv_w, a_log, dt_bias, gdn_norm,
           ret_norm, w_br_a, w_br_b, w_br_c, w_br_d, w_o, w_mlp_in, w_mlp_out):
    batch, seq, d = x.shape
    tabs = _rope_tables(seq)
    params = _prep_params(w_in, conv_w, a_log, dt_bias, w_br_a, w_br_b, w_br_c, w_br_d, w_o, w_mlp_in, w_mlp_out)
    h = x.reshape(batch * seq, d)
    for l in range(norm_pre_mix.shape[0]):
        h = _layer(h, batch, seq, tabs, norm_pre_mix[l], norm_post_mix[l], norm_pre_mlp[l], norm_post_mlp[l],
                   gdn_norm[l], ret_norm[l], params, l)
    return h.reshape(batch, seq, d)
```

```python
import functools
import math

import numpy as np
import jax
import jax.numpy as jnp
from jax import lax
from jax.experimental import pallas as pl
from jax.experimental.pallas import tpu as pltpu

F32 = jnp.float32
BF16 = jnp.bfloat16

D_MODEL = 1024
GDN_HEADS, GDN_DK, GDN_DV, GDN_CONV, GDN_CHUNK = 4, 128, 128, 4, 64
DSWA_GROUPS = ((128, 1), (512, 4), (2048, 16))
DSWA_HPG, DSWA_DH, DSWA_BLOCK = 3, 64, 128
DSWA_HEADS = DSWA_HPG * len(DSWA_GROUPS)
DSWA_ROPE_THETA, DSWA_ROPE_DIM = 500000.0, DSWA_DH // 4
SB_HEADS, SB_DH, SB_BLOCK = 8, 64, 128
SB_LOG_ZERO = -110.0
SB_LOG_DEAD = -1e30
SB_QBLOCKS_PER_STEP = 16
DSWA_TOKENS_PER_STEP = 2048
DSWA_CHAIN_GROUP = 12
RET_HEADS, RET_DK, RET_DV, RET_THETA = 4, 64, 128, 10000.0
RET_CHUNK = 128
D_FF = 4 * D_MODEL
NORM_EPS = 1e-6
L2_EPS = 1e-6

_COL_SIZES = (1536, 512, 4, 4, 1728, 1536, 512, 512, 512, 4096)
_COL_OFF = np.concatenate([[0], np.cumsum(_COL_SIZES)])

PM_A_QKV, PM_A_Z, PM_C, PM_D_QK, PM_D_V, PM_D_G, PM_GATES, PM_WIDTH = 0, 1536, 2048, 3584, 4096, 4608, 5120, 9216
DSWA_GW = 640
DSWA_OW = 256

V7X_VMEM_BYTES = 64 * 1024 * 1024
VMEM_LIMIT = V7X_VMEM_BYTES * 3 // 4
LANES = 128

_RET_LOG_GAMMA = [float(np.log1p(-np.exp2(np.float32(-5.0 - h))).astype(np.float32)) for h in range(RET_HEADS)]


def _cparams(*sem):
    return pltpu.CompilerParams(dimension_semantics=sem, vmem_limit_bytes=VMEM_LIMIT)


def _sigmoid(x):
    return 1.0 / (1.0 + jnp.exp(-x))


def _softplus(x):
    return jnp.maximum(x, 0.0) + jnp.log(1.0 + jnp.exp(-jnp.abs(x)))


def _dot(a, b):
    return jnp.dot(a, b, preferred_element_type=F32)


def _dot_nt(a, b):
    return lax.dot_general(a, b, (((1,), (1,)), ((), ())), preferred_element_type=F32)


def _split3(x):
    hi = x.astype(BF16)
    r = x - hi.astype(F32)
    mid = r.astype(BF16)
    lo = (r - mid.astype(F32)).astype(BF16)
    return hi, mid, lo


def _dot_exact_lhs(a_bf16, x):
    hi, mid, lo = _split3(x)
    return _dot(a_bf16, hi) + (_dot(a_bf16, mid) + _dot(a_bf16, lo))


def _split2(x):
    hi = x.astype(BF16)
    lo = (x - hi.astype(F32)).astype(BF16)
    return hi, lo


def _rms(x, w):
    ms = jnp.mean(x * x, axis=-1, keepdims=True)
    return x * lax.rsqrt(ms + NORM_EPS) * w


def _matmul_kernel(h_ref, w_ref, o_ref):
    o_ref[...] = _dot(h_ref[...], w_ref[...]).astype(o_ref.dtype)


def _matmul(h, w, layer, tm, tn):
    T, K = h.shape
    N = w.shape[2]
    return pl.pallas_call(
        _matmul_kernel,
        grid=(T // tm, N // tn),
        in_specs=[pl.BlockSpec((tm, K), lambda i, j: (i, 0)),
                  pl.BlockSpec((None, K, tn), lambda i, j: (layer, 0, j))],
        out_specs=pl.BlockSpec((tm, tn), lambda i, j: (i, j)),
        out_shape=jax.ShapeDtypeStruct((T, N), BF16),
        compiler_params=_cparams("parallel", "parallel"),
    )(h, w)


def _dswa_proj_kernel(x_ref, g_ref, w_ref, cos_ref, sin_ref, oh_ref, o0_ref, o1_ref, o2_ref, oab_ref):
    h = _rms(x_ref[...], g_ref[...]).astype(BF16)
    oh_ref[...] = h
    p = _dot(h, w_ref[...])
    cos = cos_ref[...]
    sin = sin_ref[...]
    lane = lax.broadcasted_iota(jnp.int32, cos.shape, 1)
    first = (lane & (DSWA_DH - 1)) < (DSWA_ROPE_DIM // 2)
    for g, o_ref in enumerate((o0_ref, o1_ref, o2_ref)):
        base = g * DSWA_GW
        for t in range(3):
            xt = p[:, base + t * LANES: base + (t + 1) * LANES]
            sw = jnp.where(first, pltpu.roll(xt, LANES - DSWA_ROPE_DIM // 2, 1),
                           pltpu.roll(xt, DSWA_ROPE_DIM // 2, 1))
            o_ref[t] = xt * cos + sw * sin
        for t in range(3, DSWA_GW // LANES):
            o_ref[t] = p[:, base + t * LANES: base + (t + 1) * LANES]
    oab_ref[...] = p[:, 3 * DSWA_GW:]


def _dswa_proj(x, gain, w, layer, cos_t, sin_t, seq, tm):
    T, K = x.shape
    N = w.shape[2]
    n_s = seq // tm
    grp = jax.ShapeDtypeStruct((DSWA_GW // LANES, T, LANES), F32)
    return pl.pallas_call(
        _dswa_proj_kernel,
        grid=(T // tm,),
        in_specs=[pl.BlockSpec((tm, K), lambda i: (i, 0)),
                  pl.BlockSpec((1, K), lambda i: (0, 0)),
                  pl.BlockSpec((None, K, N), lambda i: (layer, 0, 0)),
                  pl.BlockSpec((tm, LANES), lambda i: (i % n_s, 0)),
                  pl.BlockSpec((tm, LANES), lambda i: (i % n_s, 0))],
        out_specs=[pl.BlockSpec((tm, K), lambda i: (i, 0))]
                  + [pl.BlockSpec((DSWA_GW // LANES, tm, LANES), lambda i: (0, i, 0))] * 3
                  + [pl.BlockSpec((tm, LANES), lambda i: (i, 0))],
        out_shape=[jax.ShapeDtypeStruct((T, K), BF16), grp, grp, grp, jax.ShapeDtypeStruct((T, LANES), F32)],
        compiler_params=_cparams("parallel"),
    )(x, gain, w, cos_t, sin_t)


def _per_head_matmul(xs, p_cat, diag_mask):
    C = GDN_CHUNK
    ph, pl_ = [jnp.concatenate([t] * GDN_HEADS, axis=0) * diag_mask for t in _split2(p_cat)]
    parts = [_split2(x) for x in xs]
    his = [hi for hi, _ in parts]
    by_hi = _dot(jnp.concatenate(his + [lo for _, lo in parts], axis=0), ph)
    by_lo = _dot(jnp.concatenate(his, axis=0), pl_)
    n = len(xs)
    return [by_hi[i * C:(i + 1) * C] + (by_hi[(n + i) * C:(n + i + 1) * C] + by_lo[i * C:(i + 1) * C])
            for i in range(n)]


def _unit_lower_inverse(lows, eye_cat, diag_mask):
    xs = [eye_cat - low for low in lows]
    ps = [_per_head_matmul([low], low, diag_mask)[0] for low in lows]
    yield
    n = 4
    while n < GDN_CHUNK:
        both = [_per_head_matmul([x, p], p, diag_mask) for x, p in zip(xs, ps)]
        xs = [x + xp for x, (xp, _) in zip(xs, both)]
        ps = [pp for _, pp in both]
        n *= 2
        yield
    return [x + _per_head_matmul([x], p, diag_mask)[0] for x, p in zip(xs, ps)]


def _interleave(*gens):
    results = [None] * len(gens)
    live = list(range(len(gens)))
    while live:
        for i in list(live):
            try:
                next(gens[i])
            except StopIteration as stop:
                results[i] = stop.value
                live.remove(i)
    return results


def _gdn_kernel(qkv_ref, z_ref, ab_ref, cw_ref, alog_ref, dtb_ref, nw_ref, o_ref,
                carry_ref, q_s, k_s, v_s, state_ref, uw_s, attn_s, qd_s, kdt_s, egl_s):
    tc = qkv_ref.shape[0]
    C = GDN_CHUNK
    H = GDN_HEADS
    HD = GDN_HEADS * GDN_DK
    step = pl.program_id(1)
    n_tiles = pl.num_programs(1) - 1
    n_chunks = tc // C
    heads = range(H)
    chunks = range(n_chunks)
    ch = [(c, h) for c in chunks for h in heads]
    hsl = [slice(h * GDN_DK, (h + 1) * GDN_DK) for h in heads]
    nw = nw_ref[...]

    @pl.when(step == 0)
    def _():
        carry_ref[0:8, :] = jnp.zeros((8, 3 * HD), F32)
        for ref in (state_ref, uw_s, attn_s, qd_s, kdt_s, egl_s):
            ref[...] = jnp.zeros_like(ref)

    def recurrence():
        st = [state_ref[h] for h in heads]
        uw = {(c, h): uw_s[c * H + h] for c, h in ch}
        attn = {(c, h): attn_s[c * H + h] for c, h in ch}
        q_dec = {(c, h): qd_s[c * H + h] for c, h in ch}
        k_dec_t = {(c, h): kdt_s[c * H + h] for c, h in ch}
        egl = {(c, h): egl_s[c * H + h][0:1, :] for c, h in ch}
        gate = {(c, h): z_ref[c * C:(c + 1) * C, hsl[h]].astype(F32) for c, h in ch}
        yield
        outs = {}
        for c in chunks:
            stb = [st[h].astype(BF16) for h in heads]
            w_st = [_dot(uw[c, h][:, GDN_DV:].astype(BF16), stb[h]) for h in heads]
            q_st = [_dot(q_dec[c, h], stb[h]) for h in heads]
            yield
            vnb = [(uw[c, h][:, :GDN_DV] - w_st[h]).astype(BF16) for h in heads]
            o = [q_st[h] + _dot(attn[c, h], vnb[h]) for h in heads]
            upd = [_dot(k_dec_t[c, h], vnb[h]) for h in heads]
            yield
            for h in heads:
                st[h] = st[h] * egl[c, h] + upd[h]
                zz = gate[c, h]
                outs[c, h] = _rms(o[h], nw) * (zz * _sigmoid(zz))
            yield
        return outs, st

    def store_recurrence(outs, st):
        for c, h in ch:
            o_ref[c * C:(c + 1) * C, hsl[h]] = outs[c, h].astype(o_ref.dtype)
        for h in heads:
            state_ref[h] = st[h]

    @pl.when(step == n_tiles)
    def _():
        store_recurrence(*_interleave(recurrence())[0])

    @pl.when(step < n_tiles)
    def _():
        (outs, st), pre = _interleave(
            recurrence(), _gdn_prepare(qkv_ref, ab_ref, cw_ref, alog_ref, dtb_ref, carry_ref, q_s, k_s, v_s))
        store_recurrence(outs, st)
        for c, h in ch:
            uw_s[c * H + h] = pre["uw"][c, h]
            attn_s[c * H + h] = pre["attn"][c, h]
            qd_s[c * H + h] = pre["q_dec"][c, h]
            kdt_s[c * H + h] = pre["k_dec_t"][c, h]
            egl_s[c * H + h] = jnp.broadcast_to(pre["egl"][c, h], (8, GDN_DK))


def _gdn_prepare(qkv_ref, ab_ref, cw_ref, alog_ref, dtb_ref, carry_ref, q_s, k_s, v_s):
    tc = qkv_ref.shape[0]
    C = GDN_CHUNK
    H = GDN_HEADS
    HD = GDN_HEADS * GDN_DK
    carry_ref[8:, :] = qkv_ref[...].astype(F32)
    for grp, dst in enumerate((q_s, k_s, v_s)):
        cols = slice(grp * HD, (grp + 1) * HD)
        y = carry_ref[8:, cols] * cw_ref[GDN_CONV - 1:GDN_CONV, cols]
        for j in range(1, GDN_CONV):
            y = y + carry_ref[8 - j:8 - j + tc, cols] * cw_ref[GDN_CONV - 1 - j:GDN_CONV - j, cols]
        y = y * _sigmoid(y)
        if grp == 2:
            dst[...] = y
        else:
            for h in range(GDN_HEADS):
                yh = y[:, h * GDN_DK:(h + 1) * GDN_DK]
                yh = yh * lax.rsqrt(jnp.sum(yh * yh, axis=-1, keepdims=True) + L2_EPS)
                if grp == 0:
                    yh = yh * (GDN_DK ** -0.5)
                dst[:, h * GDN_DK:(h + 1) * GDN_DK] = yh
        yield
    carry_ref[0:8, :] = carry_ref[tc:tc + 8, :]

    ri = lax.broadcasted_iota(jnp.int32, (C, C), 0)
    ci = lax.broadcasted_iota(jnp.int32, (C, C), 1)
    incl = ri >= ci
    strict = ri > ci
    a_low = jnp.where(incl, 1.0, 0.0).astype(BF16)
    eye_cat = jnp.concatenate([jnp.where(ri == ci, 1.0, 0.0).astype(F32)] * H, axis=1)
    rb = lax.broadcasted_iota(jnp.int32, (H * C, H * C), 0) // C
    cb = lax.broadcasted_iota(jnp.int32, (H * C, H * C), 1) // C
    diag_mask = jnp.where(rb == cb, 1.0, 0.0).astype(BF16)
    neg_exp_alog = -jnp.exp(alog_ref[...])
    dtb = dtb_ref[...]
    n_chunks = tc // C
    heads = range(H)
    hsl = [slice(h * GDN_DK, (h + 1) * GDN_DK) for h in heads]

    chunks = range(n_chunks)
    ch = [(c, h) for c in chunks for h in heads]
    rows = [slice(c * C, (c + 1) * C) for c in chunks]
    ab = [ab_ref[rows[c], :] for c in chunks]
    gv = [neg_exp_alog * _softplus(ab[c] + dtb) for c in chunks]
    bv = [_sigmoid(ab[c]) for c in chunks]
    gc_all = [_dot_exact_lhs(a_low, gv[c]) for c in chunks]
    gc_t = [gc_all[c].T for c in chunks]
    yield
    q = {(c, h): q_s[rows[c], hsl[h]] for c, h in ch}
    k = {(c, h): k_s[rows[c], hsl[h]] for c, h in ch}
    v = {(c, h): v_s[rows[c], hsl[h]] for c, h in ch}
    gc = {(c, h): jnp.broadcast_to(gc_all[c][:, h:h + 1], (C, GDN_DK)) for c, h in ch}
    beta = {(c, h): jnp.broadcast_to(bv[c][:, H + h:H + h + 1], (C, GDN_DK)) for c, h in ch}
    decay = {(c, h): jnp.exp(jnp.where(incl, gc[c, h][:, :C] - gc_t[c][h:h + 1, :], -jnp.inf)) for c, h in ch}
    egc = {x: jnp.exp(gc[x]) for x in ch}
    gl = {x: gc[x][C - 1:C, :] for x in ch}
    kb = {x: k[x] * beta[x] for x in ch}
    kbf = {x: k[x].astype(BF16) for x in ch}
    kk = {x: _dot_nt(kb[x].astype(BF16), kbf[x]) for x in ch}
    low_cat = [jnp.concatenate([jnp.where(strict, kk[c, h] * decay[c, h], 0.0) for h in heads], axis=1)
               for c in chunks]
    yield
    t_cat = [t.astype(BF16) for t in (yield from _unit_lower_inverse(low_cat, eye_cat, diag_mask))]
    uw = {(c, h): _dot(t_cat[c][:, h * C:(h + 1) * C],
                       jnp.concatenate([v[c, h] * beta[c, h], kb[c, h] * egc[c, h]], axis=1).astype(BF16))
          for c, h in ch}
    yield
    attn = {x: (_dot_nt(q[x].astype(BF16), kbf[x]) * decay[x]).astype(BF16) for x in ch}
    q_dec = {x: (q[x] * egc[x]).astype(BF16) for x in ch}
    k_dec_t = {x: (k[x] * jnp.exp(gl[x] - gc[x])).T.astype(BF16) for x in ch}
    egl = {x: jnp.exp(gl[x]) for x in ch}
    return dict(uw=uw, attn=attn, q_dec=q_dec, k_dec_t=k_dec_t, egl=egl)


def _gdn(proj, ab, conv_w, alog_v, dtb_v, layer, norm_w, batch, seq, tc):
    T = proj.shape[0]
    n_s = seq // tc
    HD = GDN_HEADS * GDN_DK
    n_pairs = (tc // GDN_CHUNK) * GDN_HEADS
    cur = lambda b, s: b * n_s + jnp.minimum(s, n_s - 1)
    lag = lambda b, s: b * n_s + jnp.maximum(s - 1, 0)
    return pl.pallas_call(
        _gdn_kernel,
        grid=(batch, n_s + 1),
        in_specs=[pl.BlockSpec((tc, 3 * HD), lambda b, s: (cur(b, s), PM_A_QKV // (3 * HD))),
                  pl.BlockSpec((tc, HD), lambda b, s: (lag(b, s), PM_A_Z // HD)),
                  pl.BlockSpec((tc, LANES), lambda b, s: (cur(b, s), 0)),
                  pl.BlockSpec((None, 8, 3 * HD), lambda b, s: (layer, 0, 0)),
                  pl.BlockSpec((None, 1, LANES), lambda b, s: (layer, 0, 0)),
                  pl.BlockSpec((None, 1, LANES), lambda b, s: (layer, 0, 0)),
                  pl.BlockSpec((1, GDN_DV), lambda b, s: (0, 0))],
        out_specs=pl.BlockSpec((tc, HD), lambda b, s: (lag(b, s), 0)),
        out_shape=jax.ShapeDtypeStruct((T, HD), BF16),
        scratch_shapes=[pltpu.VMEM((tc + 8, 3 * HD), F32),
                        pltpu.VMEM((tc, HD), F32), pltpu.VMEM((tc, HD), F32), pltpu.VMEM((tc, HD), F32),
                        pltpu.VMEM((GDN_HEADS, GDN_DK, GDN_DV), F32),
                        pltpu.VMEM((n_pairs, GDN_CHUNK, 2 * GDN_DK), F32),
                        pltpu.VMEM((n_pairs, GDN_CHUNK, GDN_CHUNK), BF16),
                        pltpu.VMEM((n_pairs, GDN_CHUNK, GDN_DK), BF16),
                        pltpu.VMEM((n_pairs, GDN_DK, GDN_CHUNK), BF16),
                        pltpu.VMEM((n_pairs, 8, GDN_DK), F32)],
        compiler_params=_cparams("parallel", "arbitrary"),
    )(proj, proj, ab, conv_w, alog_v, dtb_v, norm_w)


def _dswa_kernel(cur_ref, prev_ref, o_ref, lse_ref, *, dil):
    BL = DSWA_BLOCK
    n_blk = cur_ref.shape[1] // (BL * dil)
    n = pl.program_id(1)

    def class_rows(ref, r, blk0, n_blocks):
        start, size = r + blk0 * BL * dil, n_blocks * BL
        rows = pl.ds(start, size, stride=dil) if dil > 1 else slice(start, start + size)
        return rows, jnp.concatenate([ref[t, rows, :] for t in range(ref.shape[0])], axis=1)

    ii = lax.broadcasted_iota(jnp.int32, (BL, 2 * BL), 0)
    jj = lax.broadcasted_iota(jnp.int32, (BL, 2 * BL), 1)
    band = (jj >= ii) & (jj <= ii + BL)
    first_valid = jnp.where(n > 0, 0, BL)
    band_first = band & (jj >= first_valid)
    kw = DSWA_HPG * DSWA_DH
    pad = jnp.zeros((BL, DSWA_OW - kw), F32)
    ones_keys = jnp.ones((2 * BL, DSWA_DH), BF16)
    hsl = [slice(h * DSWA_DH, (h + 1) * DSWA_DH) for h in range(DSWA_HPG)]

    mc = min(n_blk, DSWA_CHAIN_GROUP // DSWA_HPG)
    units = [(r, m0) for r in range(dil) for m0 in range(0, n_blk, mc)]
    per_group = max(1, DSWA_CHAIN_GROUP // (mc * DSWA_HPG))
    for u0 in range(0, len(units), per_group):
        group = units[u0:u0 + per_group]
        rows, cur, k_all, v_all = {}, {}, {}, {}
        for u in group:
            r, m0 = u
            rows[u], cur[u] = class_rows(cur_ref, r, m0, mc)
            before = class_rows(prev_ref, r, 0, 1)[1] if m0 == 0 else class_rows(cur_ref, r, m0 - 1, 1)[1]
            k_all[u] = jnp.concatenate([before[:, kw:2 * kw], cur[u][:, kw:2 * kw]], axis=0).astype(BF16)
            v_all[u] = jnp.concatenate([before[:, 2 * kw:3 * kw], cur[u][:, 2 * kw:3 * kw]], axis=0).astype(BF16)
        chains = [(u, m, h) for u in group for m in range(mc) for h in range(DSWA_HPG)]
        keys = {m: slice(m * BL, (m + 2) * BL) for m in range(mc)}
        q = {(u, m, h): cur[u][m * BL:(m + 1) * BL, hsl[h]].astype(BF16) for u, m, h in chains}
        s = {(u, m, h): jnp.where(band_first if (u[1] == 0 and m == 0) else band,
                                  _dot_nt(q[u, m, h], k_all[u][keys[m], hsl[h]]), -jnp.inf) for u, m, h in chains}
        mx = {x: jnp.max(s[x], axis=-1, keepdims=True) for x in chains}
        p = {x: jnp.exp(s[x] - mx[x]) for x in chains}
        pb = {x: p[x].astype(BF16) for x in chains}
        l = {x: _dot(pb[x], ones_keys) for x in chains}
        o = {(u, m, h): _dot(pb[u, m, h], v_all[u][keys[m], hsl[h]]) * (1.0 / l[u, m, h]) for u, m, h in chains}
        lse = {x: mx[x] + jnp.log(l[x]) for x in chains}
        for u in group:
            o_u = jnp.concatenate([jnp.concatenate([o[u, m, h] for h in range(DSWA_HPG)] + [pad], axis=1)
                                   for m in range(mc)], axis=0)
            lse_u = jnp.concatenate([jnp.concatenate([lse[u, m, h] for h in range(DSWA_HPG)] + [pad], axis=1)
                                     for m in range(mc)], axis=0)
            for t in range(DSWA_OW // LANES):
                o_ref[t, rows[u], :] = o_u[:, t * LANES:(t + 1) * LANES]
                lse_ref[t, rows[u], :] = lse_u[:, t * LANES:(t + 1) * LANES]


def _dswa(grp, batch, seq, dil):
    n_in, T, _ = grp.shape
    n_out = DSWA_OW // LANES
    span = DSWA_BLOCK * dil
    n_blk = max(1, min(DSWA_TOKENS_PER_STEP, seq) // span)
    step = n_blk * span
    n_steps = seq // step
    out = jax.ShapeDtypeStruct((n_out, T, LANES), F32)
    return pl.pallas_call(
        functools.partial(_dswa_kernel, dil=dil),
        grid=(batch, n_steps),
        in_specs=[pl.BlockSpec((n_in, step, LANES), lambda b, n: (0, b * n_steps + n, 0)),
                  pl.BlockSpec((n_in, span, LANES),
                               lambda b, n: (0, b * (seq // span) + jnp.maximum(n * n_blk - 1, 0), 0))],
        out_specs=[pl.BlockSpec((n_out, step, LANES), lambda b, n: (0, b * n_steps + n, 0))] * 2,
        out_shape=[out, out],
        compiler_params=_cparams("parallel", "arbitrary"),
    )(grp, grp)


def _sb_kernel(q_ref, k_ref, v_ref, o_ref, vt_s, acc_s):
    BLK, DH = SB_BLOCK, SB_DH
    W = 2 * BLK
    G = q_ref.shape[0] // BLK
    n_kblocks = k_ref.shape[0] // BLK
    step_id = pl.program_id(2)
    i0 = step_id * G

    @pl.when(step_id == 0)
    def _():
        for j in range(n_kblocks):
            vt_s[j] = v_ref[j * BLK:(j + 1) * BLK, :].astype(F32).T.astype(BF16)

    ri = lax.broadcasted_iota(jnp.int32, (BLK, W), 0)
    ci = lax.broadcasted_iota(jnp.int32, (BLK, W), 1)
    causal = ri < (ci & (BLK - 1))
    r2 = lax.broadcasted_iota(jnp.int32, (BLK, BLK), 0)
    c2 = lax.broadcasted_iota(jnp.int32, (BLK, BLK), 1)
    after = jnp.where(c2 > r2, 1.0, 0.0).astype(BF16)
    after2 = jnp.concatenate([after, after], axis=1)

    head0_feat = lax.broadcasted_iota(jnp.int32, (BLK, LANES), 1) < DH
    q_bd = []
    for g in range(G):
        qg = q_ref[g * BLK:(g + 1) * BLK, :]
        zero = jnp.zeros_like(qg)
        q_bd.append(jnp.concatenate([jnp.where(head0_feat, qg, zero), jnp.where(head0_feat, zero, qg)], axis=0))

    def visit(js, c_rows, mask):
        ks = [k_ref[pl.ds(pl.multiple_of(js[g] * BLK, BLK), BLK), :].astype(BF16) for g in range(G)]
        vts = [vt_s[js[g]] for g in range(G)]
        zs = [_dot_nt(ks[g], q_bd[g]) for g in range(G)]
        sps = [jnp.maximum(z, 0.0) + jnp.log(1.0 + jnp.exp(-jnp.abs(z))) for z in zs]
        sp_ms = sps if mask is None else [jnp.where(mask, sp, 0.0) for sp in sps]
        splits = [_split2(sp_m) for sp_m in sp_ms]
        sufs = [_dot(after2, jnp.concatenate([hi, lo], axis=0)) for hi, lo in splits]
        a_s = [jnp.exp(zs[g] - sps[g] - sufs[g] + c_rows[g]) for g in range(G)]
        if mask is not None:
            a_s = [jnp.where(mask, a, 0.0) for a in a_s]
        pvs = [_dot(vts[g], a_s[g].astype(BF16)) for g in range(G)]
        return pvs, [c_rows[g] - (sufs[g][0:1, :] + sp_ms[g][0:1, :]) for g in range(G)]

    pvs, cs = visit([i0 + g for g in range(G)], [jnp.zeros((1, W), F32)] * G, causal)
    for g in range(G):
        acc_s[g] = pvs[g]

    def cond(state):
        d, live, _ = state
        return (d < i0 + G) & live

    def body(state):
        d, _, cs = state
        js = [i0 + g - d for g in range(G)]
        c_in = [jnp.where(js[g] < 0, SB_LOG_DEAD, cs[g]) for g in range(G)]
        accs = [acc_s[g] for g in range(G)]
        pvs, new = visit([jnp.maximum(j, 0) for j in js], c_in, None)
        for g in range(G):
            acc_s[g] = accs[g] + pvs[g]
        c_max = jnp.max(functools.reduce(jnp.maximum, new))
        return d + 1, c_max > SB_LOG_ZERO, tuple(new)

    lax.while_loop(cond, body, (jnp.int32(1), jnp.bool_(True), tuple(cs)))
    for g in range(G):
        acc = acc_s[g]
        o_ref[g * BLK:(g + 1) * BLK, :] = jnp.concatenate([acc[:DH, :BLK], acc[DH:, BLK:]],
                                                          axis=0).T.astype(o_ref.dtype)


def _sb(proj, batch, seq):
    T = proj.shape[0]
    nq = seq // SB_BLOCK
    G = min(SB_QBLOCKS_PER_STEP, nq)
    n_steps = nq // G
    pairs = SB_HEADS // 2
    qo, ko, vo = PM_C // LANES, (PM_C + SB_HEADS * SB_DH) // LANES, (PM_C + 2 * SB_HEADS * SB_DH) // LANES
    return pl.pallas_call(
        _sb_kernel,
        grid=(batch, pairs, n_steps),
        in_specs=[pl.BlockSpec((G * SB_BLOCK, LANES), lambda b, p, i: (b * n_steps + i, qo + p)),
                  pl.BlockSpec((seq, LANES), lambda b, p, i: (b, ko + p)),
                  pl.BlockSpec((seq, LANES), lambda b, p, i: (b, vo + p))],
        out_specs=pl.BlockSpec((G * SB_BLOCK, LANES), lambda b, p, i: (b * n_steps + i, p)),
        out_shape=jax.ShapeDtypeStruct((T, SB_HEADS * SB_DH), BF16),
        scratch_shapes=[pltpu.VMEM((nq, LANES, SB_BLOCK), BF16),
                        pltpu.VMEM((G, LANES, 2 * SB_BLOCK), F32)],
        compiler_params=_cparams("parallel", "parallel", "arbitrary"),
    )(proj, proj, proj)


def _ret_kernel(qk_ref, v_ref, g_ref, cos_ref, sin_ref, nw_ref, o_ref, r_ref):
    tc = qk_ref.shape[0]
    C = RET_CHUNK
    half = RET_DK // 2

    @pl.when(pl.program_id(1) == 0)
    def _():
        r_ref[...] = jnp.zeros_like(r_ref)

    lane = lax.broadcasted_iota(jnp.int32, (C, LANES), 1)
    first = (lane & (RET_DK - 1)) < half
    ri = lax.broadcasted_iota(jnp.int32, (C, C), 0)
    ci = lax.broadcasted_iota(jnp.int32, (C, C), 1)
    diff = (ri - ci).astype(F32)
    causal = ri >= ci
    rowf = lax.broadcasted_iota(jnp.int32, (C, LANES), 0).astype(F32)
    nw = nw_ref[...]

    heads = range(RET_HEADS)
    chunks = range(tc // C)
    ch = [(c, h) for c in chunks for h in heads]
    rows = [slice(c * C, (c + 1) * C) for c in chunks]
    hsl = [slice(h * RET_DV, (h + 1) * RET_DV) for h in heads]
    dmat = [jnp.where(causal, jnp.exp(diff * _RET_LOG_GAMMA[h]), 0.0) for h in heads]
    xi = [jnp.exp((rowf + 1.0) * _RET_LOG_GAMMA[h]) for h in heads]
    zeta = [jnp.exp((C - 1.0 - rowf[:, :RET_DK]) * _RET_LOG_GAMMA[h]) for h in heads]

    def rope(x, c):
        sw = jnp.where(first, pltpu.roll(x, LANES - half, 1), pltpu.roll(x, half, 1))
        return x * cos_ref[rows[c], :] + sw * sin_ref[rows[c], :]

    kw = RET_HEADS * RET_DK
    qp = {(c, p): rope(qk_ref[rows[c], p * LANES:(p + 1) * LANES].astype(F32), c)
          for c in chunks for p in range(RET_HEADS // 2)}
    kp = {(c, p): rope(qk_ref[rows[c], kw + p * LANES:kw + (p + 1) * LANES].astype(F32), c)
          for c in chunks for p in range(RET_HEADS // 2)}
    q = {(c, h): qp[c, h // 2][:, (h % 2) * RET_DK:(h % 2 + 1) * RET_DK].astype(BF16) for c, h in ch}
    k = {(c, h): kp[c, h // 2][:, (h % 2) * RET_DK:(h % 2 + 1) * RET_DK] for c, h in ch}
    vb = {(c, h): v_ref[rows[c], hsl[h]].astype(BF16) for c, h in ch}
    gate = {(c, h): g_ref[rows[c], hsl[h]].astype(F32) for c, h in ch}
    intra = {(c, h): (_dot_nt(q[c, h], k[c, h].astype(BF16)) * dmat[h]).astype(BF16) for c, h in ch}
    delta = {(c, h): _dot((k[c, h] * zeta[h]).T.astype(BF16), vb[c, h]) for c, h in ch}
    r_in = {}
    for h in heads:
        r = r_ref[h]
        for c in chunks:
            r_in[c, h] = r.astype(BF16)
            r = r * math.exp(C * _RET_LOG_GAMMA[h]) + delta[c, h]
        r_ref[h] = r
    o = {(c, h): _dot(intra[c, h], vb[c, h]) + _dot(q[c, h], r_in[c, h]) * xi[h] for c, h in ch}
    for c, h in ch:
        gg = gate[c, h]
        o_ref[rows[c], hsl[h]] = (_rms(o[c, h], nw) * (gg * _sigmoid(gg))).astype(o_ref.dtype)


def _ret(proj, cos_t, sin_t, norm_w, batch, seq, tc):
    T = proj.shape[0]
    n_s = seq // tc
    W = RET_HEADS * RET_DV
    return pl.pallas_call(
        _ret_kernel,
        grid=(batch, n_s),
        in_specs=[pl.BlockSpec((tc, W), lambda b, s: (b * n_s + s, PM_D_QK // W)),
                  pl.BlockSpec((tc, W), lambda b, s: (b * n_s + s, PM_D_V // W)),
                  pl.BlockSpec((tc, W), lambda b, s: (b * n_s + s, PM_D_G // W)),
                  pl.BlockSpec((tc, LANES), lambda b, s: (s, 0)),
                  pl.BlockSpec((tc, LANES), lambda b, s: (s, 0)),
                  pl.BlockSpec((1, RET_DV), lambda b, s: (0, 0))],
        out_specs=pl.BlockSpec((tc, W), lambda b, s: (b * n_s + s, 0)),
        out_shape=jax.ShapeDtypeStruct((T, W), BF16),
        scratch_shapes=[pltpu.VMEM((RET_HEADS, RET_DK, RET_DV), F32)],
        compiler_params=_cparams("parallel", "arbitrary"),
    )(proj, proj, proj, cos_t, sin_t, norm_w)


def _merge_kernel(x_ref, oa_ref, ob0_ref, ob1_ref, ob2_ref, l0_ref, l1_ref, l2_ref, oc_ref, od_ref,
                  ga_ref, gb_ref, gc_ref, gd_ref, wa_ref, wb_ref, wc_ref, wd_ref, wo_ref, nw_ref, out_ref):
    def slabs(ref):
        return jnp.concatenate([ref[t] for t in range(ref.shape[0])], axis=1)

    l0, l1, l2 = slabs(l0_ref), slabs(l1_ref), slabs(l2_ref)
    m = jnp.maximum(jnp.maximum(l0, l1), l2)
    e0, e1, e2 = jnp.exp(l0 - m), jnp.exp(l1 - m), jnp.exp(l2 - m)
    inv = 1.0 / (e0 + e1 + e2)
    ob = jnp.concatenate([slabs(ob0_ref) * (e0 * inv), slabs(ob1_ref) * (e1 * inv), slabs(ob2_ref) * (e2 * inv)],
                         axis=1).astype(BF16)
    half = x_ref.shape[0] // 2
    parts = [slice(0, half), slice(half, 2 * half)]
    da = [_dot(oa_ref[r, :], wa_ref[...]) for r in parts]
    db = [_dot(ob[r], wb_ref[...]) for r in parts]
    dc = [_dot(oc_ref[r, :], wc_ref[...]) for r in parts]
    dd = [_dot(od_ref[r, :], wd_ref[...]) for r in parts]
    ys = [(_sigmoid(ga_ref[r, :].astype(F32)) * da[i] + _sigmoid(gb_ref[r, :].astype(F32)) * db[i]
           + _sigmoid(gc_ref[r, :].astype(F32)) * dc[i] + _sigmoid(gd_ref[r, :].astype(F32)) * dd[i])
          for i, r in enumerate(parts)]
    mixed = [_dot(y.astype(BF16), wo_ref[...]) for y in ys]
    for i, r in enumerate(parts):
        out_ref[r, :] = x_ref[r, :] + _rms(mixed[i], nw_ref[...])


def _merge(x, oa, obs, lses, oc, od, proj, wa, wb, wc, wd, wo, layer, nw, tm):
    T, D = x.shape
    row = lambda w: pl.BlockSpec((tm, w), lambda i: (i, 0))
    gate = lambda br: pl.BlockSpec((tm, D), lambda i: (i, PM_GATES // D + br))
    full = lambda a: pl.BlockSpec((None,) + a.shape[1:], lambda i: (layer, 0, 0))
    return pl.pallas_call(
        _merge_kernel,
        grid=(T // tm,),
        in_specs=[row(D), row(oa.shape[1])]
                 + [pl.BlockSpec((DSWA_OW // LANES, tm, LANES), lambda i: (0, i, 0))] * 6
                 + [row(oc.shape[1]), row(od.shape[1])]
                 + [gate(0), gate(1), gate(2), gate(3)] + [full(wa), full(wb), full(wc), full(wd), full(wo)]
                 + [pl.BlockSpec(nw.shape, lambda i: (0, 0))],
        out_specs=row(D),
        out_shape=jax.ShapeDtypeStruct((T, D), F32),
        compiler_params=_cparams("parallel"),
    )(x, oa, *obs, *lses, oc, od, proj, proj, proj, proj, wa, wb, wc, wd, wo, nw)


def _mlp_kernel(x_ref, n1_ref, w1_ref, w2_ref, n2_ref, out_ref, h_ref, acc_ref):
    f = pl.program_id(1)

    @pl.when(f == 0)
    def _():
        h_ref[...] = _rms(x_ref[...], n1_ref[...]).astype(BF16)
        acc_ref[...] = jnp.zeros_like(acc_ref)

    hid = jnp.maximum(_dot(h_ref[...], w1_ref[...]), 0.0)
    acc_ref[...] += _dot((hid * hid).astype(BF16), w2_ref[...])

    @pl.when(f == pl.num_programs(1) - 1)
    def _():
        out_ref[...] = x_ref[...] + _rms(acc_ref[...], n2_ref[...])


def _mlp(x, n1, w1, w2, layer, n2, tm, tf):
    T, D = x.shape
    F = w1.shape[2]
    return pl.pallas_call(
        _mlp_kernel,
        grid=(T // tm, F // tf),
        in_specs=[pl.BlockSpec((tm, D), lambda i, f: (i, 0)),
                  pl.BlockSpec((1, D), lambda i, f: (0, 0)),
                  pl.BlockSpec((None, D, tf), lambda i, f: (layer, 0, f)),
                  pl.BlockSpec((None, tf, D), lambda i, f: (layer, f, 0)),
                  pl.BlockSpec((1, D), lambda i, f: (0, 0))],
        out_specs=pl.BlockSpec((tm, D), lambda i, f: (i, 0)),
        out_shape=jax.ShapeDtypeStruct((T, D), F32),
        scratch_shapes=[pltpu.VMEM((tm, D), BF16), pltpu.VMEM((tm, D), F32)],
        compiler_params=_cparams("parallel", "arbitrary"),
    )(x, n1, w1, w2, n2)


def _prep_w_in(w_in):
    sec = [w_in[..., _COL_OFF[i]:_COL_OFF[i + 1]] for i in range(len(_COL_SIZES))]
    a_qkv, a_z, a_a, a_b, b_qkv, c_qkv, d_qk, d_v, d_g, gates = sec
    sbw = SB_HEADS * SB_DH
    c_qkv = jnp.concatenate([c_qkv[..., :sbw] * SB_DH ** -0.5, c_qkv[..., sbw:]], axis=-1)
    rw = RET_HEADS * RET_DK
    d_qk = jnp.concatenate([d_qk[..., :rw], d_qk[..., rw:] * RET_DK ** -0.5], axis=-1)
    w_main = jnp.concatenate([a_qkv, a_z, c_qkv, d_qk, d_v, d_g, gates], axis=-1).astype(BF16)
    bw = DSWA_HEADS * DSWA_DH
    gw = DSWA_HPG * DSWA_DH
    lead = w_in.shape[:-1]
    groups = []
    for g in range(len(DSWA_GROUPS)):
        q = b_qkv[..., g * gw:(g + 1) * gw] * DSWA_DH ** -0.5
        k = b_qkv[..., bw + g * gw: bw + (g + 1) * gw]
        v = b_qkv[..., 2 * bw + g * gw: 2 * bw + (g + 1) * gw]
        groups += [q, k, v, jnp.zeros(lead + (DSWA_GW - 3 * gw,), F32)]
    w_b = jnp.concatenate(groups + [a_a, a_b, jnp.zeros(lead + (LANES - 2 * GDN_HEADS,), F32)],
                          axis=-1).astype(BF16)
    return w_main, w_b


def _prep_params(w_in, conv_w, a_log, dt_bias, w_br_a, w_br_b, w_br_c, w_br_d, w_o, w_mlp_in, w_mlp_out):
    n_layers = w_in.shape[0]
    w_main, w_b = _prep_w_in(w_in)
    gw = DSWA_HPG * DSWA_DH
    wb = jnp.concatenate(
        [jnp.concatenate([w_br_b[:, g * gw:(g + 1) * gw], jnp.zeros((n_layers, DSWA_OW - gw, D_MODEL), F32)], axis=1)
         for g in range(len(DSWA_GROUPS))], axis=1).astype(BF16)
    lane_vec = lambda v: jnp.zeros((n_layers, 1, LANES), F32).at[:, 0, :v.shape[1]].set(v.astype(F32))
    return dict(
        w_main=w_main, w_b=w_b,
        cw=jnp.zeros((n_layers, 8, conv_w.shape[2]), F32).at[:, :GDN_CONV].set(conv_w),
        alog=lane_vec(a_log), dtb=lane_vec(dt_bias),
        wa=w_br_a.astype(BF16), wb=wb, wc=w_br_c.astype(BF16), wd=w_br_d.astype(BF16), wo=w_o.astype(BF16),
        w1=w_mlp_in.astype(BF16), w2=w_mlp_out.astype(BF16))


def _rope_tables(seq):
    pos = jnp.arange(seq, dtype=jnp.int32).astype(F32)[:, None]
    inv_b = DSWA_ROPE_THETA ** (-jnp.arange(0, DSWA_ROPE_DIM, 2, dtype=F32) / DSWA_ROPE_DIM)
    ang = pos * inv_b[None, :]
    cb, sb = jnp.cos(ang), jnp.sin(ang)
    rest = DSWA_DH - DSWA_ROPE_DIM
    cos_b = jnp.tile(jnp.concatenate([cb, cb, jnp.ones((seq, rest), F32)], axis=1), (1, LANES // DSWA_DH))
    sin_b = jnp.tile(jnp.concatenate([-sb, sb, jnp.zeros((seq, rest), F32)], axis=1), (1, LANES // DSWA_DH))
    inv_r = RET_THETA ** (-jnp.linspace(0.0, 1.0, RET_DK // 2, dtype=F32))
    ang = pos * inv_r[None, :]
    cr, sr = jnp.cos(ang), jnp.sin(ang)
    cos_r = jnp.tile(jnp.concatenate([cr, cr], axis=1), (1, LANES // RET_DK))
    sin_r = jnp.tile(jnp.concatenate([-sr, sr], axis=1), (1, LANES // RET_DK))
    return cos_b, sin_b, cos_r, sin_r


def _layer(x, batch, seq, tabs, n_pre_mix, n_post_mix, n_pre_mlp, n_post_mlp, gdn_norm, ret_norm, p, layer):
    T = x.shape[0]
    tiles = _tile_sizes(T, seq)
    cos_b, sin_b, cos_r, sin_r = tabs
    gain = n_pre_mix.reshape(1, -1)
    h, g0, g1, g2, ab = _dswa_proj(x, gain, p["w_b"], layer, cos_b, sin_b, seq, tm=tiles["dswa_proj_rows"])
    proj = _matmul(h, p["w_main"], layer, tm=tiles["proj_rows"], tn=tiles["proj_cols"])

    oa = _gdn(proj, ab, p["cw"], p["alog"], p["dtb"], layer, gdn_norm.reshape(1, -1), batch, seq,
              tc=tiles["gdn_rows"])
    obs, lses = [], []
    for grp, (_, dil) in zip((g0, g1, g2), DSWA_GROUPS):
        o, lse = _dswa(grp, batch, seq, dil)
        obs.append(o)
        lses.append(lse)
    oc = _sb(proj, batch, seq)
    od = _ret(proj, cos_r, sin_r, ret_norm.reshape(1, -1), batch, seq, tc=tiles["ret_rows"])

    x1 = _merge(x, oa, obs, lses, oc, od, proj, p["wa"], p["wb"], p["wc"], p["wd"], p["wo"], layer,
                n_post_mix.reshape(1, -1), tm=tiles["merge_rows"])
    return _mlp(x1, n_pre_mlp.reshape(1, -1), p["w1"], p["w2"], layer, n_post_mlp.reshape(1, -1),
                tm=tiles["mlp_rows"], tf=tiles["mlp_hidden"])


def _tile_sizes(n_tokens, seq):
    return dict(
        proj_rows=min(1024, n_tokens), proj_cols=3072,
        dswa_proj_rows=min(1024, seq),
        gdn_rows=min(4 * GDN_CHUNK, seq),
        ret_rows=min(8 * RET_CHUNK, seq),
        merge_rows=min(512, n_tokens),
        mlp_rows=min(1024, n_tokens), mlp_hidden=1024,
    )


def kernel(x, norm_pre_mix, norm_post_mix, norm_pre_mlp, norm_post_mlp, w_in, conv_w, a_log, dt_bias, gdn_norm,
           ret_norm, w_br_a, w_br_b, w_br_c, w_br_d, w_o, w_mlp_in, w_mlp_out):
    batch, seq, d = x.shape
    tabs = _rope_tables(seq)
    params = _prep_params(w_in, conv_w, a_log, dt_bias, w_br_a, w_br_b, w_br_c, w_br_d, w_o, w_mlp_in, w_mlp_out)
    h = x.reshape(batch * seq, d)
    for l in range(norm_pre_mix.shape[0]):
        h = _layer(h, batch, seq, tabs, norm_pre_mix[l], norm_post_mix[l], norm_pre_mlp[l], norm_post_mlp[l],
                   gdn_norm[l], ret_norm[l], params, l)
    return h.reshape(batch, seq, d)
```
